```python
import math
import jax, jax.numpy as jnp
from jax import lax
import numpy as np

D_MODEL = 1024
BATCH = 8
SEQ = 2048
DEPTH = 2
DEC_BATCH = 128
DEC_SEQ = 1
PAST_LEN = 16384
PAGE_SIZE = 128

A_WIDTH = D_MODEL * 3 // 8
A_DK = 64
A_DV = 64
A_HEADS = A_WIDTH // A_DK
B_WIDTH = D_MODEL * 3 // 8
B_DV = 64
B_HEADS = B_WIDTH // B_DV
B_DK = B_DV // 2
B_QK = B_HEADS * B_DK
GLA_RANK = 16
GLA_TAU = 16.0
C_WIDTH = D_MODEL - A_WIDTH - B_WIDTH
C_GROUP = 16
C_GROUPS = C_WIDTH // C_GROUP
C_STATE = 64
MIX_WIDTH = A_WIDTH + B_WIDTH + C_WIDTH
CHUNK = 64
NORM_EPS = 1e-6
IN_SIZES = (A_WIDTH, A_WIDTH, A_WIDTH, A_WIDTH, B_QK, B_QK, B_WIDTH, B_WIDTH, GLA_RANK, C_WIDTH, C_WIDTH)
IN_TOTAL = sum(IN_SIZES)

kernel_name = 'hymba_style_hgrn2_gla_s5_step'


def _split_points():
    pts, acc = [], 0
    for s in IN_SIZES[:-1]:
        acc += s
        pts.append(acc)
    return pts


def rmsnorm(x, g):
    xf = x.astype(jnp.float32)
    r = lax.rsqrt(jnp.mean(xf * xf, axis=-1, keepdims=True) + NORM_EPS)
    return (xf * r * g.astype(jnp.float32)).astype(x.dtype)


def head_rmsnorm(o, g):
    r = lax.rsqrt(jnp.mean(o * o, axis=-1, keepdims=True) + NORM_EPS)
    return o * r * g.astype(jnp.float32).reshape(o.shape[2], o.shape[3])


def gated_recurrence(q, k, v, log_a, s0):
    bsz, t, h, dk = q.shape
    c = min(CHUNK, t)
    n = -(-t // c)
    pad = n * c - t
    if pad:
        pw = ((0, 0), (0, pad), (0, 0), (0, 0))
        q, k, v, log_a = (jnp.pad(q, pw), jnp.pad(k, pw), jnp.pad(v, pw), jnp.pad(log_a, pw))

    def to_chunks(a):
        return a.reshape(bsz, n, c, h, a.shape[-1]).swapaxes(0, 1)

    causal = jnp.tril(jnp.ones((c, c), dtype=bool))

    def step(S, inp):
        qc, kc, vc, gc = inp
        b = jnp.cumsum(gc, axis=1)
        o_inter = jnp.einsum('bthk,bhkv->bthv', qc * jnp.exp(b), S)
        diff = b[:, :, None] - b[:, None]
        decay = jnp.exp(jnp.where(causal[None, :, :, None, None], diff, -jnp.inf))
        att = jnp.einsum('bthk,bshk,btshk->bths', qc, kc, decay)
        o_intra = jnp.einsum('bths,bshv->bthv', att, vc)
        b_last = b[:, -1]
        S_new = jnp.exp(b_last)[..., None] * S + jnp.einsum(
            'bshk,bshv->bhkv', kc * jnp.exp(b_last[:, None] - b), vc)
        return S_new, o_inter + o_intra

    s_fin, o = lax.scan(step, s0, (to_chunks(q), to_chunks(k), to_chunks(v), to_chunks(log_a)))
    o = o.swapaxes(0, 1).reshape(bsz, n * c, h, v.shape[-1])[:, :t]
    return o, s_fin


def _complex_affine_combine(e1, e2):
    a1r, a1i, b1r, b1i = e1
    a2r, a2i, b2r, b2i = e2
    ar = a2r * a1r - a2i * a1i
    ai = a2r * a1i + a2i * a1r
    br = a2r * b1r - a2i * b1i + b2r
    bi = a2r * b1i + a2i * b1r + b2i
    return ar, ai, br, bi


def s5_ssm(u, A_re, A_im, B_re, B_im, C_re, C_im, D, log_dt, x0_re, x0_im):
    f32 = jnp.float32
    dt = jnp.exp(log_dt.astype(f32))[:, None]
    lr, li = A_re.astype(f32), A_im.astype(f32)
    mag = jnp.exp(lr * dt)
    ab_re = mag * jnp.cos(li * dt)
    ab_im = mag * jnp.sin(li * dt)
    den = lr * lr + li * li
    nr = ab_re - 1.0
    co_re = (nr * lr + ab_im * li) / den
    co_im = (ab_im * lr - nr * li) / den
    Br, Bi = B_re.astype(f32), B_im.astype(f32)
    bb_re = co_re[..., None] * Br - co_im[..., None] * Bi
    bb_im = co_re[..., None] * Bi + co_im[..., None] * Br
    bu_re = jnp.einsum('gpc,btgc->btgp', bb_re, u)
    bu_im = jnp.einsum('gpc,btgc->btgp', bb_im, u)
    a_re = jnp.broadcast_to(ab_re, bu_re.shape)
    a_im = jnp.broadcast_to(ab_im, bu_re.shape)
    ac_re, ac_im, h_re, h_im = lax.associative_scan(
        _complex_affine_combine, (a_re, a_im, bu_re, bu_im), axis=1)
    x_re = h_re + ac_re * x0_re[:, None] - ac_im * x0_im[:, None]
    x_im = h_im + ac_re * x0_im[:, None] + ac_im * x0_re[:, None]
    y = (jnp.einsum('gcp,btgp->btgc', C_re.astype(f32), x_re)
         - jnp.einsum('gcp,btgp->btgc', C_im.astype(f32), x_im)
         + D.astype(f32).reshape(C_GROUPS, C_GROUP) * u)
    return y, x_re[:, -1], x_im[:, -1]


def mixer_layer(x, s_hg, s_gla, s_re, s_im, lb, norm_g, w_in, hg_norm_g, gla_w2, gla_b2,
                gla_norm_g, A_re, A_im, B_re, B_im, C_re, C_im, D, log_dt, w_glu, b_glu, w_out):
    f32 = jnp.float32
    bsz, t, _ = x.shape
    h = rmsnorm(x, norm_g)
    p = jnp.einsum('btd,de->bte', h, w_in).astype(f32)
    qa, fa, ia, za, qb, kb, vb, zb, rb, uc, zc = jnp.split(p, _split_points(), axis=-1)

    lbf = lb.astype(f32)
    log_f = jnp.logaddexp(jnp.log(lbf), jnp.log1p(-lbf) + jax.nn.log_sigmoid(fa))
    k_a = (1.0 - lbf) * jax.nn.sigmoid(-fa)
    q_a = jax.nn.silu(qa)
    shp_a = (bsz, t, A_HEADS, A_DK)
    o_a, s_hg_new = gated_recurrence(q_a.reshape(shp_a), k_a.reshape(shp_a),
                                     ia.reshape(bsz, t, A_HEADS, A_DV), log_f.reshape(shp_a),
                                     s_hg.astype(f32))
    o_a = head_rmsnorm(o_a, hg_norm_g).reshape(bsz, t, A_WIDTH) * jax.nn.silu(za)

    log_g = jax.nn.log_sigmoid(rb @ gla_w2.astype(f32) + gla_b2.astype(f32)) / GLA_TAU
    shp_b = (bsz, t, B_HEADS, B_DK)
    o_b, s_gla_new = gated_recurrence((qb * B_DK ** -0.5).reshape(shp_b), kb.reshape(shp_b),
                                      vb.reshape(bsz, t, B_HEADS, B_DV), log_g.reshape(shp_b),
                                      s_gla.astype(f32))
    o_b = head_rmsnorm(o_b, gla_norm_g).reshape(bsz, t, B_WIDTH) * jax.nn.silu(zb)

    y_c, s_re_new, s_im_new = s5_ssm(uc.reshape(bsz, t, C_GROUPS, C_GROUP), A_re, A_im, B_re, B_im,
                                     C_re, C_im, D, log_dt, s_re.astype(f32), s_im.astype(f32))
    y_c = jax.nn.gelu(y_c.reshape(bsz, t, C_WIDTH))
    y_c = y_c * jax.nn.sigmoid(y_c @ w_glu.astype(f32) + b_glu.astype(f32))
    o_c = y_c * jax.nn.silu(zc)

    mix = jnp.concatenate([o_a, o_b, o_c], axis=-1).astype(x.dtype)
    out = jnp.einsum('bte,ed->btd', mix, w_out)
    return x + out.astype(x.dtype), s_hg_new, s_gla_new, s_re_new, s_im_new


def setup_inputs(seed: int = 0) -> dict:
    key = jax.random.key(seed)
    ks = jax.random.split(key, 32)
    f32 = jnp.float32

    def nrm(k, shape, scale):
        return jax.random.normal(k, shape, f32) * scale

    log_dt = jax.random.uniform(ks[20], (DEPTH, C_GROUPS), f32,
                                minval=math.log(1e-3), maxval=math.log(1e-1))
    a_im = math.pi * jnp.arange(C_STATE, dtype=f32)
    return {
        'x_prompt': nrm(ks[0], (BATCH, SEQ, D_MODEL), 1.0),
        'x_sample': nrm(ks[1], (DEC_BATCH, DEC_SEQ, D_MODEL), 1.0),
        'state_hgrn': nrm(ks[2], (DEPTH, DEC_BATCH, A_HEADS, A_DK, A_DV), 0.5),
        'state_gla': nrm(ks[3], (DEPTH, DEC_BATCH, B_HEADS, B_DK, B_DV), 1.0),
        'state_s5_re': nrm(ks[4], (DEPTH, DEC_BATCH, C_GROUPS, C_STATE), 0.5),
        'state_s5_im': nrm(ks[5], (DEPTH, DEC_BATCH, C_GROUPS, C_STATE), 0.5),
        'norm_g': 1.0 + nrm(ks[6], (DEPTH, D_MODEL), 0.02),
        'w_in': nrm(ks[7], (DEPTH, D_MODEL, IN_TOTAL), D_MODEL ** -0.5),
        'hg_lb_logits': nrm(ks[8], (DEPTH, A_WIDTH), 0.5),
        'hg_norm_g': 1.0 + nrm(ks[9], (DEPTH, A_WIDTH), 0.02),
        'gla_w2': nrm(ks[10], (DEPTH, GLA_RANK, B_QK), GLA_RANK ** -0.5),
        'gla_b2': nrm(ks[11], (DEPTH, B_QK), 0.01),
        'gla_norm_g': 1.0 + nrm(ks[12], (DEPTH, B_WIDTH), 0.02),
        's5_A_re': -0.5 + nrm(ks[13], (DEPTH, C_GROUPS, C_STATE), 0.01),
        's5_A_im': a_im + nrm(ks[14], (DEPTH, C_GROUPS, C_STATE), 0.01),
        's5_B_re': nrm(ks[15], (DEPTH, C_GROUPS, C_STATE, C_GROUP), (2 * C_GROUP) ** -0.5),
        's5_B_im': nrm(ks[16], (DEPTH, C_GROUPS, C_STATE, C_GROUP), (2 * C_GROUP) ** -0.5),
        's5_C_re': nrm(ks[17], (DEPTH, C_GROUPS, C_GROUP, C_STATE), C_STATE ** -0.5),
        's5_C_im': nrm(ks[18], (DEPTH, C_GROUPS, C_GROUP, C_STATE), C_STATE ** -0.5),
        's5_D': nrm(ks[19], (DEPTH, C_WIDTH), 1.0),
        's5_log_dt': log_dt,
        's5_w_glu': nrm(ks[21], (DEPTH, C_WIDTH, C_WIDTH), C_WIDTH ** -0.5),
        's5_b_glu': nrm(ks[22], (DEPTH, C_WIDTH), 0.01),
        'w_out': nrm(ks[23], (DEPTH, MIX_WIDTH, D_MODEL), MIX_WIDTH ** -0.5),
        'final_norm_g': 1.0 + nrm(ks[24], (D_MODEL,), 0.02),
    }


def reference(x_prompt, x_sample, state_hgrn, state_gla, state_s5_re, state_s5_im, norm_g, w_in,
              hg_lb_logits, hg_norm_g, gla_w2, gla_b2, gla_norm_g, s5_A_re, s5_A_im, s5_B_re,
              s5_B_im, s5_C_re, s5_C_im, s5_D, s5_log_dt, s5_w_glu, s5_b_glu, w_out, final_norm_g):
    f32 = jnp.float32
    cum = jnp.cumsum(jax.nn.softmax(hg_lb_logits.astype(f32), axis=0), axis=0)
    lb_all = cum - cum[0:1]

    bp = x_prompt.shape[0]
    z_hg = jnp.zeros((bp, A_HEADS, A_DK, A_DV), f32)
    z_gla = jnp.zeros((bp, B_HEADS, B_DK, B_DV), f32)
    z_s5 = jnp.zeros((bp, C_GROUPS, C_STATE), f32)

    yp, ys = x_prompt, x_sample
    hg_p, gla_p, re_p, im_p = [], [], [], []
    hg_s, gla_s, re_s, im_s = [], [], [], []
    for l in range(DEPTH):
        w = (norm_g[l], w_in[l], hg_norm_g[l], gla_w2[l], gla_b2[l], gla_norm_g[l],
             s5_A_re[l], s5_A_im[l], s5_B_re[l], s5_B_im[l], s5_C_re[l], s5_C_im[l],
             s5_D[l], s5_log_dt[l], s5_w_glu[l], s5_b_glu[l], w_out[l])
        yp, a1, a2, a3, a4 = mixer_layer(yp, z_hg, z_gla, z_s5, z_s5, lb_all[l], *w)
        ys, b1, b2, b3, b4 = mixer_layer(ys, state_hgrn[l], state_gla[l], state_s5_re[l],
                                         state_s5_im[l], lb_all[l], *w)
        hg_p.append(a1); gla_p.append(a2); re_p.append(a3); im_p.append(a4)
        hg_s.append(b1); gla_s.append(b2); re_s.append(b3); im_s.append(b4)

    y_prompt = rmsnorm(yp, final_norm_g)
    y_sample = rmsnorm(ys, final_norm_g)
    hgrn_prompt = jnp.stack(hg_p, axis=0)
    gla_prompt = jnp.stack(gla_p, axis=0)
    s5re_prompt = jnp.stack(re_p, axis=0)
    s5im_prompt = jnp.stack(im_p, axis=0)
    hgrn_sample = jnp.stack(hg_s, axis=0)
    gla_sample = jnp.stack(gla_s, axis=0)
    s5re_sample = jnp.stack(re_s, axis=0)
    s5im_sample = jnp.stack(im_s, axis=0)
    return (y_prompt, y_sample, hgrn_prompt, gla_prompt, s5re_prompt, s5im_prompt,
            hgrn_sample, gla_sample, s5re_sample, s5im_sample)
```

```python
import functools

import numpy as np
import jax
import jax.numpy as jnp
from jax import lax
from jax.experimental import pallas as pl
from jax.experimental.pallas import tpu as pltpu

F32 = jnp.float32
BF16 = jnp.bfloat16

D_MODEL = 1024
A_WIDTH, A_DK, A_DV, A_HEADS = 384, 64, 64, 6
B_WIDTH, B_DV, B_HEADS, B_DK, B_QK = 384, 64, 6, 32, 192
GLA_RANK = 16
GLA_TAU = 16.0
C_WIDTH, C_GROUP, C_GROUPS, C_STATE = 256, 16, 16, 64
NORM_EPS = 1e-6
IN_SIZES = (A_WIDTH, A_WIDTH, A_WIDTH, A_WIDTH, B_QK, B_QK, B_WIDTH, B_WIDTH, GLA_RANK, C_WIDTH, C_WIDTH)
IN_TOTAL = sum(IN_SIZES)
(O_QA, O_FA, O_IA, O_ZA, O_QB, O_KB, O_VB, O_ZB, O_RB, O_UC, O_ZC) = (
    int(v) for v in np.concatenate([[0], np.cumsum(IN_SIZES)[:-1]]))

LANES = 128
N_LEVELS = 4
DIAG = LANES >> N_LEVELS
N_WIN = N_LEVELS + 2
SCAN_STEPS = 7
VMEM_LIMIT = 56 * 1024 * 1024


def _chunk_tables():
    s = np.arange(LANES)[:, None]
    t = np.arange(LANES)[None, :]
    wins = []
    for l in range(N_LEVELS):
        h = (LANES // 2) >> l
        mid = (t // (2 * h)) * (2 * h) + h - 1
        role_q = (t & h) != 0
        wins.append(np.where(role_q, (s > mid) & (s <= t), (s > t) & (s <= mid)))
    wins.append(s <= t)
    wins.append(s > t)
    win = np.concatenate(wins, axis=1).astype(np.float32)
    x = s ^ t
    lv = np.full((LANES, LANES), -1, np.int32)
    for l in range(N_LEVELS):
        h = (LANES // 2) >> l
        lv[(x >= h) & (x < 2 * h) & (t > s)] = l
    return win, lv


def _dot(a, b):
    return jnp.dot(a, b, preferred_element_type=F32)


def _dot_nt(a, b):
    return lax.dot_general(a, b, (((1,), (1,)), ((), ())), preferred_element_type=F32)


def _rmsnorm_rows(x, g):
    r = lax.rsqrt(jnp.mean(x * x, axis=-1, keepdims=True) + NORM_EPS)
    return x * r * g


def _lower_bound_consts(lbl_ref, layer, out_ref):
    rows = [lbl_ref[l] for l in range(lbl_ref.shape[0])]
    m = functools.reduce(jnp.maximum, rows)
    es = [jnp.exp(r - m) for r in rows]
    tot = functools.reduce(lambda a, b: a + b, es)
    sm = [e / tot for e in es]
    cum = sm[0]
    for l in range(1, layer + 1):
        cum = cum + sm[l]
    lb = cum - sm[0]
    out_ref[0] = jnp.log(lb)
    out_ref[1] = jnp.log(1.0 - lb)
    out_ref[2] = 1.0 - lb


def _hgrn_gates(fa, lbc_ref):
    e = jnp.exp(-jnp.abs(fa))
    logsig = jnp.minimum(fa, 0.0) - jnp.log(1.0 + e)
    a = lbc_ref[0]
    c = lbc_ref[1] + logsig
    log_f = jnp.maximum(a, c) + jnp.log(1.0 + jnp.exp(-jnp.abs(a - c)))
    k_a = lbc_ref[2] * (jnp.where(fa >= 0.0, e, 1.0) / (1.0 + e))
    return log_f, k_a


def _gla_log_gate(rb, w2t_ref, b2_ref):
    pre = _dot(w2t_ref[...], rb.astype(BF16)) + b2_ref[...]
    return jax.nn.log_sigmoid(pre) / GLA_TAU


def _window_sums(g, win_ref):
    hi = g.astype(BF16)
    lo = (g - hi.astype(F32)).astype(BF16)
    w = win_ref[...]
    return _dot(hi, w) + _dot(lo, w)


def _gated_chunk_head(q, k, v, e_ref, r0, dk, s_ref, lv, lane):
    rows = pl.ds(r0, dk)
    bq = e_ref[rows, N_LEVELS * LANES:(N_LEVELS + 1) * LANES]
    bk = e_ref[rows, (N_LEVELS + 1) * LANES:(N_LEVELS + 2) * LANES]
    s_old = s_ref[rows, :]
    qs = q * jnp.exp(bq)
    o = _dot(s_old.T.astype(BF16), qs.astype(BF16))
    kd = k * jnp.exp(bk)
    s_ref[rows, :] = jnp.exp(bq[:, LANES - 1:LANES]) * s_old + _dot_nt(kd.astype(BF16), v.astype(BF16))
    att = jnp.zeros((LANES, LANES), F32)
    for l in range(N_LEVELS):
        h = (LANES // 2) >> l
        z = jnp.where((lane & h) != 0, q, k) * jnp.exp(e_ref[rows, l * LANES:(l + 1) * LANES])
        gram = _dot(z.T.astype(BF16), z.astype(BF16))
        att = jnp.where(lv == l, gram, att)
    o = o + _dot(v.astype(BF16), att.astype(BF16))
    od = jnp.sum(q * k, axis=0, keepdims=True) * v
    pos = lane & (DIAG - 1)
    for j in range(1, DIAG):
        kr = pltpu.roll(k, j, 1)
        br = pltpu.roll(bq, j, 1)
        vr = pltpu.roll(v, j, 1)
        w = jnp.sum(q * kr * jnp.exp(jnp.minimum(bq - br, 0.0)), axis=0, keepdims=True)
        od = od + jnp.where(pos[:1] >= j, w, 0.0) * vr
    return o + od


def _head_norm_gate(o, gain, z):
    r = lax.rsqrt(jnp.mean(o * o, axis=0, keepdims=True) + NORM_EPS)
    return o * r * gain * jax.nn.silu(z)


def _s5_readout_gate(y, zc, wglu_ref, bglu_ref):
    y = jax.nn.gelu(y)
    y = y * jax.nn.sigmoid(_dot(wglu_ref[...], y.astype(BF16)) + bglu_ref[...])
    return y * jax.nn.silu(zc)


def _project(x, ng_ref, wint_ref, pt_ref):
    h = _rmsnorm_rows(x, ng_ref[...])
    pt_ref[...] = _dot_nt(wint_ref[...], h.astype(BF16))


def _out_project(x, mix_ref, wout_ref, fg_ref, final):
    out = x + _dot(mix_ref[...].T.astype(BF16), wout_ref[...])
    if final:
        out = _rmsnorm_rows(out, fg_ref[...])
    return out


def _prompt_kernel(layer, final,
                   x_ref, ng_ref, wint_ref, lbl_ref, hgain_ref, w2t_ref, b2_ref, ggain_ref,
                   bb_ref, cc_ref, dd_ref, pw_ref, pcar_ref, wglu_ref, bglu_ref, wout_ref, fg_ref,
                   win_ref, lv_ref,
                   y_ref, shg_ref, sgla_ref, sre_ref, sim_ref,
                   pt_ref, ea_ref, eb_ref, ka_ref, mix_ref, hg_s, gla_s, cre_s, cim_s, lbc_s):
    t_idx = pl.program_id(1)

    @pl.when(t_idx == 0)
    def _():
        hg_s[...] = jnp.zeros_like(hg_s)
        gla_s[...] = jnp.zeros_like(gla_s)
        cre_s[...] = jnp.zeros_like(cre_s)
        cim_s[...] = jnp.zeros_like(cim_s)
        _lower_bound_consts(lbl_ref, layer, lbc_s)

    x = x_ref[0]
    _project(x, ng_ref, wint_ref, pt_ref)

    lane = lax.broadcasted_iota(jnp.int32, (8, LANES), 1)
    lv = lv_ref[...]

    log_f, k_a = _hgrn_gates(pt_ref[O_FA:O_FA + A_WIDTH, :], lbc_s)
    ka_ref[...] = k_a
    ea_ref[...] = _window_sums(log_f, win_ref)
    for h in range(A_HEADS):
        r0 = h * A_DK
        q = jax.nn.silu(pt_ref[O_QA + r0:O_QA + r0 + A_DK, :])
        k = ka_ref[r0:r0 + A_DK, :]
        v = pt_ref[O_IA + r0:O_IA + r0 + A_DV, :]
        o = _gated_chunk_head(q, k, v, ea_ref, r0, A_DK, hg_s, lv, lane[:1])
        mix_ref[r0:r0 + A_DV, :] = _head_norm_gate(
            o, hgain_ref[r0:r0 + A_DV, :], pt_ref[O_ZA + r0:O_ZA + r0 + A_DV, :])

    log_g = _gla_log_gate(pt_ref[O_RB:O_RB + GLA_RANK, :], w2t_ref, b2_ref)
    eb_ref[...] = _window_sums(log_g, win_ref)
    for h in range(B_HEADS):
        r0 = h * B_DK
        v0 = h * B_DV
        q = pt_ref[O_QB + r0:O_QB + r0 + B_DK, :] * (B_DK ** -0.5)
        k = pt_ref[O_KB + r0:O_KB + r0 + B_DK, :]
        v = pt_ref[O_VB + v0:O_VB + v0 + B_DV, :]
        o = _gated_chunk_head(q, k, v, eb_ref, r0, B_DK, gla_s, lv, lane[:1])
        mix_ref[A_WIDTH + v0:A_WIDTH + v0 + B_DV, :] = _head_norm_gate(
            o, ggain_ref[v0:v0 + B_DV, :], pt_ref[O_ZB + v0:O_ZB + v0 + B_DV, :])

    for g in range(C_GROUPS):
        c0 = g * C_GROUP
        p0 = g * C_STATE
        u = pt_ref[O_UC + c0:O_UC + c0 + C_GROUP, :]
        bu = _dot(bb_ref[g], u.astype(BF16))
        re, im = bu[:C_STATE], bu[C_STATE:]
        for j in range(SCAN_STEPS):
            d = 1 << j
            sre = pltpu.roll(re, d, 1)
            sim = pltpu.roll(im, d, 1)
            pr = pw_ref[j, 0, p0:p0 + C_STATE, :]
            pi = pw_ref[j, 1, p0:p0 + C_STATE, :]
            re, im = re + pr * sre - pi * sim, im + pr * sim + pi * sre
        pr = pcar_ref[0, p0:p0 + C_STATE, :]
        pi = pcar_ref[1, p0:p0 + C_STATE, :]
        cr = cre_s[p0:p0 + C_STATE, :]
        ci = cim_s[p0:p0 + C_STATE, :]
        re, im = re + pr * cr - pi * ci, im + pr * ci + pi * cr
        cre_s[p0:p0 + C_STATE, :] = jnp.broadcast_to(re[:, LANES - 1:], (C_STATE, LANES))
        cim_s[p0:p0 + C_STATE, :] = jnp.broadcast_to(im[:, LANES - 1:], (C_STATE, LANES))
        xcat = jnp.concatenate([re, im], axis=0).astype(BF16)
        y = _dot(cc_ref[g], xcat) + dd_ref[c0:c0 + C_GROUP, :] * u
        mix_ref[A_WIDTH + B_WIDTH + c0:A_WIDTH + B_WIDTH + c0 + C_GROUP, :] = y
    yc = mix_ref[A_WIDTH + B_WIDTH:, :]
    mix_ref[A_WIDTH + B_WIDTH:, :] = _s5_readout_gate(yc, pt_ref[O_ZC:O_ZC + C_WIDTH, :], wglu_ref, bglu_ref)

    y_ref[0] = _out_project(x, mix_ref, wout_ref, fg_ref, final)

    @pl.when(t_idx == pl.num_programs(1) - 1)
    def _():
        for h in range(A_HEADS):
            shg_ref[0, h] = hg_s[h * A_DK:(h + 1) * A_DK, :]
        for h in range(B_HEADS):
            sgla_ref[0, h] = gla_s[h * B_DK:(h + 1) * B_DK, :]
        sre_ref[0] = cre_s[:, :1]
        sim_ref[0] = cim_s[:, :1]


def _sample_kernel(layer, final,
                   x_ref, ng_ref, wint_ref, lbl_ref, hgain_ref, w2t_ref, b2_ref, ggain_ref,
                   bb_ref, cc_ref, dd_ref, ab_ref, wglu_ref, bglu_ref, wout_ref, fg_ref,
                   shg_in, sgla_in, sre_in, sim_in,
                   y_ref, shg_out, sgla_out, sre_out, sim_out,
                   pt_ref, fa_s, ka_s, qa_s, gb_s, mix_ref, lbc_s):
    h = pl.program_id(0)

    @pl.when(h == 0)
    def _():
        _lower_bound_consts(lbl_ref, layer, lbc_s)
        _project(x_ref[...], ng_ref, wint_ref, pt_ref)
        log_f, k_a = _hgrn_gates(pt_ref[O_FA:O_FA + A_WIDTH, :], lbc_s)
        fa_s[...] = jnp.exp(log_f)
        ka_s[...] = k_a
        qa_s[...] = jax.nn.silu(pt_ref[O_QA:O_QA + A_WIDTH, :])
        gb_s[...] = jnp.exp(_gla_log_gate(pt_ref[O_RB:O_RB + GLA_RANK, :], w2t_ref, b2_ref))

    ra = pl.multiple_of(h * A_DK, A_DK)
    v_a = pt_ref[pl.ds(O_IA + ra, A_DV), :]

    def hg_body(k, o):
        r = ra + k
        rows = pl.ds(pl.multiple_of(k * A_DV, A_DV), A_DV)
        new = fa_s[pl.ds(r, 1), :] * shg_in[rows, :] + ka_s[pl.ds(r, 1), :] * v_a
        shg_out[rows, :] = new
        return o + qa_s[pl.ds(r, 1), :] * new

    o = lax.fori_loop(0, A_DK, hg_body, jnp.zeros((A_DV, LANES), F32))
    mix_ref[pl.ds(ra, A_DV), :] = _head_norm_gate(
        o, hgain_ref[pl.ds(ra, A_DV), :], pt_ref[pl.ds(O_ZA + ra, A_DV), :])

    rb = pl.multiple_of(h * B_DK, B_DK)
    vb0 = pl.multiple_of(h * B_DV, B_DV)
    v_b = pt_ref[pl.ds(O_VB + vb0, B_DV), :]

    def gla_body(k, o):
        r = rb + k
        rows = pl.ds(pl.multiple_of(k * B_DV, B_DV), B_DV)
        new = gb_s[pl.ds(r, 1), :] * sgla_in[rows, :] + pt_ref[pl.ds(O_KB + r, 1), :] * v_b
        sgla_out[rows, :] = new
        return o + (pt_ref[pl.ds(O_QB + r, 1), :] * (B_DK ** -0.5)) * new

    o = lax.fori_loop(0, B_DK, gla_body, jnp.zeros((B_DV, LANES), F32))
    mix_ref[pl.ds(A_WIDTH + vb0, B_DV), :] = _head_norm_gate(
        o, ggain_ref[pl.ds(vb0, B_DV), :], pt_ref[pl.ds(O_ZB + vb0, B_DV), :])

    @pl.when(h == pl.num_programs(0) - 1)
    def _():
        for g in range(C_GROUPS):
            c0 = g * C_GROUP
            p0 = g * C_STATE
            u = pt_ref[O_UC + c0:O_UC + c0 + C_GROUP, :]
            bu = _dot(bb_ref[g], u.astype(BF16))
            ar = ab_ref[0, p0:p0 + C_STATE, :]
            ai = ab_ref[1, p0:p0 + C_STATE, :]
            xr = sre_in[p0:p0 + C_STATE, :]
            xi = sim_in[p0:p0 + C_STATE, :]
            re = bu[:C_STATE] + ar * xr - ai * xi
            im = bu[C_STATE:] + ar * xi + ai * xr
            sre_out[p0:p0 + C_STATE, :] = re
            sim_out[p0:p0 + C_STATE, :] = im
            xcat = jnp.concatenate([re, im], axis=0).astype(BF16)
            y = _dot(cc_ref[g], xcat) + dd_ref[c0:c0 + C_GROUP, :] * u
            mix_ref[A_WIDTH + B_WIDTH + c0:A_WIDTH + B_WIDTH + c0 + C_GROUP, :] = y
        yc = mix_ref[A_WIDTH + B_WIDTH:, :]
        mix_ref[A_WIDTH + B_WIDTH:, :] = _s5_readout_gate(
            yc, pt_ref[O_ZC:O_ZC + C_WIDTH, :], wglu_ref, bglu_ref)
        y_ref[...] = _out_project(x_ref[...], mix_ref, wout_ref, fg_ref, final)


def _lane_bcast(v):
    return jnp.broadcast_to(v.astype(F32)[..., None], v.shape + (LANES,))


def _powers(ar, ai, n):
    pr, pi = ar[..., None], ai[..., None]
    while pr.shape[-1] < n:
        lr, li = pr[..., -1:], pi[..., -1:]
        pr, pi = (jnp.concatenate([pr, pr * lr - pi * li], axis=-1),
                  jnp.concatenate([pi, pr * li + pi * lr], axis=-1))
    return pr[..., :n], pi[..., :n]


def _s5_tables(A_re, A_im, B_re, B_im, C_re, C_im, log_dt):
    dt = jnp.exp(log_dt.astype(F32))[:, None]
    lr, li = A_re.astype(F32), A_im.astype(F32)
    mag = jnp.exp(lr * dt)
    ab_re = mag * jnp.cos(li * dt)
    ab_im = mag * jnp.sin(li * dt)
    den = lr * lr + li * li
    nr = ab_re - 1.0
    co_re = (nr * lr + ab_im * li) / den
    co_im = (ab_im * lr - nr * li) / den
    Br, Bi = B_re.astype(F32), B_im.astype(F32)
    bb_re = co_re[..., None] * Br - co_im[..., None] * Bi
    bb_im = co_re[..., None] * Bi + co_im[..., None] * Br
    bb = jnp.concatenate([bb_re, bb_im], axis=1).astype(BF16)
    cc = jnp.concatenate([C_re.astype(F32), -C_im.astype(F32)], axis=2).astype(BF16)
    pr, pi = _powers(ab_re.reshape(-1), ab_im.reshape(-1), LANES)
    pcar = jnp.stack([pr, pi])
    lane = jnp.arange(LANES)
    pw = []
    for j in range(SCAN_STEPS):
        d = 1 << j
        keep = (lane >= d)[None, :]
        pw.append(jnp.stack([jnp.where(keep, pr[:, d - 1:d], 0.0), jnp.where(keep, pi[:, d - 1:d], 0.0)]))
    pw = jnp.stack(pw)
    ab = jnp.stack([_lane_bcast(ab_re.reshape(-1)), _lane_bcast(ab_im.reshape(-1))])
    return bb, cc, pw, pcar, ab


def _const_spec(a):
    nd = a.ndim
    return pl.BlockSpec(a.shape, lambda *_, nd=nd: (0,) * nd)


def _layer_operands(l, norm_g, w_in, hg_lb_logits, hg_norm_g, gla_w2, gla_b2, gla_norm_g,
                    s5_A_re, s5_A_im, s5_B_re, s5_B_im, s5_C_re, s5_C_im, s5_D, s5_log_dt,
                    s5_w_glu, s5_b_glu, w_out, final_norm_g):
    bb, cc, pw, pcar, ab = _s5_tables(s5_A_re[l], s5_A_im[l], s5_B_re[l], s5_B_im[l],
                                      s5_C_re[l], s5_C_im[l], s5_log_dt[l])
    common_a = (
        norm_g[l].astype(F32).reshape(1, D_MODEL),
        w_in[l].T.astype(BF16),
        _lane_bcast(hg_lb_logits),
        _lane_bcast(hg_norm_g[l]),
        gla_w2[l].T.astype(BF16),
        _lane_bcast(gla_b2[l]),
        _lane_bcast(gla_norm_g[l]),
        bb, cc, _lane_bcast(s5_D[l]),
    )
    common_b = (
        s5_w_glu[l].T.astype(BF16),
        _lane_bcast(s5_b_glu[l]),
        w_out[l].astype(BF16),
        final_norm_g.astype(F32).reshape(1, D_MODEL),
    )
    return common_a, (pw, pcar), (ab,), common_b


def _prompt_layer(layer, final, x, ops_a, ops_scan, ops_b, win, lv):
    bsz, t, _ = x.shape
    assert t % LANES == 0
    consts = ops_a + ops_scan + ops_b + (win, lv)
    out_shape = (
        jax.ShapeDtypeStruct((bsz, t, D_MODEL), F32),
        jax.ShapeDtypeStruct((bsz, A_HEADS, A_DK, A_DV), F32),
        jax.ShapeDtypeStruct((bsz, B_HEADS, B_DK, B_DV), F32),
        jax.ShapeDtypeStruct((bsz, C_GROUPS * C_STATE, 1), F32),
        jax.ShapeDtypeStruct((bsz, C_GROUPS * C_STATE, 1), F32),
    )
    x_spec = pl.BlockSpec((1, LANES, D_MODEL), lambda b, i: (b, i, 0))
    out_specs = (
        x_spec,
        pl.BlockSpec((1, A_HEADS, A_DK, A_DV), lambda b, i: (b, 0, 0, 0)),
        pl.BlockSpec((1, B_HEADS, B_DK, B_DV), lambda b, i: (b, 0, 0, 0)),
        pl.BlockSpec((1, C_GROUPS * C_STATE, 1), lambda b, i: (b, 0, 0)),
        pl.BlockSpec((1, C_GROUPS * C_STATE, 1), lambda b, i: (b, 0, 0)),
    )
    scratch = [
        pltpu.VMEM((IN_TOTAL, LANES), F32),
        pltpu.VMEM((A_WIDTH, N_WIN * LANES), F32),
        pltpu.VMEM((B_QK, N_WIN * LANES), F32),
        pltpu.VMEM((A_WIDTH, LANES), F32),
        pltpu.VMEM((D_MODEL, LANES), F32),
        pltpu.VMEM((A_WIDTH, A_DV), F32),
        pltpu.VMEM((B_QK, B_DV), F32),
        pltpu.VMEM((C_GROUPS * C_STATE, LANES), F32),
        pltpu.VMEM((C_GROUPS * C_STATE, LANES), F32),
        pltpu.VMEM((3, A_WIDTH, LANES), F32),
    ]
    return pl.pallas_call(
        functools.partial(_prompt_kernel, layer, final),
        grid=(bsz, t // LANES),
        in_specs=[x_spec] + [_const_spec(a) for a in consts],
        out_specs=out_specs,
        out_shape=out_shape,
        scratch_shapes=scratch,
        compiler_params=pltpu.CompilerParams(
            dimension_semantics=("arbitrary", "arbitrary"), vmem_limit_bytes=VMEM_LIMIT),
    )(x, *consts)


def _sample_layer(layer, final, xs, ops_a, ops_ab, ops_b, shg, sgla, sre, sim):
    n = xs.shape[0]
    assert n == LANES
    consts = ops_a + ops_ab + ops_b
    hg_rows, gla_rows = A_DK * A_DV, B_DK * B_DV
    s5_rows = C_GROUPS * C_STATE
    out_shape = (
        jax.ShapeDtypeStruct((n, D_MODEL), F32),
        jax.ShapeDtypeStruct((A_HEADS * hg_rows, LANES), F32),
        jax.ShapeDtypeStruct((B_HEADS * gla_rows, LANES), F32),
        jax.ShapeDtypeStruct((s5_rows, LANES), F32),
        jax.ShapeDtypeStruct((s5_rows, LANES), F32),
    )
    full = lambda shape: pl.BlockSpec(shape, lambda h: (0, 0))
    state_specs = [
        pl.BlockSpec((hg_rows, LANES), lambda h: (h, 0)),
        pl.BlockSpec((gla_rows, LANES), lambda h: (h, 0)),
        full((s5_rows, LANES)),
        full((s5_rows, LANES)),
    ]
    scratch = [
        pltpu.VMEM((IN_TOTAL, LANES), F32),
        pltpu.VMEM((A_WIDTH, LANES), F32),
        pltpu.VMEM((A_WIDTH, LANES), F32),
        pltpu.VMEM((A_WIDTH, LANES), F32),
        pltpu.VMEM((B_QK, LANES), F32),
        pltpu.VMEM((D_MODEL, LANES), F32),
        pltpu.VMEM((3, A_WIDTH, LANES), F32),
    ]
    return pl.pallas_call(
        functools.partial(_sample_kernel, layer, final),
        grid=(A_HEADS,),
        in_specs=[full((n, D_MODEL))] + [_const_spec(a) for a in consts] + state_specs,
        out_specs=[full((n, D_MODEL))] + state_specs,
        out_shape=out_shape,
        scratch_shapes=scratch,
        compiler_params=pltpu.CompilerParams(
            dimension_semantics=("arbitrary",), vmem_limit_bytes=VMEM_LIMIT),
    )(xs, *consts, shg, sgla, sre, sim)


def kernel(x_prompt, x_sample, state_hgrn, state_gla, state_s5_re, state_s5_im, norm_g, w_in, hg_lb_logits, hg_norm_g, gla_w2, gla_b2, gla_norm_g, s5_A_re, s5_A_im, s5_B_re, s5_B_im, s5_C_re, s5_C_im, s5_D, s5_log_dt, s5_w_glu, s5_b_glu, w_out, final_norm_g):
    depth = w_in.shape[0]
    bsz = x_prompt.shape[0]
    n = x_sample.shape[0]
    assert x_sample.shape[1] == 1
    win_np, lv_np = _chunk_tables()
    win = jnp.asarray(win_np, BF16)
    lv = jnp.asarray(lv_np)

    yp = x_prompt.astype(F32)
    ys = x_sample.astype(F32).reshape(n, D_MODEL)
    prompt_states, sample_states = [], []
    for l in range(depth):
        final = l == depth - 1
        ops_a, ops_scan, ops_ab, ops_b = _layer_operands(
            l, norm_g, w_in, hg_lb_logits, hg_norm_g, gla_w2, gla_b2, gla_norm_g,
            s5_A_re, s5_A_im, s5_B_re, s5_B_im, s5_C_re, s5_C_im, s5_D, s5_log_dt,
            s5_w_glu, s5_b_glu, w_out, final_norm_g)
        yp, hg_p, gla_p, re_p, im_p = _prompt_layer(l, final, yp, ops_a, ops_scan, ops_b, win, lv)
        prompt_states.append((hg_p, gla_p,
                              re_p.reshape(bsz, C_GROUPS, C_STATE), im_p.reshape(bsz, C_GROUPS, C_STATE)))
        to_cm = lambda s: s.astype(F32).reshape(n, -1).T
        ys, hg_s, gla_s, re_s, im_s = _sample_layer(
            l, final, ys, ops_a, ops_ab, ops_b,
            to_cm(state_hgrn[l]), to_cm(state_gla[l]), to_cm(state_s5_re[l]), to_cm(state_s5_im[l]))
        sample_states.append((hg_s.T.reshape(n, A_HEADS, A_DK, A_DV),
                              gla_s.T.reshape(n, B_HEADS, B_DK, B_DV),
                              re_s.T.reshape(n, C_GROUPS, C_STATE),
                              im_s.T.reshape(n, C_GROUPS, C_STATE)))

    stack = lambda states, i: jnp.stack([s[i] for s in states], axis=0)
    return (yp, ys.reshape(n, 1, D_MODEL),
            stack(prompt_states, 0), stack(prompt_states, 1), stack(prompt_states, 2), stack(prompt_states, 3),
            stack(sample_states, 0), stack(sample_states, 1), stack(sample_states, 2), stack(sample_states, 3))
```

```python
import functools

import numpy as np
import jax
import jax.numpy as jnp
from jax import lax
from jax.experimental import pallas as pl
from jax.experimental.pallas import tpu as pltpu

F32 = jnp.float32
BF16 = jnp.bfloat16

D_MODEL = 1024
A_WIDTH, A_DK, A_DV, A_HEADS = 384, 64, 64, 6
B_WIDTH, B_DV, B_HEADS, B_DK, B_QK = 384, 64, 6, 32, 192
GLA_RANK = 16
GLA_TAU = 16.0
C_WIDTH, C_GROUP, C_GROUPS, C_STATE = 256, 16, 16, 64
S5_DIM = C_GROUPS * C_STATE
NORM_EPS = 1e-6
IN_SIZES = (A_WIDTH, A_WIDTH, A_WIDTH, A_WIDTH, B_QK, B_QK, B_WIDTH, B_WIDTH, GLA_RANK, C_WIDTH, C_WIDTH)
IN_TOTAL = sum(IN_SIZES)
(O_QA, O_FA, O_IA, O_ZA, O_QB, O_KB, O_VB, O_ZB, O_RB, O_UC, O_ZC) = (
    int(v) for v in np.concatenate([[0], np.cumsum(IN_SIZES)[:-1]]))

LANES = 128
SUBLANES = 8
N_LEVELS = 7
N_WIN = N_LEVELS + 2
SEG = LANES // SUBLANES
VMEM_LIMIT = 56 * 1024 * 1024


def _chunk_tables():
    s = np.arange(LANES)[:, None]
    t = np.arange(LANES)[None, :]
    wins = []
    for l in range(N_LEVELS):
        h = (LANES // 2) >> l
        mid = (t // (2 * h)) * (2 * h) + h - 1
        role_q = (t & h) != 0
        wins.append(np.where(role_q, (s > mid) & (s <= t), (s > t) & (s <= mid)))
    wins.append(s <= t)
    wins.append(s > t)
    win = np.concatenate(wins, axis=1).astype(np.float32)
    x = s ^ t
    lv = np.full((LANES, LANES), -1, np.int32)
    for l in range(N_LEVELS):
        h = (LANES // 2) >> l
        lv[(x >= h) & (x < 2 * h) & (t > s)] = l
    return win, np.tile(lv, (1, LANES // B_DK))


def _dot(a, b):
    return jnp.dot(a, b, preferred_element_type=F32)


def _dot_nt(a, b):
    return lax.dot_general(a, b, (((1,), (1,)), ((), ())), preferred_element_type=F32)


def _rmsnorm_rows(x, g):
    r = lax.rsqrt(jnp.mean(x * x, axis=-1, keepdims=True) + NORM_EPS)
    return x * r * g


def _lower_bound_consts(lbl_ref, layer, out_ref):
    rows = [lbl_ref[l] for l in range(lbl_ref.shape[0])]
    m = functools.reduce(jnp.maximum, rows)
    es = [jnp.exp(r - m) for r in rows]
    tot = functools.reduce(lambda a, b: a + b, es)
    sm = [e / tot for e in es]
    cum = sm[0]
    for l in range(1, layer + 1):
        cum = cum + sm[l]
    lb = cum - sm[0]
    out_ref[0] = jnp.log(lb)
    out_ref[1] = jnp.log(1.0 - lb)
    out_ref[2] = 1.0 - lb


def _hgrn_gates(fa, lbc_ref):
    e = jnp.exp(-jnp.abs(fa))
    logsig = jnp.minimum(fa, 0.0) - jnp.log(1.0 + e)
    a = lbc_ref[0]
    c = lbc_ref[1] + logsig
    log_f = jnp.maximum(a, c) + jnp.log(1.0 + jnp.exp(-jnp.abs(a - c)))
    k_a = lbc_ref[2] * (jnp.where(fa >= 0.0, e, 1.0) / (1.0 + e))
    return log_f, k_a


def _gla_log_gate(rb, w2t_ref, b2_ref):
    pre = _dot(w2t_ref[...], rb.astype(BF16)) + b2_ref[...]
    return jax.nn.log_sigmoid(pre) / GLA_TAU


def _window_sums(g, win_ref):
    hi = g.astype(BF16)
    lo = (g - hi.astype(F32)).astype(BF16)
    w = win_ref[...]
    return _dot(hi, w) + _dot(lo, w)


def _block_diag(blocks):
    n = len(blocks)
    zero = jnp.zeros_like(blocks[0])
    return jnp.concatenate(
        [jnp.concatenate([b if j == i else zero for j in range(n)], axis=1) for i, b in enumerate(blocks)],
        axis=0)


def _gated_chunk(q, k, v, e_ref, s_ref, heads, dk, dv, lv_ref, lane):
    bq = e_ref[:, N_LEVELS * LANES:(N_LEVELS + 1) * LANES]
    bk = e_ref[:, (N_LEVELS + 1) * LANES:(N_LEVELS + 2) * LANES]
    qs = (q * jnp.exp(bq)).astype(BF16)
    kd = (k * jnp.exp(bk)).astype(BF16)
    vb = v.astype(BF16)
    dec = jnp.exp(bq[:, LANES - 1:LANES])
    qk = q * k

    per_group = LANES // dk
    groups = [list(range(g0, min(g0 + per_group, heads))) for g0 in range(0, heads, per_group)]
    att = [jnp.zeros((LANES, len(g) * LANES), F32) for g in groups]
    for l in range(N_LEVELS):
        h = (LANES // 2) >> l
        z = jnp.where((lane & h) != 0, q, k) * jnp.exp(e_ref[:, l * LANES:(l + 1) * LANES])
        for gi, g in enumerate(groups):
            zg = z[g[0] * dk:(g[-1] + 1) * dk]
            zb = zg.astype(BF16)
            rhs = _block_diag([zb[i * dk:(i + 1) * dk] for i in range(len(g))])
            gram = _dot(zg.T.astype(BF16), rhs)
            att[gi] = jnp.where(lv_ref[:, :len(g) * LANES] == l, gram, att[gi])

    outs = []
    for gi, g in enumerate(groups):
        attb = att[gi].astype(BF16)
        for i, hd in enumerate(g):
            rk = slice(hd * dk, (hd + 1) * dk)
            rv = slice(hd * dv, (hd + 1) * dv)
            s_old = s_ref[rk, :]
            o = _dot(s_old.T.astype(BF16), qs[rk])
            o = o + _dot(vb[rv], attb[:, i * LANES:(i + 1) * LANES])
            o = o + jnp.sum(qk[rk], axis=0, keepdims=True) * v[rv]
            s_ref[rk, :] = dec[rk] * s_old + _dot_nt(kd[rk], vb[rv])
            outs.append(o)
    return outs


def _head_norm_gate(o, gain, z):
    r = lax.rsqrt(jnp.mean(o * o, axis=0, keepdims=True) + NORM_EPS)
    return o * r * gain * jax.nn.silu(z)


def _load_lane_tiles(ref, rows):
    return jnp.concatenate([ref[c, rows, :] for c in range(ref.shape[0])], axis=1)


def _store_lane_tiles(ref, val, rows=slice(None)):
    for c in range(ref.shape[0]):
        ref[c, rows, :] = val[:, c * LANES:(c + 1) * LANES]


def _cmul_add(ar, ai, xr, xi, br, bi):
    return br + ar * xr - ai * xi, bi + ar * xi + ai * xr


def _s5_gate_tokens(x_re, x_im, u, zc, cb_ref, d_ref, wglu_ref, bglu_ref):
    xcat = jnp.concatenate([x_re, x_im], axis=1).astype(BF16)
    y = _dot(xcat, cb_ref[...]) + d_ref[...] * u
    y = jax.nn.gelu(y)
    y = y * jax.nn.sigmoid(_dot(y.astype(BF16), wglu_ref[...]) + bglu_ref[...])
    return y * jax.nn.silu(zc)


def _project(x, ng_ref, wint_ref, pt_ref):
    h = _rmsnorm_rows(x, ng_ref[...])
    pt_ref[...] = _dot_nt(wint_ref[...], h.astype(BF16))


def _out_project(x, mix, wout_ref, fg_ref, final):
    out = x + _dot(mix.astype(BF16), wout_ref[...])
    if final:
        out = _rmsnorm_rows(out, fg_ref[...])
    return out


def _prompt_kernel(layer, final,
                   x_ref, ng_ref, wint_ref, lbl_ref, hgain_ref, w2t_ref, b2_ref, ggain_ref,
                   bd_ref, cb_ref, dd_ref, wglu_ref, bglu_ref, wout_ref, fg_ref,
                   aseg_ref, tsub_ref, tcar_ref, tpos_ref, win_ref, lv_ref,
                   y_ref, shg_ref, sgla_ref, sre_ref, sim_ref,
                   pt_ref, ea_ref, eb_ref, mixt_ref, bu_ref, hg_s, gla_s, cre_s, cim_s, lbc_s):
    t_idx = pl.program_id(1)

    @pl.when(t_idx == 0)
    def _():
        hg_s[...] = jnp.zeros_like(hg_s)
        gla_s[...] = jnp.zeros_like(gla_s)
        cre_s[...] = jnp.zeros_like(cre_s)
        cim_s[...] = jnp.zeros_like(cim_s)
        _lower_bound_consts(lbl_ref, layer, lbc_s)

    x = x_ref[0]
    _project(x, ng_ref, wint_ref, pt_ref)
    lane = lax.broadcasted_iota(jnp.int32, (1, LANES), 1)

    log_f, k_a = _hgrn_gates(pt_ref[O_FA:O_FA + A_WIDTH, :], lbc_s)
    ea_ref[...] = _window_sums(log_f, win_ref)
    outs = _gated_chunk(jax.nn.silu(pt_ref[O_QA:O_QA + A_WIDTH, :]), k_a, pt_ref[O_IA:O_IA + A_WIDTH, :],
                        ea_ref, hg_s, A_HEADS, A_DK, A_DV, lv_ref, lane)
    for h, o in enumerate(outs):
        r0 = h * A_DV
        mixt_ref[r0:r0 + A_DV, :] = _head_norm_gate(
            o, hgain_ref[r0:r0 + A_DV, :], pt_ref[O_ZA + r0:O_ZA + r0 + A_DV, :])

    log_g = _gla_log_gate(pt_ref[O_RB:O_RB + GLA_RANK, :], w2t_ref, b2_ref)
    eb_ref[...] = _window_sums(log_g, win_ref)
    outs = _gated_chunk(pt_ref[O_QB:O_QB + B_QK, :] * (B_DK ** -0.5), pt_ref[O_KB:O_KB + B_QK, :],
                        pt_ref[O_VB:O_VB + B_WIDTH, :], eb_ref, gla_s, B_HEADS, B_DK, B_DV, lv_ref, lane)
    for h, o in enumerate(outs):
        r0 = h * B_DV
        mixt_ref[A_WIDTH + r0:A_WIDTH + r0 + B_DV, :] = _head_norm_gate(
            o, ggain_ref[r0:r0 + B_DV, :], pt_ref[O_ZB + r0:O_ZB + r0 + B_DV, :])

    uz = pt_ref[O_UC:O_UC + 2 * C_WIDTH, :].T
    u, zc = uz[:, :C_WIDTH], uz[:, C_WIDTH:]
    _store_lane_tiles(bu_ref, _dot(u.astype(BF16), bd_ref[...]))
    ar, ai = aseg_ref[0], aseg_ref[1]
    xr = jnp.zeros((SUBLANES, S5_DIM), F32)
    xi = jnp.zeros((SUBLANES, S5_DIM), F32)
    for j in range(SEG):
        rows = pl.ds(j, SUBLANES, stride=SEG)
        b = _load_lane_tiles(bu_ref, rows)
        xr, xi = _cmul_add(ar, ai, xr, xi, b[:, :S5_DIM], b[:, S5_DIM:])
        _store_lane_tiles(bu_ref, jnp.concatenate([xr, xi], axis=1), rows)
    for i in range(3):
        xr, xi = _cmul_add(tsub_ref[i, 0], tsub_ref[i, 1],
                           pltpu.roll(xr, 1 << i, 0), pltpu.roll(xi, 1 << i, 0), xr, xi)
    cr, ci = cre_s[...], cim_s[...]
    xr, xi = _cmul_add(tcar_ref[0], tcar_ref[1], cr, ci, xr, xi)
    first = lax.broadcasted_iota(jnp.int32, (SUBLANES, S5_DIM), 0) == 0
    pr = jnp.where(first, cr, pltpu.roll(xr, 1, 0))
    pi = jnp.where(first, ci, pltpu.roll(xi, 1, 0))
    cre_s[...] = jnp.broadcast_to(xr[SUBLANES - 1:], (SUBLANES, S5_DIM))
    cim_s[...] = jnp.broadcast_to(xi[SUBLANES - 1:], (SUBLANES, S5_DIM))
    tr, ti = tpos_ref[0], tpos_ref[1]
    x_re, x_im = [], []
    for i in range(SUBLANES):
        rows = slice(i * SEG, (i + 1) * SEG)
        sr = jnp.broadcast_to(pr[i:i + 1], (SEG, S5_DIM))
        si = jnp.broadcast_to(pi[i:i + 1], (SEG, S5_DIM))
        b = _load_lane_tiles(bu_ref, rows)
        fr, fi = _cmul_add(tr, ti, sr, si, b[:, :S5_DIM], b[:, S5_DIM:])
        x_re.append(fr)
        x_im.append(fi)
    oc = _s5_gate_tokens(jnp.concatenate(x_re, axis=0), jnp.concatenate(x_im, axis=0), u, zc,
                         cb_ref, dd_ref, wglu_ref, bglu_ref)

    mix = jnp.concatenate([mixt_ref[...].T, oc], axis=1)
    y_ref[0] = _out_project(x, mix, wout_ref, fg_ref, final)

    @pl.when(t_idx == pl.num_programs(1) - 1)
    def _():
        for h in range(A_HEADS):
            shg_ref[0, h] = hg_s[h * A_DK:(h + 1) * A_DK, :]
        for h in range(B_HEADS):
            sgla_ref[0, h] = gla_s[h * B_DK:(h + 1) * B_DK, :]
        sre_ref[0] = cre_s[:1, :]
        sim_ref[0] = cim_s[:1, :]


def _sample_kernel(layer, final,
                   x_ref, ng_ref, wint_ref, lbl_ref, hgain_ref, w2t_ref, b2_ref, ggain_ref,
                   bd_ref, cb_ref, dd_ref, wglu_ref, bglu_ref, wout_ref, fg_ref, aseg_ref,
                   shg_in, sgla_in, sre_in, sim_in,
                   y_ref, shg_out, sgla_out, sre_out, sim_out,
                   pt_ref, fa_s, ka_s, qa_s, gb_s, mixt_ref, lbc_s):
    h = pl.program_id(0)

    @pl.when(h == 0)
    def _():
        _lower_bound_consts(lbl_ref, layer, lbc_s)
        _project(x_ref[...], ng_ref, wint_ref, pt_ref)
        log_f, k_a = _hgrn_gates(pt_ref[O_FA:O_FA + A_WIDTH, :], lbc_s)
        fa_s[...] = jnp.exp(log_f)
        ka_s[...] = k_a
        qa_s[...] = jax.nn.silu(pt_ref[O_QA:O_QA + A_WIDTH, :])
        gb_s[...] = jnp.exp(_gla_log_gate(pt_ref[O_RB:O_RB + GLA_RANK, :], w2t_ref, b2_ref))

    ra = pl.multiple_of(h * A_DK, A_DK)
    v_a = pt_ref[pl.ds(O_IA + ra, A_DV), :]

    def hg_body(k, o):
        r = ra + k
        rows = pl.ds(pl.multiple_of(k * A_DV, A_DV), A_DV)
        new = fa_s[pl.ds(r, 1), :] * shg_in[rows, :] + ka_s[pl.ds(r, 1), :] * v_a
        shg_out[rows, :] = new
        return o + qa_s[pl.ds(r, 1), :] * new

    o = lax.fori_loop(0, A_DK, hg_body, jnp.zeros((A_DV, LANES), F32))
    mixt_ref[pl.ds(ra, A_DV), :] = _head_norm_gate(
        o, hgain_ref[pl.ds(ra, A_DV), :], pt_ref[pl.ds(O_ZA + ra, A_DV), :])

    rb = pl.multiple_of(h * B_DK, B_DK)
    vb0 = pl.multiple_of(h * B_DV, B_DV)
    v_b = pt_ref[pl.ds(O_VB + vb0, B_DV), :]

    def gla_body(k, o):
        r = rb + k
        rows = pl.ds(pl.multiple_of(k * B_DV, B_DV), B_DV)
        new = gb_s[pl.ds(r, 1), :] * sgla_in[rows, :] + pt_ref[pl.ds(O_KB + r, 1), :] * v_b
        sgla_out[rows, :] = new
        return o + (pt_ref[pl.ds(O_QB + r, 1), :] * (B_DK ** -0.5)) * new

    o = lax.fori_loop(0, B_DK, gla_body, jnp.zeros((B_DV, LANES), F32))
    mixt_ref[pl.ds(A_WIDTH + vb0, B_DV), :] = _head_norm_gate(
        o, ggain_ref[pl.ds(vb0, B_DV), :], pt_ref[pl.ds(O_ZB + vb0, B_DV), :])

    @pl.when(h == pl.num_programs(0) - 1)
    def _():
        uz = pt_ref[O_UC:O_UC + 2 * C_WIDTH, :].T
        u, zc = uz[:, :C_WIDTH], uz[:, C_WIDTH:]
        bu = _dot(u.astype(BF16), bd_ref[...])
        x_re, x_im = _cmul_add(aseg_ref[0, :1], aseg_ref[1, :1], sre_in[...], sim_in[...],
                               bu[:, :S5_DIM], bu[:, S5_DIM:])
        sre_out[...] = x_re
        sim_out[...] = x_im
        oc = _s5_gate_tokens(x_re, x_im, u, zc, cb_ref, dd_ref, wglu_ref, bglu_ref)
        mix = jnp.concatenate([mixt_ref[...].T, oc], axis=1)
        y_ref[...] = _out_project(x_ref[...], mix, wout_ref, fg_ref, final)


def _lane_bcast(v):
    return jnp.broadcast_to(v.astype(F32)[..., None], v.shape + (LANES,))


def _powers(ar, ai, n):
    pr, pi = ar[..., None], ai[..., None]
    while pr.shape[-1] < n:
        lr, li = pr[..., -1:], pi[..., -1:]
        pr, pi = (jnp.concatenate([pr, pr * lr - pi * li], axis=-1),
                  jnp.concatenate([pi, pr * li + pi * lr], axis=-1))
    return pr[..., :n], pi[..., :n]


def _group_block_diag(m):
    g, r, c = m.shape
    eye = jnp.eye(g, dtype=m.dtype)
    return (m[:, :, None, :] * eye[:, None, :, None]).reshape(g * r, g * c)


def _s5_tables(A_re, A_im, B_re, B_im, C_re, C_im, log_dt):
    dt = jnp.exp(log_dt.astype(F32))[:, None]
    lr, li = A_re.astype(F32), A_im.astype(F32)
    mag = jnp.exp(lr * dt)
    ab_re = mag * jnp.cos(li * dt)
    ab_im = mag * jnp.sin(li * dt)
    den = lr * lr + li * li
    nr = ab_re - 1.0
    co_re = (nr * lr + ab_im * li) / den
    co_im = (ab_im * lr - nr * li) / den
    Br, Bi = B_re.astype(F32), B_im.astype(F32)
    bb_re = co_re[..., None] * Br - co_im[..., None] * Bi
    bb_im = co_re[..., None] * Bi + co_im[..., None] * Br
    bd = jnp.concatenate([_group_block_diag(bb_re.transpose(0, 2, 1)),
                          _group_block_diag(bb_im.transpose(0, 2, 1))], axis=1).astype(BF16)
    cb = jnp.concatenate([_group_block_diag(C_re.astype(F32).transpose(0, 2, 1)),
                          _group_block_diag(-C_im.astype(F32).transpose(0, 2, 1))], axis=0).astype(BF16)
    pr, pi = _powers(ab_re.reshape(-1), ab_im.reshape(-1), LANES)
    pw = jnp.stack([pr.T, pi.T])
    aseg = jnp.broadcast_to(pw[:, :1], (2, SUBLANES, S5_DIM))
    sub = jnp.arange(SUBLANES)[None, :, None]
    tsub = jnp.stack([jnp.where(sub >= (1 << i), pw[:, SEG * (1 << i) - 1][:, None, :], 0.0) for i in range(3)])
    tcar = pw[:, SEG - 1::SEG]
    tpos = pw[:, :SEG]
    return bd, cb, aseg, tsub, tcar, tpos


def _const_spec(a):
    nd = a.ndim
    return pl.BlockSpec(a.shape, lambda *_, nd=nd: (0,) * nd)


def _layer_operands(l, norm_g, w_in, hg_lb_logits, hg_norm_g, gla_w2, gla_b2, gla_norm_g,
                    s5_A_re, s5_A_im, s5_B_re, s5_B_im, s5_C_re, s5_C_im, s5_D, s5_log_dt,
                    s5_w_glu, s5_b_glu, w_out, final_norm_g):
    bd, cb, aseg, tsub, tcar, tpos = _s5_tables(s5_A_re[l], s5_A_im[l], s5_B_re[l], s5_B_im[l],
                                                s5_C_re[l], s5_C_im[l], s5_log_dt[l])
    common = (
        norm_g[l].astype(F32).reshape(1, D_MODEL),
        w_in[l].T.astype(BF16),
        _lane_bcast(hg_lb_logits),
        _lane_bcast(hg_norm_g[l]),
        gla_w2[l].T.astype(BF16),
        _lane_bcast(gla_b2[l]),
        _lane_bcast(gla_norm_g[l]),
        bd, cb,
        s5_D[l].astype(F32).reshape(1, C_WIDTH),
        s5_w_glu[l].astype(BF16),
        s5_b_glu[l].astype(F32).reshape(1, C_WIDTH),
        w_out[l].astype(BF16),
        final_norm_g.astype(F32).reshape(1, D_MODEL),
        aseg,
    )
    return common, (tsub, tcar, tpos)


def _prompt_layer(layer, final, x, common, scan_tables, win, lv):
    bsz, t, _ = x.shape
    assert t % LANES == 0
    consts = common + scan_tables + (win, lv)
    out_shape = (
        jax.ShapeDtypeStruct((bsz, t, D_MODEL), F32),
        jax.ShapeDtypeStruct((bsz, A_HEADS, A_DK, A_DV), F32),
        jax.ShapeDtypeStruct((bsz, B_HEADS, B_DK, B_DV), F32),
        jax.ShapeDtypeStruct((bsz, 1, S5_DIM), F32),
        jax.ShapeDtypeStruct((bsz, 1, S5_DIM), F32),
    )
    x_spec = pl.BlockSpec((1, LANES, D_MODEL), lambda b, i: (b, i, 0))
    out_specs = (
        x_spec,
        pl.BlockSpec((1, A_HEADS, A_DK, A_DV), lambda b, i: (b, 0, 0, 0)),
        pl.BlockSpec((1, B_HEADS, B_DK, B_DV), lambda b, i: (b, 0, 0, 0)),
        pl.BlockSpec((1, 1, S5_DIM), lambda b, i: (b, 0, 0)),
        pl.BlockSpec((1, 1, S5_DIM), lambda b, i: (b, 0, 0)),
    )
    scratch = [
        pltpu.VMEM((IN_TOTAL, LANES), F32),
        pltpu.VMEM((A_WIDTH, N_WIN * LANES), F32),
        pltpu.VMEM((B_QK, N_WIN * LANES), F32),
        pltpu.VMEM((A_WIDTH + B_WIDTH, LANES), F32),
        pltpu.VMEM((2 * S5_DIM // LANES, LANES, LANES), F32),
        pltpu.VMEM((A_WIDTH, A_DV), F32),
        pltpu.VMEM((B_QK, B_DV), F32),
        pltpu.VMEM((SUBLANES, S5_DIM), F32),
        pltpu.VMEM((SUBLANES, S5_DIM), F32),
        pltpu.VMEM((3, A_WIDTH, LANES), F32),
    ]
    return pl.pallas_call(
        functools.partial(_prompt_kernel, layer, final),
        grid=(bsz, t // LANES),
        in_specs=[x_spec] + [_const_spec(a) for a in consts],
        out_specs=out_specs,
        out_shape=out_shape,
        scratch_shapes=scratch,
        compiler_params=pltpu.CompilerParams(
            dimension_semantics=("arbitrary", "arbitrary"), vmem_limit_bytes=VMEM_LIMIT),
    )(x, *consts)


def _sample_layer(layer, final, xs, common, shg, sgla, sre, sim):
    n = xs.shape[0]
    assert n == LANES
    hg_rows, gla_rows = A_DK * A_DV, B_DK * B_DV
    out_shape = (
        jax.ShapeDtypeStruct((n, D_MODEL), F32),
        jax.ShapeDtypeStruct((A_HEADS * hg_rows, LANES), F32),
        jax.ShapeDtypeStruct((B_HEADS * gla_rows, LANES), F32),
        jax.ShapeDtypeStruct((n, S5_DIM), F32),
        jax.ShapeDtypeStruct((n, S5_DIM), F32),
    )
    full = lambda shape: pl.BlockSpec(shape, lambda h: (0, 0))
    state_specs = [
        pl.BlockSpec((hg_rows, LANES), lambda h: (h, 0)),
        pl.BlockSpec((gla_rows, LANES), lambda h: (h, 0)),
        full((n, S5_DIM)),
        full((n, S5_DIM)),
    ]
    scratch = [
        pltpu.VMEM((IN_TOTAL, LANES), F32),
        pltpu.VMEM((A_WIDTH, LANES), F32),
        pltpu.VMEM((A_WIDTH, LANES), F32),
        pltpu.VMEM((A_WIDTH, LANES), F32),
        pltpu.VMEM((B_QK, LANES), F32),
        pltpu.VMEM((A_WIDTH + B_WIDTH, LANES), F32),
        pltpu.VMEM((3, A_WIDTH, LANES), F32),
    ]
    return pl.pallas_call(
        functools.partial(_sample_kernel, layer, final),
        grid=(A_HEADS,),
        in_specs=[full((n, D_MODEL))] + [_const_spec(a) for a in common] + state_specs,
        out_specs=[full((n, D_MODEL))] + state_specs,
        out_shape=out_shape,
        scratch_shapes=scratch,
        compiler_params=pltpu.CompilerParams(
            dimension_semantics=("arbitrary",), vmem_limit_bytes=VMEM_LIMIT),
    )(xs, *common, shg, sgla, sre, sim)


def kernel(x_prompt, x_sample, state_hgrn, state_gla, state_s5_re, state_s5_im, norm_g, w_in, hg_lb_logits, hg_norm_g, gla_w2, gla_b2, gla_norm_g, s5_A_re, s5_A_im, s5_B_re, s5_B_im, s5_C_re, s5_C_im, s5_D, s5_log_dt, s5_w_glu, s5_b_glu, w_out, final_norm_g):
    depth = w_in.shape[0]
    bsz = x_prompt.shape[0]
    n = x_sample.shape[0]
    assert x_sample.shape[1] == 1
    win_np, lv_np = _chunk_tables()
    win = jnp.asarray(win_np, BF16)
    lv = jnp.asarray(lv_np)

    yp = x_prompt.astype(F32)
    ys = x_sample.astype(F32).reshape(n, D_MODEL)
    prompt_states, sample_states = [], []
    for l in range(depth):
        final = l == depth - 1
        common, scan_tables = _layer_operands(
            l, norm_g, w_in, hg_lb_logits, hg_norm_g, gla_w2, gla_b2, gla_norm_g,
            s5_A_re, s5_A_im, s5_B_re, s5_B_im, s5_C_re, s5_C_im, s5_D, s5_log_dt,
            s5_w_glu, s5_b_glu, w_out, final_norm_g)
        yp, hg_p, gla_p, re_p, im_p = _prompt_layer(l, final, yp, common, scan_tables, win, lv)
        prompt_states.append((hg_p, gla_p,
                              re_p.reshape(bsz, C_GROUPS, C_STATE), im_p.reshape(bsz, C_GROUPS, C_STATE)))
        to_cm = lambda s: s.astype(F32).reshape(n, -1).T
        ys, hg_s, gla_s, re_s, im_s = _sample_layer(
            l, final, ys, common, to_cm(state_hgrn[l]), to_cm(state_gla[l]),
            state_s5_re[l].astype(F32).reshape(n, S5_DIM), state_s5_im[l].astype(F32).reshape(n, S5_DIM))
        sample_states.append((hg_s.T.reshape(n, A_HEADS, A_DK, A_DV),
                              gla_s.T.reshape(n, B_HEADS, B_DK, B_DV),
                              re_s.reshape(n, C_GROUPS, C_STATE),
                              im_s.reshape(n, C_GROUPS, C_STATE)))

    stack = lambda states, i: jnp.stack([s[i] for s in states], axis=0)
    return (yp, ys.reshape(n, 1, D_MODEL),
            stack(prompt_states, 0), stack(prompt_states, 1), stack(prompt_states, 2), stack(prompt_states, 3),
            stack(sample_states, 0), stack(sample_states, 1), stack(sample_states, 2), stack(sample_states, 3))
```

```python
import functools

import numpy as np
import jax
import jax.numpy as jnp
from jax import lax
from jax.experimental import pallas as pl
from jax.experimental.pallas import tpu as pltpu

F32 = jnp.float32
BF16 = jnp.bfloat16

D_MODEL = 1024
A_WIDTH, A_DK, A_DV, A_HEADS = 384, 64, 64, 6
B_WIDTH, B_DV, B_HEADS, B_DK, B_QK = 384, 64, 6, 32, 192
GLA_RANK = 16
GLA_TAU = 16.0
C_WIDTH, C_GROUP, C_GROUPS, C_STATE = 256, 16, 16, 64
S5_DIM = C_GROUPS * C_STATE
NORM_EPS = 1e-6
REF_BLOCKS = ("qa", "fa", "ia", "za", "qb", "kb", "vb", "zb", "rb", "uc", "zc")
REF_SIZES = (A_WIDTH, A_WIDTH, A_WIDTH, A_WIDTH, B_QK, B_QK, B_WIDTH, B_WIDTH, GLA_RANK, C_WIDTH, C_WIDTH)
KER_BLOCKS = ("qa", "fa", "ia", "za", "qb", "kb", "vb", "zb", "uc", "zc", "rb")
IN_TOTAL = sum(REF_SIZES)
_SIZE = dict(zip(REF_BLOCKS, REF_SIZES))
_REF_OFF = dict(zip(REF_BLOCKS, np.concatenate([[0], np.cumsum(REF_SIZES)[:-1]]).tolist()))
OFF = dict(zip(KER_BLOCKS, np.concatenate([[0], np.cumsum([_SIZE[b] for b in KER_BLOCKS])[:-1]]).tolist()))
A_CM = 3 * A_WIDTH
B_CM = 2 * B_QK + B_WIDTH

LANES = 128
SUBLANES = 8
N_LEVELS = 7
N_WIN = N_LEVELS + 1
SEG = LANES // SUBLANES
TOK_BLOCK = 512
VMEM_LIMIT = 56 * 1024 * 1024


def _chunk_tables():
    s = np.arange(LANES)[:, None]
    t = np.arange(LANES)[None, :]
    wins = []
    for l in range(N_LEVELS):
        h = (LANES // 2) >> l
        mid = (t // (2 * h)) * (2 * h) + h - 1
        role_q = (t & h) != 0
        wins.append(np.where(role_q, (s > mid) & (s <= t), (s > t) & (s <= mid)))
    wins.append(s <= t)
    win = np.concatenate(wins, axis=1).astype(np.float32)
    x = s ^ t
    lv = np.full((LANES, LANES), -1, np.int32)
    for l in range(N_LEVELS):
        h = (LANES // 2) >> l
        lv[(x >= h) & (x < 2 * h) & (t > s)] = l
    return np.concatenate([win, win], axis=0), np.tile(lv, (1, 2))


def _dot(a, b):
    return jnp.dot(a, b, preferred_element_type=F32)


def _dot_nt(a, b):
    return lax.dot_general(a, b, (((1,), (1,)), ((), ())), preferred_element_type=F32)


def _rmsnorm_rows(x, g):
    r = lax.rsqrt(jnp.mean(x * x, axis=-1, keepdims=True) + NORM_EPS)
    return x * r * g


def _lower_bound_consts(lbl_ref, layer, out_ref):
    rows = [lbl_ref[l] for l in range(lbl_ref.shape[0])]
    m = functools.reduce(jnp.maximum, rows)
    es = [jnp.exp(r - m) for r in rows]
    tot = functools.reduce(lambda a, b: a + b, es)
    sm = [e / tot for e in es]
    cum = sm[0]
    for l in range(1, layer + 1):
        cum = cum + sm[l]
    lb = cum - sm[0]
    out_ref[0] = jnp.log(lb)
    out_ref[1] = jnp.log(1.0 - lb)
    out_ref[2] = 1.0 - lb


def _hgrn_gates(fa, lbc_ref):
    e = jnp.exp(-jnp.abs(fa))
    logsig = jnp.minimum(fa, 0.0) - jnp.log(1.0 + e)
    a = lbc_ref[0]
    c = lbc_ref[1] + logsig
    log_f = jnp.maximum(a, c) + jnp.log(1.0 + jnp.exp(-jnp.abs(a - c)))
    k_a = lbc_ref[2] * (jnp.where(fa >= 0.0, e, 1.0) / (1.0 + e))
    return log_f, k_a


def _gla_log_gate_cm(rb, w2_ref, b2_ref):
    pre = _dot(rb.astype(BF16), w2_ref[...]) + b2_ref[...]
    return (jax.nn.log_sigmoid(pre) / GLA_TAU).T[:B_QK]


def _window_sums(g, win_ref):
    hi = g.astype(BF16)
    lo = (g - hi.astype(F32)).astype(BF16)
    return _dot(jnp.concatenate([hi, lo], axis=1), win_ref[...])


def _gated_chunk(load_q, load_k, load_v, e_ref, s_ref, heads, dk, dv, lv_ref, lane, emit):
    for p in range(heads // 2):
        rows = slice(2 * p * dk, (2 * p + 2) * dk)
        q, k = load_q(rows), load_k(rows)
        att = jnp.zeros((LANES, 2 * LANES), F32)
        for l in range(N_LEVELS):
            h = (LANES // 2) >> l
            z = jnp.where((lane & h) != 0, q, k) * jnp.exp(e_ref[rows, l * LANES:(l + 1) * LANES])
            zb = z.astype(BF16)
            zero = jnp.zeros((dk, LANES), BF16)
            rhs = jnp.concatenate([jnp.concatenate([zb[:dk], zero], axis=1),
                                   jnp.concatenate([zero, zb[dk:]], axis=1)], axis=0)
            gram = _dot(z.T.astype(BF16), rhs)
            att = jnp.where(lv_ref[...] == l, gram, att)
        attb = att.astype(BF16)
        bq = e_ref[rows, N_LEVELS * LANES:(N_LEVELS + 1) * LANES]
        b_last = bq[:, LANES - 1:LANES]
        qs = (q * jnp.exp(bq)).astype(BF16)
        kd = (k * jnp.exp(b_last - bq)).astype(BF16)
        dec = jnp.exp(b_last)
        qk = q * k
        for i in range(2):
            hd = 2 * p + i
            r = slice(i * dk, (i + 1) * dk)
            rs = slice(hd * dk, (hd + 1) * dk)
            v = load_v(slice(hd * dv, (hd + 1) * dv))
            vb = v.astype(BF16)
            s_old = s_ref[rs, :]
            o = _dot(s_old.T.astype(BF16), qs[r])
            o = o + _dot(vb, attb[:, i * LANES:(i + 1) * LANES])
            o = o + jnp.sum(qk[r], axis=0, keepdims=True) * v
            s_ref[rs, :] = dec[r] * s_old + _dot_nt(kd[r], vb)
            emit(hd, o)


def _head_rms(o):
    return o * lax.rsqrt(jnp.mean(o * o, axis=0, keepdims=True) + NORM_EPS)


def _load_lane_tiles(ref, rows):
    return jnp.concatenate([ref[c, rows, :] for c in range(ref.shape[0])], axis=1)


def _store_lane_tiles(ref, val, rows=slice(None)):
    for c in range(ref.shape[0]):
        ref[c, rows, :] = val[:, c * LANES:(c + 1) * LANES]


def _cmul_add(ar, ai, xr, xi, br, bi):
    return br + ar * xr - ai * xi, bi + ar * xi + ai * xr


def _s5_gate_tokens(x_re, x_im, u, zc, cb_ref, d_ref, wglu_ref, bglu_ref):
    xcat = jnp.concatenate([x_re, x_im], axis=1).astype(BF16)
    y = _dot(xcat, cb_ref[...]) + d_ref[...] * u
    y = jax.nn.gelu(y)
    y = y * jax.nn.sigmoid(_dot(y.astype(BF16), wglu_ref[...]) + bglu_ref[...])
    return y * jax.nn.silu(zc)


def _project(x, ng_ref, win_ref, p_ref):
    h = _rmsnorm_rows(x, ng_ref[...])
    p_ref[...] = _dot(h.astype(BF16), win_ref[...])


def _mix_and_project(x, mixt_ref, za, zb, oc, hgain_ref, ggain_ref, wout_ref, fg_ref, final):
    o_t = mixt_ref[...].T
    o_a = o_t[:, :A_WIDTH] * hgain_ref[...] * jax.nn.silu(za)
    o_b = o_t[:, A_WIDTH:] * ggain_ref[...] * jax.nn.silu(zb)
    mix = jnp.concatenate([o_a, o_b, oc], axis=1).astype(BF16)
    out = x + _dot(mix, wout_ref[...])
    if final:
        out = _rmsnorm_rows(out, fg_ref[...])
    return out


def _cols(name):
    return slice(OFF[name], OFF[name] + _SIZE[name])


def _prompt_kernel(layer, final, n_sub,
                   x_ref, ng_ref, win_ref, lbl_ref, hgain_ref, w2_ref, b2_ref, ggain_ref,
                   bd_ref, cb_ref, dd_ref, wglu_ref, bglu_ref, wout_ref, fg_ref, aseg_ref,
                   tsub_ref, tcar_ref, tpos_ref, wwin_ref, lv_ref,
                   y_ref, shg_ref, sgla_ref, sre_ref, sim_ref,
                   p_ref, cma_ref, cmb_ref, ea_ref, eb_ref, mixt_ref, bu_ref,
                   hg_s, gla_s, cre_s, cim_s, lbc_s):
    t_idx = pl.program_id(1)
    sub = t_idx % n_sub

    @pl.when(t_idx == 0)
    def _():
        hg_s[...] = jnp.zeros_like(hg_s)
        gla_s[...] = jnp.zeros_like(gla_s)
        cre_s[...] = jnp.zeros_like(cre_s)
        cim_s[...] = jnp.zeros_like(cim_s)
        _lower_bound_consts(lbl_ref, layer, lbc_s)

    @pl.when(sub == 0)
    def _():
        _project(x_ref[0], ng_ref, win_ref, p_ref)

    tok = pl.ds(pl.multiple_of(sub * LANES, LANES), LANES)
    lane = lax.broadcasted_iota(jnp.int32, (1, LANES), 1)
    cma_ref[...] = p_ref[tok, :A_CM].T
    cmb_ref[...] = p_ref[tok, OFF["qb"]:OFF["qb"] + B_CM].T

    log_f, k_a = _hgrn_gates(cma_ref[A_WIDTH:2 * A_WIDTH, :], lbc_s)
    cma_ref[A_WIDTH:2 * A_WIDTH, :] = k_a
    ea_ref[...] = _window_sums(log_f, wwin_ref)

    def emit_a(hd, o):
        mixt_ref[hd * A_DV:(hd + 1) * A_DV, :] = _head_rms(o)

    _gated_chunk(lambda r: jax.nn.silu(cma_ref[r, :]),
                 lambda r: cma_ref[A_WIDTH + r.start:A_WIDTH + r.stop, :],
                 lambda r: cma_ref[2 * A_WIDTH + r.start:2 * A_WIDTH + r.stop, :],
                 ea_ref, hg_s, A_HEADS, A_DK, A_DV, lv_ref, lane, emit_a)

    eb_ref[...] = _window_sums(_gla_log_gate_cm(p_ref[tok, _cols("rb")], w2_ref, b2_ref), wwin_ref)

    def emit_b(hd, o):
        mixt_ref[A_WIDTH + hd * B_DV:A_WIDTH + (hd + 1) * B_DV, :] = _head_rms(o)

    _gated_chunk(lambda r: cmb_ref[r, :] * (B_DK ** -0.5),
                 lambda r: cmb_ref[B_QK + r.start:B_QK + r.stop, :],
                 lambda r: cmb_ref[2 * B_QK + r.start:2 * B_QK + r.stop, :],
                 eb_ref, gla_s, B_HEADS, B_DK, B_DV, lv_ref, lane, emit_b)

    u = p_ref[tok, _cols("uc")]
    _store_lane_tiles(bu_ref, _dot(u.astype(BF16), bd_ref[...]))
    ar, ai = aseg_ref[0], aseg_ref[1]
    xr = jnp.zeros((SUBLANES, S5_DIM), F32)
    xi = jnp.zeros((SUBLANES, S5_DIM), F32)
    for j in range(SEG):
        rows = pl.ds(j, SUBLANES, stride=SEG)
        b = _load_lane_tiles(bu_ref, rows)
        xr, xi = _cmul_add(ar, ai, xr, xi, b[:, :S5_DIM], b[:, S5_DIM:])
        _store_lane_tiles(bu_ref, jnp.concatenate([xr, xi], axis=1), rows)
    for i in range(3):
        xr, xi = _cmul_add(tsub_ref[i, 0], tsub_ref[i, 1],
                           pltpu.roll(xr, 1 << i, 0), pltpu.roll(xi, 1 << i, 0), xr, xi)
    cr, ci = cre_s[...], cim_s[...]
    xr, xi = _cmul_add(tcar_ref[0], tcar_ref[1], cr, ci, xr, xi)
    first = lax.broadcasted_iota(jnp.int32, (SUBLANES, S5_DIM), 0) == 0
    pr = jnp.where(first, cr, pltpu.roll(xr, 1, 0))
    pi = jnp.where(first, ci, pltpu.roll(xi, 1, 0))
    cre_s[...] = jnp.broadcast_to(xr[SUBLANES - 1:], (SUBLANES, S5_DIM))
    cim_s[...] = jnp.broadcast_to(xi[SUBLANES - 1:], (SUBLANES, S5_DIM))
    tr, ti = tpos_ref[0], tpos_ref[1]
    x_re, x_im = [], []
    for i in range(SUBLANES):
        rows = slice(i * SEG, (i + 1) * SEG)
        sr = jnp.broadcast_to(pr[i:i + 1], (SEG, S5_DIM))
        si = jnp.broadcast_to(pi[i:i + 1], (SEG, S5_DIM))
        b = _load_lane_tiles(bu_ref, rows)
        fr, fi = _cmul_add(tr, ti, sr, si, b[:, :S5_DIM], b[:, S5_DIM:])
        x_re.append(fr)
        x_im.append(fi)
    oc = _s5_gate_tokens(jnp.concatenate(x_re, axis=0), jnp.concatenate(x_im, axis=0), u,
                         p_ref[tok, _cols("zc")], cb_ref, dd_ref, wglu_ref, bglu_ref)

    y_ref[0, tok, :] = _mix_and_project(
        x_ref[0, tok, :], mixt_ref, p_ref[tok, _cols("za")], p_ref[tok, _cols("zb")], oc,
        hgain_ref, ggain_ref, wout_ref, fg_ref, final)

    @pl.when(t_idx == pl.num_programs(1) - 1)
    def _():
        for h in range(A_HEADS):
            shg_ref[0, h] = hg_s[h * A_DK:(h + 1) * A_DK, :]
        for h in range(B_HEADS):
            sgla_ref[0, h] = gla_s[h * B_DK:(h + 1) * B_DK, :]
        sre_ref[0] = cre_s[:1, :]
        sim_ref[0] = cim_s[:1, :]


def _sample_kernel(layer, final,
                   x_ref, ng_ref, win_ref, lbl_ref, hgain_ref, w2_ref, b2_ref, ggain_ref,
                   bd_ref, cb_ref, dd_ref, wglu_ref, bglu_ref, wout_ref, fg_ref, aseg_ref,
                   shg_in, sgla_in, sre_in, sim_in,
                   y_ref, shg_out, sgla_out, sre_out, sim_out,
                   p_ref, cma_ref, cmb_ref, fa_s, gb_s, mixt_ref, lbc_s):
    h = pl.program_id(0)

    @pl.when(h == 0)
    def _():
        _lower_bound_consts(lbl_ref, layer, lbc_s)
        _project(x_ref[...], ng_ref, win_ref, p_ref)
        cma_ref[...] = p_ref[:, :A_CM].T
        cmb_ref[...] = p_ref[:, OFF["qb"]:OFF["qb"] + B_CM].T
        log_f, k_a = _hgrn_gates(cma_ref[A_WIDTH:2 * A_WIDTH, :], lbc_s)
        fa_s[...] = jnp.exp(log_f)
        cma_ref[A_WIDTH:2 * A_WIDTH, :] = k_a
        cma_ref[:A_WIDTH, :] = jax.nn.silu(cma_ref[:A_WIDTH, :])
        gb_s[...] = jnp.exp(_gla_log_gate_cm(p_ref[:, _cols("rb")], w2_ref, b2_ref))

    ra = pl.multiple_of(h * A_DK, A_DK)
    v_a = cma_ref[pl.ds(2 * A_WIDTH + ra, A_DV), :]

    def hg_body(k, o):
        r = ra + k
        rows = pl.ds(pl.multiple_of(k * A_DV, A_DV), A_DV)
        new = fa_s[pl.ds(r, 1), :] * shg_in[rows, :] + cma_ref[pl.ds(A_WIDTH + r, 1), :] * v_a
        shg_out[rows, :] = new
        return o + cma_ref[pl.ds(r, 1), :] * new

    o = lax.fori_loop(0, A_DK, hg_body, jnp.zeros((A_DV, LANES), F32))
    mixt_ref[pl.ds(ra, A_DV), :] = _head_rms(o)

    rb = pl.multiple_of(h * B_DK, B_DK)
    vb0 = pl.multiple_of(h * B_DV, B_DV)
    v_b = cmb_ref[pl.ds(2 * B_QK + vb0, B_DV), :]

    def gla_body(k, o):
        r = rb + k
        rows = pl.ds(pl.multiple_of(k * B_DV, B_DV), B_DV)
        new = gb_s[pl.ds(r, 1), :] * sgla_in[rows, :] + cmb_ref[pl.ds(B_QK + r, 1), :] * v_b
        sgla_out[rows, :] = new
        return o + (cmb_ref[pl.ds(r, 1), :] * (B_DK ** -0.5)) * new

    o = lax.fori_loop(0, B_DK, gla_body, jnp.zeros((B_DV, LANES), F32))
    mixt_ref[pl.ds(A_WIDTH + vb0, B_DV), :] = _head_rms(o)

    @pl.when(h == pl.num_programs(0) - 1)
    def _():
        u = p_ref[:, _cols("uc")]
        bu = _dot(u.astype(BF16), bd_ref[...])
        x_re, x_im = _cmul_add(aseg_ref[0, :1], aseg_ref[1, :1], sre_in[...], sim_in[...],
                               bu[:, :S5_DIM], bu[:, S5_DIM:])
        sre_out[...] = x_re
        sim_out[...] = x_im
        oc = _s5_gate_tokens(x_re, x_im, u, p_ref[:, _cols("zc")], cb_ref, dd_ref, wglu_ref, bglu_ref)
        y_ref[...] = _mix_and_project(
            x_ref[...], mixt_ref, p_ref[:, _cols("za")], p_ref[:, _cols("zb")], oc,
            hgain_ref, ggain_ref, wout_ref, fg_ref, final)


def _lane_bcast(v):
    return jnp.broadcast_to(v.astype(F32)[..., None], v.shape + (LANES,))


def _row(v):
    return v.astype(F32).reshape(1, -1)


def _powers(ar, ai, n):
    pr, pi = ar[..., None], ai[..., None]
    while pr.shape[-1] < n:
        lr, li = pr[..., -1:], pi[..., -1:]
        pr, pi = (jnp.concatenate([pr, pr * lr - pi * li], axis=-1),
                  jnp.concatenate([pi, pr * li + pi * lr], axis=-1))
    return pr[..., :n], pi[..., :n]


def _group_block_diag(m):
    g, r, c = m.shape
    eye = jnp.eye(g, dtype=m.dtype)
    return (m[:, :, None, :] * eye[:, None, :, None]).reshape(g * r, g * c)


def _s5_tables(A_re, A_im, B_re, B_im, C_re, C_im, log_dt):
    dt = jnp.exp(log_dt.astype(F32))[:, None]
    lr, li = A_re.astype(F32), A_im.astype(F32)
    mag = jnp.exp(lr * dt)
    ab_re = mag * jnp.cos(li * dt)
    ab_im = mag * jnp.sin(li * dt)
    den = lr * lr + li * li
    nr = ab_re - 1.0
    co_re = (nr * lr + ab_im * li) / den
    co_im = (ab_im * lr - nr * li) / den
    Br, Bi = B_re.astype(F32), B_im.astype(F32)
    bb_re = co_re[..., None] * Br - co_im[..., None] * Bi
    bb_im = co_re[..., None] * Bi + co_im[..., None] * Br
    bd = jnp.concatenate([_group_block_diag(bb_re.transpose(0, 2, 1)),
                          _group_block_diag(bb_im.transpose(0, 2, 1))], axis=1).astype(BF16)
    cb = jnp.concatenate([_group_block_diag(C_re.astype(F32).transpose(0, 2, 1)),
                          _group_block_diag(-C_im.astype(F32).transpose(0, 2, 1))], axis=0).astype(BF16)
    pr, pi = _powers(ab_re.reshape(-1), ab_im.reshape(-1), LANES)
    pw = jnp.stack([pr.T, pi.T])
    aseg = jnp.broadcast_to(pw[:, :1], (2, SUBLANES, S5_DIM))
    sub = jnp.arange(SUBLANES)[None, :, None]
    tsub = jnp.stack([jnp.where(sub >= (1 << i), pw[:, SEG * (1 << i) - 1][:, None, :], 0.0) for i in range(3)])
    tcar = pw[:, SEG - 1::SEG]
    tpos = pw[:, :SEG]
    return bd, cb, aseg, tsub, tcar, tpos


def _const_spec(a):
    nd = a.ndim
    return pl.BlockSpec(a.shape, lambda *_, nd=nd: (0,) * nd)


def _layer_operands(l, norm_g, w_in, hg_lb_logits, hg_norm_g, gla_w2, gla_b2, gla_norm_g,
                    s5_A_re, s5_A_im, s5_B_re, s5_B_im, s5_C_re, s5_C_im, s5_D, s5_log_dt,
                    s5_w_glu, s5_b_glu, w_out, final_norm_g):
    bd, cb, aseg, tsub, tcar, tpos = _s5_tables(s5_A_re[l], s5_A_im[l], s5_B_re[l], s5_B_im[l],
                                                s5_C_re[l], s5_C_im[l], s5_log_dt[l])
    w = w_in[l]
    w_ker = jnp.concatenate([w[:, _REF_OFF[b]:_REF_OFF[b] + _SIZE[b]] for b in KER_BLOCKS], axis=1)
    pad = 2 * LANES - B_QK
    common = (
        _row(norm_g[l]),
        w_ker.astype(BF16),
        _lane_bcast(hg_lb_logits),
        _row(hg_norm_g[l]),
        jnp.pad(gla_w2[l], ((0, 0), (0, pad))).astype(BF16),
        jnp.pad(_row(gla_b2[l]), ((0, 0), (0, pad))),
        _row(gla_norm_g[l]),
        bd, cb,
        _row(s5_D[l]),
        s5_w_glu[l].astype(BF16),
        _row(s5_b_glu[l]),
        w_out[l].astype(BF16),
        _row(final_norm_g),
        aseg,
    )
    return common, (tsub, tcar, tpos)


def _prompt_layer(layer, final, x, common, scan_tables, win, lv):
    bsz, t, _ = x.shape
    tb = min(TOK_BLOCK, t)
    assert t % tb == 0 and tb % LANES == 0
    n_sub = tb // LANES
    consts = common + scan_tables + (win, lv)
    out_shape = (
        jax.ShapeDtypeStruct((bsz, t, D_MODEL), F32),
        jax.ShapeDtypeStruct((bsz, A_HEADS, A_DK, A_DV), F32),
        jax.ShapeDtypeStruct((bsz, B_HEADS, B_DK, B_DV), F32),
        jax.ShapeDtypeStruct((bsz, 1, S5_DIM), F32),
        jax.ShapeDtypeStruct((bsz, 1, S5_DIM), F32),
    )
    x_spec = pl.BlockSpec((1, tb, D_MODEL), lambda b, i: (b, i // n_sub, 0))
    out_specs = (
        x_spec,
        pl.BlockSpec((1, A_HEADS, A_DK, A_DV), lambda b, i: (b, 0, 0, 0)),
        pl.BlockSpec((1, B_HEADS, B_DK, B_DV), lambda b, i: (b, 0, 0, 0)),
        pl.BlockSpec((1, 1, S5_DIM), lambda b, i: (b, 0, 0)),
        pl.BlockSpec((1, 1, S5_DIM), lambda b, i: (b, 0, 0)),
    )
    scratch = [
        pltpu.VMEM((tb, IN_TOTAL), F32),
        pltpu.VMEM((A_CM, LANES), F32),
        pltpu.VMEM((B_CM, LANES), F32),
        pltpu.VMEM((A_WIDTH, N_WIN * LANES), F32),
        pltpu.VMEM((B_QK, N_WIN * LANES), F32),
        pltpu.VMEM((A_WIDTH + B_WIDTH, LANES), F32),
        pltpu.VMEM((2 * S5_DIM // LANES, LANES, LANES), F32),
        pltpu.VMEM((A_WIDTH, A_DV), F32),
        pltpu.VMEM((B_QK, B_DV), F32),
        pltpu.VMEM((SUBLANES, S5_DIM), F32),
        pltpu.VMEM((SUBLANES, S5_DIM), F32),
        pltpu.VMEM((3, A_WIDTH, LANES), F32),
    ]
    return pl.pallas_call(
        functools.partial(_prompt_kernel, layer, final, n_sub),
        grid=(bsz, t // LANES),
        in_specs=[x_spec] + [_const_spec(a) for a in consts],
        out_specs=out_specs,
        out_shape=out_shape,
        scratch_shapes=scratch,
        compiler_params=pltpu.CompilerParams(
            dimension_semantics=("arbitrary", "arbitrary"), vmem_limit_bytes=VMEM_LIMIT),
    )(x, *consts)


def _sample_layer(layer, final, xs, common, shg, sgla, sre, sim):
    n = xs.shape[0]
    assert n == LANES
    hg_rows, gla_rows = A_DK * A_DV, B_DK * B_DV
    out_shape = (
        jax.ShapeDtypeStruct((n, D_MODEL), F32),
        jax.ShapeDtypeStruct((A_HEADS * hg_rows, LANES), F32),
        jax.ShapeDtypeStruct((B_HEADS * gla_rows, LANES), F32),
        jax.ShapeDtypeStruct((n, S5_DIM), F32),
        jax.ShapeDtypeStruct((n, S5_DIM), F32),
    )
    full = lambda shape: pl.BlockSpec(shape, lambda h: (0, 0))
    state_specs = [
        pl.BlockSpec((hg_rows, LANES), lambda h: (h, 0)),
        pl.BlockSpec((gla_rows, LANES), lambda h: (h, 0)),
        full((n, S5_DIM)),
        full((n, S5_DIM)),
    ]
    scratch = [
        pltpu.VMEM((n, IN_TOTAL), F32),
        pltpu.VMEM((A_CM, LANES), F32),
        pltpu.VMEM((B_CM, LANES), F32),
        pltpu.VMEM((A_WIDTH, LANES), F32),
        pltpu.VMEM((B_QK, LANES), F32),
        pltpu.VMEM((A_WIDTH + B_WIDTH, LANES), F32),
        pltpu.VMEM((3, A_WIDTH, LANES), F32),
    ]
    return pl.pallas_call(
        functools.partial(_sample_kernel, layer, final),
        grid=(A_HEADS,),
        in_specs=[full((n, D_MODEL))] + [_const_spec(a) for a in common] + state_specs,
        out_specs=[full((n, D_MODEL))] + state_specs,
        out_shape=out_shape,
        scratch_shapes=scratch,
        compiler_params=pltpu.CompilerParams(
            dimension_semantics=("arbitrary",), vmem_limit_bytes=VMEM_LIMIT),
    )(xs, *common, shg, sgla, sre, sim)


def kernel(x_prompt, x_sample, state_hgrn, state_gla, state_s5_re, state_s5_im, norm_g, w_in, hg_lb_logits, hg_norm_g, gla_w2, gla_b2, gla_norm_g, s5_A_re, s5_A_im, s5_B_re, s5_B_im, s5_C_re, s5_C_im, s5_D, s5_log_dt, s5_w_glu, s5_b_glu, w_out, final_norm_g):
    depth = w_in.shape[0]
    bsz = x_prompt.shape[0]
    n = x_sample.shape[0]
    assert x_sample.shape[1] == 1
    win_np, lv_np = _chunk_tables()
    win = jnp.asarray(win_np, BF16)
    lv = jnp.asarray(lv_np)

    yp = x_prompt.astype(F32)
    ys = x_sample.astype(F32).reshape(n, D_MODEL)
    prompt_states, sample_states = [], []
    for l in range(depth):
        final = l == depth - 1
        common, scan_tables = _layer_operands(
            l, norm_g, w_in, hg_lb_logits, hg_norm_g, gla_w2, gla_b2, gla_norm_g,
            s5_A_re, s5_A_im, s5_B_re, s5_B_im, s5_C_re, s5_C_im, s5_D, s5_log_dt,
            s5_w_glu, s5_b_glu, w_out, final_norm_g)
        yp, hg_p, gla_p, re_p, im_p = _prompt_layer(l, final, yp, common, scan_tables, win, lv)
        prompt_states.append((hg_p, gla_p,
                              re_p.reshape(bsz, C_GROUPS, C_STATE), im_p.reshape(bsz, C_GROUPS, C_STATE)))
        to_cm = lambda s: s.astype(F32).reshape(n, -1).T
        ys, hg_s, gla_s, re_s, im_s = _sample_layer(
            l, final, ys, common, to_cm(state_hgrn[l]), to_cm(state_gla[l]),
            state_s5_re[l].astype(F32).reshape(n, S5_DIM), state_s5_im[l].astype(F32).reshape(n, S5_DIM))
        sample_states.append((hg_s.T.reshape(n, A_HEADS, A_DK, A_DV),
                              gla_s.T.reshape(n, B_HEADS, B_DK, B_DV),
                              re_s.reshape(n, C_GROUPS, C_STATE),
                              im_s.reshape(n, C_GROUPS, C_STATE)))

    stack = lambda states, i: jnp.stack([s[i] for s in states], axis=0)
    return (yp, ys.reshape(n, 1, D_MODEL),
            stack(prompt_states, 0), stack(prompt_states, 1), stack(prompt_states, 2), stack(prompt_states, 3),
            stack(sample_states, 0), stack(sample_states, 1), stack(sample_states, 2), stack(sample_states, 3))
```

```python
import functools

import numpy as np
import jax
import jax.numpy as jnp
from jax import lax
from jax.experimental import pallas as pl
from jax.experimental.pallas import tpu as pltpu

F32 = jnp.float32
BF16 = jnp.bfloat16

D_MODEL = 1024
A_WIDTH, A_DK, A_DV, A_HEADS = 384, 64, 64, 6
B_WIDTH, B_DV, B_HEADS, B_DK, B_QK = 384, 64, 6, 32, 192
GLA_RANK = 16
GLA_TAU = 16.0
C_WIDTH, C_GROUP, C_GROUPS, C_STATE = 256, 16, 16, 64
S5_DIM = C_GROUPS * C_STATE
NORM_EPS = 1e-6
REF_BLOCKS = ("qa", "fa", "ia", "za", "qb", "kb", "vb", "zb", "rb", "uc", "zc")
REF_SIZES = (A_WIDTH, A_WIDTH, A_WIDTH, A_WIDTH, B_QK, B_QK, B_WIDTH, B_WIDTH, GLA_RANK, C_WIDTH, C_WIDTH)
KER_BLOCKS = ("qa", "fa", "ia", "za", "qb", "kb", "vb", "zb", "uc", "zc", "rb")
IN_TOTAL = sum(REF_SIZES)
_SIZE = dict(zip(REF_BLOCKS, REF_SIZES))
_REF_OFF = dict(zip(REF_BLOCKS, np.concatenate([[0], np.cumsum(REF_SIZES)[:-1]]).tolist()))
OFF = dict(zip(KER_BLOCKS, np.concatenate([[0], np.cumsum([_SIZE[b] for b in KER_BLOCKS])[:-1]]).tolist()))
A_CM = 3 * A_WIDTH
B_CM = 2 * B_QK + B_WIDTH

LANES = 128
SUBLANES = 8
N_LEVELS = 7
N_WIN = N_LEVELS + 1
SEG = LANES // SUBLANES
STREAMS = 2
TOK_BLOCK = 256
VMEM_LIMIT = 56 * 1024 * 1024


def _chunk_tables():
    s = np.arange(LANES)[:, None]
    t = np.arange(LANES)[None, :]
    wins = []
    for l in range(N_LEVELS):
        h = (LANES // 2) >> l
        mid = (t // (2 * h)) * (2 * h) + h - 1
        role_q = (t & h) != 0
        wins.append(np.where(role_q, (s > mid) & (s <= t), (s > t) & (s <= mid)))
    wins.append(s <= t)
    win = np.concatenate(wins, axis=1).astype(np.float32)
    x = s ^ t
    lv = np.full((LANES, LANES), -1, np.int32)
    for l in range(N_LEVELS):
        h = (LANES // 2) >> l
        lv[(x >= h) & (x < 2 * h) & (t > s)] = l
    perm = np.zeros((LANES, LANES), np.float32)
    i, j = np.divmod(np.arange(LANES), SEG)
    perm[SUBLANES * j + i, np.arange(LANES)] = 1.0
    return np.concatenate([win, win], axis=0), np.tile(lv, (1, 2)), np.stack([perm, perm.T])


def _dot(a, b):
    return jnp.dot(a, b, preferred_element_type=F32)


def _dot_nt(a, b):
    return lax.dot_general(a, b, (((1,), (1,)), ((), ())), preferred_element_type=F32)


def _rmsnorm_rows(x, g):
    r = lax.rsqrt(jnp.mean(x * x, axis=-1, keepdims=True) + NORM_EPS)
    return x * r * g


def _lower_bound_consts(lbl_ref, layer, out_ref):
    rows = [lbl_ref[l] for l in range(lbl_ref.shape[0])]
    m = functools.reduce(jnp.maximum, rows)
    es = [jnp.exp(r - m) for r in rows]
    tot = functools.reduce(lambda a, b: a + b, es)
    sm = [e / tot for e in es]
    cum = sm[0]
    for l in range(1, layer + 1):
        cum = cum + sm[l]
    lb = cum - sm[0]
    out_ref[0] = jnp.log(lb)
    out_ref[1] = jnp.log(1.0 - lb)
    out_ref[2] = 1.0 - lb


def _hgrn_gates(fa, lbc_ref):
    e = jnp.exp(-jnp.abs(fa))
    logsig = jnp.minimum(fa, 0.0) - jnp.log(1.0 + e)
    a = lbc_ref[0]
    c = lbc_ref[1] + logsig
    log_f = jnp.maximum(a, c) + jnp.log(1.0 + jnp.exp(-jnp.abs(a - c)))
    k_a = lbc_ref[2] * (jnp.where(fa >= 0.0, e, 1.0) / (1.0 + e))
    return log_f, k_a


def _gla_log_gate_cm(rb, w2_ref, b2_ref):
    pre = _dot(rb.astype(BF16), w2_ref[...]) + b2_ref[...]
    return (jax.nn.log_sigmoid(pre) / GLA_TAU).T[:B_QK]


def _window_sums(g, win_ref):
    hi = g.astype(BF16)
    lo = (g - hi.astype(F32)).astype(BF16)
    return _dot(jnp.concatenate([hi, lo], axis=1), win_ref[...])


def _gated_chunk(load_q, load_k, load_v, e_ref, s_ref, heads, dk, dv, lv_ref, lane, emit):
    for p in range(heads // 2):
        rows = slice(2 * p * dk, (2 * p + 2) * dk)
        q, k = load_q(rows), load_k(rows)
        att = jnp.zeros((LANES, 2 * LANES), F32)
        for l in range(N_LEVELS):
            h = (LANES // 2) >> l
            z = jnp.where((lane & h) != 0, q, k) * jnp.exp(e_ref[rows, l * LANES:(l + 1) * LANES])
            zb = z.astype(BF16)
            zero = jnp.zeros((dk, LANES), BF16)
            rhs = jnp.concatenate([jnp.concatenate([zb[:dk], zero], axis=1),
                                   jnp.concatenate([zero, zb[dk:]], axis=1)], axis=0)
            gram = _dot(z.T.astype(BF16), rhs)
            att = jnp.where(lv_ref[...] == l, gram, att)
            yield
        attb = att.astype(BF16)
        bq = e_ref[rows, N_LEVELS * LANES:(N_LEVELS + 1) * LANES]
        b_last = bq[:, LANES - 1:LANES]
        qs = (q * jnp.exp(bq)).astype(BF16)
        kd = (k * jnp.exp(b_last - bq)).astype(BF16)
        dec = jnp.exp(b_last)
        qk = q * k
        for i in range(2):
            hd = 2 * p + i
            r = slice(i * dk, (i + 1) * dk)
            rs = slice(hd * dk, (hd + 1) * dk)
            v = load_v(slice(hd * dv, (hd + 1) * dv))
            vb = v.astype(BF16)
            s_old = s_ref[rs, :]
            o = _dot(s_old.T.astype(BF16), qs[r])
            o = o + _dot(vb, attb[:, i * LANES:(i + 1) * LANES])
            o = o + jnp.sum(qk[r], axis=0, keepdims=True) * v
            s_ref[rs, :] = dec[r] * s_old + _dot_nt(kd[r], vb)
            emit(hd, o)
            yield


def _head_rms(o):
    return o * lax.rsqrt(jnp.mean(o * o, axis=0, keepdims=True) + NORM_EPS)


def _cmul_add(ar, ai, xr, xi, br, bi):
    return br + ar * xr - ai * xi, bi + ar * xi + ai * xr


def _s5_readout(x_re, x_im, cb_ref):
    return _dot(jnp.concatenate([x_re, x_im], axis=1).astype(BF16), cb_ref[...])


def _s5_gate_tokens(y, u, zc, d_ref, wglu_ref, bglu_ref):
    y = y + d_ref[...] * u
    y = jax.nn.gelu(y)
    y = y * jax.nn.sigmoid(_dot(y.astype(BF16), wglu_ref[...]) + bglu_ref[...])
    return y * jax.nn.silu(zc)


def _project(x, ng_ref, win_ref, p_ref):
    h = _rmsnorm_rows(x, ng_ref[...])
    p_ref[...] = _dot(h.astype(BF16), win_ref[...])


def _mix_and_project(x, mixt_ref, za, zb, oc, hgain_ref, ggain_ref, wout_ref, fg_ref, final):
    o_t = mixt_ref[...].T
    o_a = o_t[:, :A_WIDTH] * hgain_ref[...] * jax.nn.silu(za)
    o_b = o_t[:, A_WIDTH:] * ggain_ref[...] * jax.nn.silu(zb)
    mix = jnp.concatenate([o_a, o_b, oc], axis=1).astype(BF16)
    out = x + _dot(mix, wout_ref[...])
    if final:
        out = _rmsnorm_rows(out, fg_ref[...])
    return out


def _cols(name):
    return slice(OFF[name], OFF[name] + _SIZE[name])


def _interleave(gens):
    live = list(gens)
    while live:
        live = [g for g in live if next(g, StopIteration) is not StopIteration]


def _prompt_kernel(layer, final, n_sub, tb,
                   x_ref, ng_ref, win_ref, lbl_ref, hgain_ref, w2_ref, b2_ref, ggain_ref,
                   bd_ref, cb_ref, dd_ref, wglu_ref, bglu_ref, wout_ref, fg_ref, aseg_ref,
                   tsub_ref, tcar_ref, tpos_ref, perm_ref, wwin_ref, lv_ref,
                   y_ref, shg_ref, sgla_ref, sre_ref, sim_ref,
                   p_ref, cma_ref, cmb_ref, ea_ref, eb_ref, mixt_ref, bu_ref,
                   hg_s, gla_s, cre_s, cim_s, lbc_s):
    t_idx = pl.program_id(1)
    sub = t_idx % n_sub

    @pl.when(t_idx == 0)
    def _():
        hg_s[...] = jnp.zeros_like(hg_s)
        gla_s[...] = jnp.zeros_like(gla_s)
        cre_s[...] = jnp.zeros_like(cre_s)
        cim_s[...] = jnp.zeros_like(cim_s)
        _lower_bound_consts(lbl_ref, layer, lbc_s)

    @pl.when(sub == 0)
    def _():
        _project(x_ref[...].reshape(STREAMS * tb, D_MODEL), ng_ref, win_ref, p_ref)

    lane = lax.broadcasted_iota(jnp.int32, (1, LANES), 1)

    def chunk_steps(k):
        tok = pl.ds(pl.multiple_of(sub * LANES, LANES), LANES)
        ptok = pl.ds(pl.multiple_of(k * tb + sub * LANES, LANES), LANES)
        cma, cmb, ea, eb, mixt, bu = (r.at[k] for r in (cma_ref, cmb_ref, ea_ref, eb_ref, mixt_ref, bu_ref))
        hg, gla, cre, cim = (r.at[k] for r in (hg_s, gla_s, cre_s, cim_s))
        cma[...] = p_ref[ptok, :A_CM].T
        cmb[...] = p_ref[ptok, OFF["qb"]:OFF["qb"] + B_CM].T
        yield

        log_f, k_a = _hgrn_gates(cma[A_WIDTH:2 * A_WIDTH, :], lbc_s)
        cma[A_WIDTH:2 * A_WIDTH, :] = k_a
        ea[...] = _window_sums(log_f, wwin_ref)
        yield

        def emit_a(hd, o):
            mixt[hd * A_DV:(hd + 1) * A_DV, :] = _head_rms(o)

        yield from _gated_chunk(lambda r: jax.nn.silu(cma[r, :]),
                                lambda r: cma[A_WIDTH + r.start:A_WIDTH + r.stop, :],
                                lambda r: cma[2 * A_WIDTH + r.start:2 * A_WIDTH + r.stop, :],
                                ea, hg, A_HEADS, A_DK, A_DV, lv_ref, lane, emit_a)

        eb[...] = _window_sums(_gla_log_gate_cm(p_ref[ptok, _cols("rb")], w2_ref, b2_ref), wwin_ref)
        yield

        def emit_b(hd, o):
            mixt[A_WIDTH + hd * B_DV:A_WIDTH + (hd + 1) * B_DV, :] = _head_rms(o)

        yield from _gated_chunk(lambda r: cmb[r, :] * (B_DK ** -0.5),
                                lambda r: cmb[B_QK + r.start:B_QK + r.stop, :],
                                lambda r: cmb[2 * B_QK + r.start:2 * B_QK + r.stop, :],
                                eb, gla, B_HEADS, B_DK, B_DV, lv_ref, lane, emit_b)

        u = p_ref[ptok, _cols("uc")]
        u_seg = _dot(perm_ref[0], u.astype(BF16)).astype(BF16)
        bu[...] = _dot(u_seg, bd_ref[...])
        yield
        ar, ai = aseg_ref[0], aseg_ref[1]
        xr = jnp.zeros((SUBLANES, S5_DIM), F32)
        xi = jnp.zeros((SUBLANES, S5_DIM), F32)
        for j in range(SEG):
            rows = slice(j * SUBLANES, (j + 1) * SUBLANES)
            xr, xi = _cmul_add(ar, ai, xr, xi, bu[rows, :S5_DIM], bu[rows, S5_DIM:])
            bu[rows, :S5_DIM] = xr
            bu[rows, S5_DIM:] = xi
            yield
        for i in range(3):
            xr, xi = _cmul_add(tsub_ref[i, 0], tsub_ref[i, 1],
                               pltpu.roll(xr, 1 << i, 0), pltpu.roll(xi, 1 << i, 0), xr, xi)
        cr, ci = cre[...], cim[...]
        xr, xi = _cmul_add(tcar_ref[0], tcar_ref[1], cr, ci, xr, xi)
        first = lax.broadcasted_iota(jnp.int32, (SUBLANES, S5_DIM), 0) == 0
        pr = jnp.where(first, cr, pltpu.roll(xr, 1, 0))
        pi = jnp.where(first, ci, pltpu.roll(xi, 1, 0))
        cre[...] = jnp.broadcast_to(xr[SUBLANES - 1:], (SUBLANES, S5_DIM))
        cim[...] = jnp.broadcast_to(xi[SUBLANES - 1:], (SUBLANES, S5_DIM))
        yield
        for j in range(SEG):
            rows = slice(j * SUBLANES, (j + 1) * SUBLANES)
            tr = jnp.broadcast_to(tpos_ref[0, j:j + 1, :], (SUBLANES, S5_DIM))
            ti = jnp.broadcast_to(tpos_ref[1, j:j + 1, :], (SUBLANES, S5_DIM))
            fr, fi = _cmul_add(tr, ti, pr, pi, bu[rows, :S5_DIM], bu[rows, S5_DIM:])
            bu[rows, :S5_DIM] = fr
            bu[rows, S5_DIM:] = fi
            yield
        y_seg = _dot(bu[...].astype(BF16), cb_ref[...])
        y1 = y_seg.astype(BF16)
        r1 = y_seg - y1.astype(F32)
        y2 = r1.astype(BF16)
        y3 = (r1 - y2.astype(F32)).astype(BF16)
        yield
        yp = _dot(perm_ref[1], jnp.concatenate([y1, y2, y3], axis=1))
        y_tok = yp[:, :C_WIDTH] + yp[:, C_WIDTH:2 * C_WIDTH] + yp[:, 2 * C_WIDTH:]
        yield
        oc = _s5_gate_tokens(y_tok, u, p_ref[ptok, _cols("zc")], dd_ref, wglu_ref, bglu_ref)
        yield
        y_ref[k, tok, :] = _mix_and_project(
            x_ref[k, tok, :], mixt, p_ref[ptok, _cols("za")], p_ref[ptok, _cols("zb")], oc,
            hgain_ref, ggain_ref, wout_ref, fg_ref, final)

    _interleave([chunk_steps(k) for k in range(STREAMS)])

    @pl.when(t_idx == pl.num_programs(1) - 1)
    def _():
        for k in range(STREAMS):
            for h in range(A_HEADS):
                shg_ref[k, h] = hg_s[k, h * A_DK:(h + 1) * A_DK, :]
            for h in range(B_HEADS):
                sgla_ref[k, h] = gla_s[k, h * B_DK:(h + 1) * B_DK, :]
            sre_ref[k] = cre_s[k, :1, :]
            sim_ref[k] = cim_s[k, :1, :]


def _sample_kernel(layer, final,
                   x_ref, ng_ref, win_ref, lbl_ref, hgain_ref, w2_ref, b2_ref, ggain_ref,
                   bd_ref, cb_ref, dd_ref, wglu_ref, bglu_ref, wout_ref, fg_ref, aseg_ref,
                   shg_in, sgla_in, sre_in, sim_in,
                   y_ref, shg_out, sgla_out, sre_out, sim_out,
                   p_ref, cma_ref, cmb_ref, fa_s, gb_s, mixt_ref, lbc_s):
    h = pl.program_id(0)

    @pl.when(h == 0)
    def _():
        _lower_bound_consts(lbl_ref, layer, lbc_s)
        _project(x_ref[...], ng_ref, win_ref, p_ref)
        cma_ref[...] = p_ref[:, :A_CM].T
        cmb_ref[...] = p_ref[:, OFF["qb"]:OFF["qb"] + B_CM].T
        log_f, k_a = _hgrn_gates(cma_ref[A_WIDTH:2 * A_WIDTH, :], lbc_s)
        fa_s[...] = jnp.exp(log_f)
        cma_ref[A_WIDTH:2 * A_WIDTH, :] = k_a
        cma_ref[:A_WIDTH, :] = jax.nn.silu(cma_ref[:A_WIDTH, :])
        gb_s[...] = jnp.exp(_gla_log_gate_cm(p_ref[:, _cols("rb")], w2_ref, b2_ref))

    ra = pl.multiple_of(h * A_DK, A_DK)
    v_a = cma_ref[pl.ds(2 * A_WIDTH + ra, A_DV), :]

    def hg_body(k, o):
        r = ra + k
        rows = pl.ds(pl.multiple_of(k * A_DV, A_DV), A_DV)
        new = fa_s[pl.ds(r, 1), :] * shg_in[rows, :] + cma_ref[pl.ds(A_WIDTH + r, 1), :] * v_a
        shg_out[rows, :] = new
        return o + cma_ref[pl.ds(r, 1), :] * new

    o = lax.fori_loop(0, A_DK, hg_body, jnp.zeros((A_DV, LANES), F32))
    mixt_ref[pl.ds(ra, A_DV), :] = _head_rms(o)

    rb = pl.multiple_of(h * B_DK, B_DK)
    vb0 = pl.multiple_of(h * B_DV, B_DV)
    v_b = cmb_ref[pl.ds(2 * B_QK + vb0, B_DV), :]

    def gla_body(k, o):
        r = rb + k
        rows = pl.ds(pl.multiple_of(k * B_DV, B_DV), B_DV)
        new = gb_s[pl.ds(r, 1), :] * sgla_in[rows, :] + cmb_ref[pl.ds(B_QK + r, 1), :] * v_b
        sgla_out[rows, :] = new
        return o + (cmb_ref[pl.ds(r, 1), :] * (B_DK ** -0.5)) * new

    o = lax.fori_loop(0, B_DK, gla_body, jnp.zeros((B_DV, LANES), F32))
    mixt_ref[pl.ds(A_WIDTH + vb0, B_DV), :] = _head_rms(o)

    @pl.when(h == pl.num_programs(0) - 1)
    def _():
        u = p_ref[:, _cols("uc")]
        bu = _dot(u.astype(BF16), bd_ref[...])
        x_re, x_im = _cmul_add(aseg_ref[0, :1], aseg_ref[1, :1], sre_in[...], sim_in[...],
                               bu[:, :S5_DIM], bu[:, S5_DIM:])
        sre_out[...] = x_re
        sim_out[...] = x_im
        oc = _s5_gate_tokens(_s5_readout(x_re, x_im, cb_ref), u, p_ref[:, _cols("zc")],
                             dd_ref, wglu_ref, bglu_ref)
        y_ref[...] = _mix_and_project(
            x_ref[...], mixt_ref, p_ref[:, _cols("za")], p_ref[:, _cols("zb")], oc,
            hgain_ref, ggain_ref, wout_ref, fg_ref, final)


def _lane_bcast(v):
    return jnp.broadcast_to(v.astype(F32)[..., None], v.shape + (LANES,))


def _row(v):
    return v.astype(F32).reshape(1, -1)


def _powers(ar, ai, n):
    pr, pi = ar[..., None], ai[..., None]
    while pr.shape[-1] < n:
        lr, li = pr[..., -1:], pi[..., -1:]
        pr, pi = (jnp.concatenate([pr, pr * lr - pi * li], axis=-1),
                  jnp.concatenate([pi, pr * li + pi * lr], axis=-1))
    return pr[..., :n], pi[..., :n]


def _group_block_diag(m):
    g, r, c = m.shape
    eye = jnp.eye(g, dtype=m.dtype)
    return (m[:, :, None, :] * eye[:, None, :, None]).reshape(g * r, g * c)


def _s5_tables(A_re, A_im, B_re, B_im, C_re, C_im, log_dt):
    dt = jnp.exp(log_dt.astype(F32))[:, None]
    lr, li = A_re.astype(F32), A_im.astype(F32)
    mag = jnp.exp(lr * dt)
    ab_re = mag * jnp.cos(li * dt)
    ab_im = mag * jnp.sin(li * dt)
    den = lr * lr + li * li
    nr = ab_re - 1.0
    co_re = (nr * lr + ab_im * li) / den
    co_im = (ab_im * lr - nr * li) / den
    Br, Bi = B_re.astype(F32), B_im.astype(F32)
    bb_re = co_re[..., None] * Br - co_im[..., None] * Bi
    bb_im = co_re[..., None] * Bi + co_im[..., None] * Br
    bd = jnp.concatenate([_group_block_diag(bb_re.transpose(0, 2, 1)),
                          _group_block_diag(bb_im.transpose(0, 2, 1))], axis=1).astype(BF16)
    cb = jnp.concatenate([_group_block_diag(C_re.astype(F32).transpose(0, 2, 1)),
                          _group_block_diag(-C_im.astype(F32).transpose(0, 2, 1))], axis=0).astype(BF16)
    pr, pi = _powers(ab_re.reshape(-1), ab_im.reshape(-1), LANES)
    pw = jnp.stack([pr.T, pi.T])
    aseg = jnp.broadcast_to(pw[:, :1], (2, SUBLANES, S5_DIM))
    sub = jnp.arange(SUBLANES)[None, :, None]
    tsub = jnp.stack([jnp.where(sub >= (1 << i), pw[:, SEG * (1 << i) - 1][:, None, :], 0.0) for i in range(3)])
    tcar = pw[:, SEG - 1::SEG]
    tpos = pw[:, :SEG]
    return bd, cb, aseg, tsub, tcar, tpos


def _const_spec(a):
    nd = a.ndim
    return pl.BlockSpec(a.shape, lambda *_, nd=nd: (0,) * nd)


def _layer_operands(l, norm_g, w_in, hg_lb_logits, hg_norm_g, gla_w2, gla_b2, gla_norm_g,
                    s5_A_re, s5_A_im, s5_B_re, s5_B_im, s5_C_re, s5_C_im, s5_D, s5_log_dt,
                    s5_w_glu, s5_b_glu, w_out, final_norm_g):
    bd, cb, aseg, tsub, tcar, tpos = _s5_tables(s5_A_re[l], s5_A_im[l], s5_B_re[l], s5_B_im[l],
                                                s5_C_re[l], s5_C_im[l], s5_log_dt[l])
    w = w_in[l]
    w_ker = jnp.concatenate([w[:, _REF_OFF[b]:_REF_OFF[b] + _SIZE[b]] for b in KER_BLOCKS], axis=1)
    pad = 2 * LANES - B_QK
    common = (
        _row(norm_g[l]),
        w_ker.astype(BF16),
        _lane_bcast(hg_lb_logits),
        _row(hg_norm_g[l]),
        jnp.pad(gla_w2[l], ((0, 0), (0, pad))).astype(BF16),
        jnp.pad(_row(gla_b2[l]), ((0, 0), (0, pad))),
        _row(gla_norm_g[l]),
        bd, cb,
        _row(s5_D[l]),
        s5_w_glu[l].astype(BF16),
        _row(s5_b_glu[l]),
        w_out[l].astype(BF16),
        _row(final_norm_g),
        aseg,
    )
    return common, (tsub, tcar, tpos)


def _prompt_layer(layer, final, x, common, scan_tables, chunk_tables):
    bsz, t, _ = x.shape
    tb = min(TOK_BLOCK, t)
    assert t % tb == 0 and tb % LANES == 0 and bsz % STREAMS == 0
    n_sub = tb // LANES
    consts = common + scan_tables + chunk_tables
    out_shape = (
        jax.ShapeDtypeStruct((bsz, t, D_MODEL), F32),
        jax.ShapeDtypeStruct((bsz, A_HEADS, A_DK, A_DV), F32),
        jax.ShapeDtypeStruct((bsz, B_HEADS, B_DK, B_DV), F32),
        jax.ShapeDtypeStruct((bsz, 1, S5_DIM), F32),
        jax.ShapeDtypeStruct((bsz, 1, S5_DIM), F32),
    )
    x_spec = pl.BlockSpec((STREAMS, tb, D_MODEL), lambda b, i: (b, i // n_sub, 0))
    out_specs = (
        x_spec,
        pl.BlockSpec((STREAMS, A_HEADS, A_DK, A_DV), lambda b, i: (b, 0, 0, 0)),
        pl.BlockSpec((STREAMS, B_HEADS, B_DK, B_DV), lambda b, i: (b, 0, 0, 0)),
        pl.BlockSpec((STREAMS, 1, S5_DIM), lambda b, i: (b, 0, 0)),
        pl.BlockSpec((STREAMS, 1, S5_DIM), lambda b, i: (b, 0, 0)),
    )
    scratch = [
        pltpu.VMEM((STREAMS * tb, IN_TOTAL), F32),
        pltpu.VMEM((STREAMS, A_CM, LANES), F32),
        pltpu.VMEM((STREAMS, B_CM, LANES), F32),
        pltpu.VMEM((STREAMS, A_WIDTH, N_WIN * LANES), F32),
        pltpu.VMEM((STREAMS, B_QK, N_WIN * LANES), F32),
        pltpu.VMEM((STREAMS, A_WIDTH + B_WIDTH, LANES), F32),
        pltpu.VMEM((STREAMS, LANES, 2 * S5_DIM), F32),
        pltpu.VMEM((STREAMS, A_WIDTH, A_DV), F32),
        pltpu.VMEM((STREAMS, B_QK, B_DV), F32),
        pltpu.VMEM((STREAMS, SUBLANES, S5_DIM), F32),
        pltpu.VMEM((STREAMS, SUBLANES, S5_DIM), F32),
        pltpu.VMEM((3, A_WIDTH, LANES), F32),
    ]
    return pl.pallas_call(
        functools.partial(_prompt_kernel, layer, final, n_sub, tb),
        grid=(bsz // STREAMS, t // LANES),
        in_specs=[x_spec] + [_const_spec(a) for a in consts],
        out_specs=out_specs,
        out_shape=out_shape,
        scratch_shapes=scratch,
        compiler_params=pltpu.CompilerParams(
            dimension_semantics=("arbitrary", "arbitrary"), vmem_limit_bytes=VMEM_LIMIT),
    )(x, *consts)


def _sample_layer(layer, final, xs, common, shg, sgla, sre, sim):
    n = xs.shape[0]
    assert n == LANES
    hg_rows, gla_rows = A_DK * A_DV, B_DK * B_DV
    out_shape = (
        jax.ShapeDtypeStruct((n, D_MODEL), F32),
        jax.ShapeDtypeStruct((A_HEADS * hg_rows, LANES), F32),
        jax.ShapeDtypeStruct((B_HEADS * gla_rows, LANES), F32),
        jax.ShapeDtypeStruct((n, S5_DIM), F32),
        jax.ShapeDtypeStruct((n, S5_DIM), F32),
    )
    full = lambda shape: pl.BlockSpec(shape, lambda h: (0, 0))
    state_specs = [
        pl.BlockSpec((hg_rows, LANES), lambda h: (h, 0)),
        pl.BlockSpec((gla_rows, LANES), lambda h: (h, 0)),
        full((n, S5_DIM)),
        full((n, S5_DIM)),
    ]
    scratch = [
        pltpu.VMEM((n, IN_TOTAL), F32),
        pltpu.VMEM((A_CM, LANES), F32),
        pltpu.VMEM((B_CM, LANES), F32),
        pltpu.VMEM((A_WIDTH, LANES), F32),
        pltpu.VMEM((B_QK, LANES), F32),
        pltpu.VMEM((A_WIDTH + B_WIDTH, LANES), F32),
        pltpu.VMEM((3, A_WIDTH, LANES), F32),
    ]
    return pl.pallas_call(
        functools.partial(_sample_kernel, layer, final),
        grid=(A_HEADS,),
        in_specs=[full((n, D_MODEL))] + [_const_spec(a) for a in common] + state_specs,
        out_specs=[full((n, D_MODEL))] + state_specs,
        out_shape=out_shape,
        scratch_shapes=scratch,
        compiler_params=pltpu.CompilerParams(
            dimension_semantics=("arbitrary",), vmem_limit_bytes=VMEM_LIMIT),
    )(xs, *common, shg, sgla, sre, sim)


def kernel(x_prompt, x_sample, state_hgrn, state_gla, state_s5_re, state_s5_im, norm_g, w_in, hg_lb_logits, hg_norm_g, gla_w2, gla_b2, gla_norm_g, s5_A_re, s5_A_im, s5_B_re, s5_B_im, s5_C_re, s5_C_im, s5_D, s5_log_dt, s5_w_glu, s5_b_glu, w_out, final_norm_g):
    depth = w_in.shape[0]
    bsz = x_prompt.shape[0]
    n = x_sample.shape[0]
    assert x_sample.shape[1] == 1
    win_np, lv_np, perm_np = _chunk_tables()
    chunk_tables = (jnp.asarray(perm_np, BF16), jnp.asarray(win_np, BF16), jnp.asarray(lv_np))

    yp = x_prompt.astype(F32)
    ys = x_sample.astype(F32).reshape(n, D_MODEL)
    prompt_states, sample_states = [], []
    for l in range(depth):
        final = l == depth - 1
        common, scan_tables = _layer_operands(
            l, norm_g, w_in, hg_lb_logits, hg_norm_g, gla_w2, gla_b2, gla_norm_g,
            s5_A_re, s5_A_im, s5_B_re, s5_B_im, s5_C_re, s5_C_im, s5_D, s5_log_dt,
            s5_w_glu, s5_b_glu, w_out, final_norm_g)
        yp, hg_p, gla_p, re_p, im_p = _prompt_layer(l, final, yp, common, scan_tables, chunk_tables)
        prompt_states.append((hg_p, gla_p,
                              re_p.reshape(bsz, C_GROUPS, C_STATE), im_p.reshape(bsz, C_GROUPS, C_STATE)))
        to_cm = lambda s: s.astype(F32).reshape(n, -1).T
        ys, hg_s, gla_s, re_s, im_s = _sample_layer(
            l, final, ys, common, to_cm(state_hgrn[l]), to_cm(state_gla[l]),
            state_s5_re[l].astype(F32).reshape(n, S5_DIM), state_s5_im[l].astype(F32).reshape(n, S5_DIM))
        sample_states.append((hg_s.T.reshape(n, A_HEADS, A_DK, A_DV),
                              gla_s.T.reshape(n, B_HEADS, B_DK, B_DV),
                              re_s.reshape(n, C_GROUPS, C_STATE),
                              im_s.reshape(n, C_GROUPS, C_STATE)))

    stack = lambda states, i: jnp.stack([s[i] for s in states], axis=0)
    return (yp, ys.reshape(n, 1, D_MODEL),
            stack(prompt_states, 0), stack(prompt_states, 1), stack(prompt_states, 2), stack(prompt_states, 3),
            stack(sample_states, 0), stack(sample_states, 1), stack(sample_states, 2), stack(sample_states, 3))
```

```python
import functools

import numpy as np
import jax
import jax.numpy as jnp
from jax import lax
from jax.experimental import pallas as pl
from jax.experimental.pallas import tpu as pltpu

F32 = jnp.float32
BF16 = jnp.bfloat16

D_MODEL = 1024
A_WIDTH, A_DK, A_DV, A_HEADS = 384, 64, 64, 6
B_WIDTH, B_DV, B_HEADS, B_DK, B_QK = 384, 64, 6, 32, 192
GLA_RANK = 16
GLA_TAU = 16.0
C_WIDTH, C_GROUP, C_GROUPS, C_STATE = 256, 16, 16, 64
S5_DIM = C_GROUPS * C_STATE
NORM_EPS = 1e-6
REF_BLOCKS = ("qa", "fa", "ia", "za", "qb", "kb", "vb", "zb", "rb", "uc", "zc")
REF_SIZES = (A_WIDTH, A_WIDTH, A_WIDTH, A_WIDTH, B_QK, B_QK, B_WIDTH, B_WIDTH, GLA_RANK, C_WIDTH, C_WIDTH)
KER_BLOCKS = ("qa", "fa", "ia", "za", "qb", "kb", "vb", "zb", "uc", "zc", "rb")
IN_TOTAL = sum(REF_SIZES)
_SIZE = dict(zip(REF_BLOCKS, REF_SIZES))
_REF_OFF = dict(zip(REF_BLOCKS, np.concatenate([[0], np.cumsum(REF_SIZES)[:-1]]).tolist()))
OFF = dict(zip(KER_BLOCKS, np.concatenate([[0], np.cumsum([_SIZE[b] for b in KER_BLOCKS])[:-1]]).tolist()))
A_CM = 3 * A_WIDTH
B_CM = 2 * B_QK + B_WIDTH

LANES = 128
SUBLANES = 8
N_LEVELS = 7
N_WIN = N_LEVELS + 1
SEG = LANES // SUBLANES
STREAMS = 2
PROJ_COLS = 256
PROJ_GAP = 6
VMEM_LIMIT = 56 * 1024 * 1024


def _chunk_tables():
    s = np.arange(LANES)[:, None]
    t = np.arange(LANES)[None, :]
    wins = []
    for l in range(N_LEVELS):
        h = (LANES // 2) >> l
        mid = (t // (2 * h)) * (2 * h) + h - 1
        role_q = (t & h) != 0
        wins.append(np.where(role_q, (s > mid) & (s <= t), (s > t) & (s <= mid)))
    wins.append(s <= t)
    win = np.concatenate(wins, axis=1).astype(np.float32)
    x = s ^ t
    lv = np.full((LANES, LANES), -1, np.int32)
    for l in range(N_LEVELS):
        h = (LANES // 2) >> l
        lv[(x >= h) & (x < 2 * h) & (t > s)] = l
    perm = np.zeros((LANES, LANES), np.float32)
    i, j = np.divmod(np.arange(LANES), SEG)
    perm[SUBLANES * j + i, np.arange(LANES)] = 1.0
    return np.concatenate([win, win], axis=0), np.tile(lv, (1, 2)), np.stack([perm, perm.T])


def _dot(a, b):
    return jnp.dot(a, b, preferred_element_type=F32)


def _dot_nt(a, b):
    return lax.dot_general(a, b, (((1,), (1,)), ((), ())), preferred_element_type=F32)


def _rmsnorm_rows(x, g):
    r = lax.rsqrt(jnp.mean(x * x, axis=-1, keepdims=True) + NORM_EPS)
    return x * r * g


def _lower_bound_consts(lbl_ref, layer, out_ref):
    rows = [lbl_ref[l] for l in range(lbl_ref.shape[0])]
    m = functools.reduce(jnp.maximum, rows)
    es = [jnp.exp(r - m) for r in rows]
    tot = functools.reduce(lambda a, b: a + b, es)
    sm = [e / tot for e in es]
    cum = sm[0]
    for l in range(1, layer + 1):
        cum = cum + sm[l]
    lb = cum - sm[0]
    out_ref[0] = jnp.log(lb)
    out_ref[1] = jnp.log(1.0 - lb)
    out_ref[2] = 1.0 - lb


def _hgrn_gates(fa, lbc_ref):
    e = jnp.exp(-jnp.abs(fa))
    logsig = jnp.minimum(fa, 0.0) - jnp.log(1.0 + e)
    a = lbc_ref[0]
    c = lbc_ref[1] + logsig
    log_f = jnp.maximum(a, c) + jnp.log(1.0 + jnp.exp(-jnp.abs(a - c)))
    k_a = lbc_ref[2] * (jnp.where(fa >= 0.0, e, 1.0) / (1.0 + e))
    return log_f, k_a


def _gla_log_gate_cm(rb, w2_ref, b2_ref):
    pre = _dot(rb.astype(BF16), w2_ref[...]) + b2_ref[...]
    return (jax.nn.log_sigmoid(pre) / GLA_TAU).T[:B_QK]


def _window_sums(g, win_ref):
    hi = g.astype(BF16)
    lo = (g - hi.astype(F32)).astype(BF16)
    return _dot(jnp.concatenate([hi, lo], axis=1), win_ref[...])


def _gated_chunk(load_q, load_k, load_v, e_ref, s_ref, heads, dk, dv, lv_ref, lane, emit):
    for p in range(heads // 2):
        rows = slice(2 * p * dk, (2 * p + 2) * dk)
        q, k = load_q(rows), load_k(rows)
        att = jnp.zeros((LANES, 2 * LANES), F32)
        for l in range(N_LEVELS):
            h = (LANES // 2) >> l
            z = jnp.where((lane & h) != 0, q, k) * jnp.exp(e_ref[rows, l * LANES:(l + 1) * LANES])
            zb = z.astype(BF16)
            zero = jnp.zeros((dk, LANES), BF16)
            rhs = jnp.concatenate([jnp.concatenate([zb[:dk], zero], axis=1),
                                   jnp.concatenate([zero, zb[dk:]], axis=1)], axis=0)
            gram = _dot(z.T.astype(BF16), rhs)
            att = jnp.where(lv_ref[...] == l, gram, att)
            yield
        attb = att.astype(BF16)
        bq = e_ref[rows, N_LEVELS * LANES:(N_LEVELS + 1) * LANES]
        b_last = bq[:, LANES - 1:LANES]
        qs = (q * jnp.exp(bq)).astype(BF16)
        kd = (k * jnp.exp(b_last - bq)).astype(BF16)
        dec = jnp.exp(b_last)
        qk = q * k
        for i in range(2):
            hd = 2 * p + i
            r = slice(i * dk, (i + 1) * dk)
            rs = slice(hd * dk, (hd + 1) * dk)
            v = load_v(slice(hd * dv, (hd + 1) * dv))
            vb = v.astype(BF16)
            s_old = s_ref[rs, :]
            o = _dot(s_old.T.astype(BF16), qs[r])
            o = o + _dot(vb, attb[:, i * LANES:(i + 1) * LANES])
            o = o + jnp.sum(qk[r], axis=0, keepdims=True) * v
            s_ref[rs, :] = dec[r] * s_old + _dot_nt(kd[r], vb)
            emit(hd, o)
            yield


def _head_rms(o):
    return o * lax.rsqrt(jnp.mean(o * o, axis=0, keepdims=True) + NORM_EPS)


def _cmul_add(ar, ai, xr, xi, br, bi):
    return br + ar * xr - ai * xi, bi + ar * xi + ai * xr


def _s5_readout(x_re, x_im, cb_ref):
    return _dot(jnp.concatenate([x_re, x_im], axis=1).astype(BF16), cb_ref[...])


def _s5_gate_tokens(y, u, zc, d_ref, wglu_ref, bglu_ref):
    y = y + d_ref[...] * u
    y = jax.nn.gelu(y)
    y = y * jax.nn.sigmoid(_dot(y.astype(BF16), wglu_ref[...]) + bglu_ref[...])
    return y * jax.nn.silu(zc)


def _project(x, ng_ref, win_ref, p_ref):
    h = _rmsnorm_rows(x, ng_ref[...])
    p_ref[...] = _dot(h.astype(BF16), win_ref[...])


def _project_steps(x, ng_ref, win_ref, hb_ref, p_ref):
    hb_ref[...] = _rmsnorm_rows(x, ng_ref[...]).astype(BF16)
    for c0 in range(0, IN_TOTAL, PROJ_COLS):
        for _ in range(PROJ_GAP + 1):
            yield
        cols = slice(c0, min(c0 + PROJ_COLS, IN_TOTAL))
        p_ref[:, cols] = _dot(hb_ref[...], win_ref[:, cols])


def _mix_and_project(x, mixt_ref, za, zb, oc, hgain_ref, ggain_ref, wout_ref, fg_ref, final):
    o_t = mixt_ref[...].T
    o_a = o_t[:, :A_WIDTH] * hgain_ref[...] * jax.nn.silu(za)
    o_b = o_t[:, A_WIDTH:] * ggain_ref[...] * jax.nn.silu(zb)
    mix = jnp.concatenate([o_a, o_b, oc], axis=1).astype(BF16)
    out = x + _dot(mix, wout_ref[...])
    if final:
        out = _rmsnorm_rows(out, fg_ref[...])
    return out


def _cols(name):
    return slice(OFF[name], OFF[name] + _SIZE[name])


def _interleave(gens):
    live = list(gens)
    while live:
        live = [g for g in live if next(g, StopIteration) is not StopIteration]


def _prompt_kernel(layer, final,
                   x_ref, xn_ref, ng_ref, win_ref, lbl_ref, hgain_ref, w2_ref, b2_ref, ggain_ref,
                   bd_ref, cb_ref, dd_ref, wglu_ref, bglu_ref, wout_ref, fg_ref, aseg_ref,
                   tsub_ref, tcar_ref, tpos_ref, perm_ref, wwin_ref, lv_ref,
                   y_ref, shg_ref, sgla_ref, sre_ref, sim_ref,
                   p_ref, hb_ref, cma_ref, cmb_ref, ea_ref, eb_ref, mixt_ref, bu_ref,
                   hg_s, gla_s, cre_s, cim_s, lbc_s):
    t_idx = pl.program_id(1)
    step = pl.program_id(0) * pl.num_programs(1) + t_idx
    cur = step % 2

    @pl.when(t_idx == 0)
    def _():
        hg_s[...] = jnp.zeros_like(hg_s)
        gla_s[...] = jnp.zeros_like(gla_s)
        cre_s[...] = jnp.zeros_like(cre_s)
        cim_s[...] = jnp.zeros_like(cim_s)
        _lower_bound_consts(lbl_ref, layer, lbc_s)

    @pl.when(step == 0)
    def _():
        _project(x_ref[...].reshape(STREAMS * LANES, D_MODEL), ng_ref, win_ref, p_ref.at[0])

    lane = lax.broadcasted_iota(jnp.int32, (1, LANES), 1)

    def chunk_steps(k):
        p_cur = p_ref.at[cur, k * LANES:(k + 1) * LANES]
        cma, cmb, ea, eb, mixt, bu = (r.at[k] for r in (cma_ref, cmb_ref, ea_ref, eb_ref, mixt_ref, bu_ref))
        hg, gla, cre, cim = (r.at[k] for r in (hg_s, gla_s, cre_s, cim_s))
        cma[...] = p_cur[:, :A_CM].T
        cmb[...] = p_cur[:, OFF["qb"]:OFF["qb"] + B_CM].T
        yield

        log_f, k_a = _hgrn_gates(cma[A_WIDTH:2 * A_WIDTH, :], lbc_s)
        cma[A_WIDTH:2 * A_WIDTH, :] = k_a
        ea[...] = _window_sums(log_f, wwin_ref)
        yield

        def emit_a(hd, o):
            mixt[hd * A_DV:(hd + 1) * A_DV, :] = _head_rms(o)

        yield from _gated_chunk(lambda r: jax.nn.silu(cma[r, :]),
                                lambda r: cma[A_WIDTH + r.start:A_WIDTH + r.stop, :],
                                lambda r: cma[2 * A_WIDTH + r.start:2 * A_WIDTH + r.stop, :],
                                ea, hg, A_HEADS, A_DK, A_DV, lv_ref, lane, emit_a)

        eb[...] = _window_sums(_gla_log_gate_cm(p_cur[:, _cols("rb")], w2_ref, b2_ref), wwin_ref)
        yield

        def emit_b(hd, o):
            mixt[A_WIDTH + hd * B_DV:A_WIDTH + (hd + 1) * B_DV, :] = _head_rms(o)

        yield from _gated_chunk(lambda r: cmb[r, :] * (B_DK ** -0.5),
                                lambda r: cmb[B_QK + r.start:B_QK + r.stop, :],
                                lambda r: cmb[2 * B_QK + r.start:2 * B_QK + r.stop, :],
                                eb, gla, B_HEADS, B_DK, B_DV, lv_ref, lane, emit_b)

        u = p_cur[:, _cols("uc")]
        u_seg = _dot(perm_ref[0], u.astype(BF16)).astype(BF16)
        bu[...] = _dot(u_seg, bd_ref[...])
        yield
        ar, ai = aseg_ref[0], aseg_ref[1]
        xr = jnp.zeros((SUBLANES, S5_DIM), F32)
        xi = jnp.zeros((SUBLANES, S5_DIM), F32)
        for j in range(SEG):
            rows = slice(j * SUBLANES, (j + 1) * SUBLANES)
            xr, xi = _cmul_add(ar, ai, xr, xi, bu[rows, :S5_DIM], bu[rows, S5_DIM:])
            bu[rows, :S5_DIM] = xr
            bu[rows, S5_DIM:] = xi
            yield
        for i in range(3):
            xr, xi = _cmul_add(tsub_ref[i, 0], tsub_ref[i, 1],
                               pltpu.roll(xr, 1 << i, 0), pltpu.roll(xi, 1 << i, 0), xr, xi)
        cr, ci = cre[...], cim[...]
        xr, xi = _cmul_add(tcar_ref[0], tcar_ref[1], cr, ci, xr, xi)
        first = lax.broadcasted_iota(jnp.int32, (SUBLANES, S5_DIM), 0) == 0
        pr = jnp.where(first, cr, pltpu.roll(xr, 1, 0))
        pi = jnp.where(first, ci, pltpu.roll(xi, 1, 0))
        cre[...] = jnp.broadcast_to(xr[SUBLANES - 1:], (SUBLANES, S5_DIM))
        cim[...] = jnp.broadcast_to(xi[SUBLANES - 1:], (SUBLANES, S5_DIM))
        yield
        for j in range(SEG):
            rows = slice(j * SUBLANES, (j + 1) * SUBLANES)
            tr = jnp.broadcast_to(tpos_ref[0, j:j + 1, :], (SUBLANES, S5_DIM))
            ti = jnp.broadcast_to(tpos_ref[1, j:j + 1, :], (SUBLANES, S5_DIM))
            fr, fi = _cmul_add(tr, ti, pr, pi, bu[rows, :S5_DIM], bu[rows, S5_DIM:])
            bu[rows, :S5_DIM] = fr
            bu[rows, S5_DIM:] = fi
            yield
        y_seg = _dot(bu[...].astype(BF16), cb_ref[...])
        y1 = y_seg.astype(BF16)
        r1 = y_seg - y1.astype(F32)
        y2 = r1.astype(BF16)
        y3 = (r1 - y2.astype(F32)).astype(BF16)
        yield
        yp = _dot(perm_ref[1], jnp.concatenate([y1, y2, y3], axis=1))
        y_tok = yp[:, :C_WIDTH] + yp[:, C_WIDTH:2 * C_WIDTH] + yp[:, 2 * C_WIDTH:]
        yield
        oc = _s5_gate_tokens(y_tok, u, p_cur[:, _cols("zc")], dd_ref, wglu_ref, bglu_ref)
        yield
        y_ref[k] = _mix_and_project(
            x_ref[k], mixt, p_cur[:, _cols("za")], p_cur[:, _cols("zb")], oc,
            hgain_ref, ggain_ref, wout_ref, fg_ref, final)

    project_next = _project_steps(xn_ref[...].reshape(STREAMS * LANES, D_MODEL), ng_ref, win_ref,
                                  hb_ref, p_ref.at[1 - cur])
    _interleave([chunk_steps(k) for k in range(STREAMS)] + [project_next])

    @pl.when(t_idx == pl.num_programs(1) - 1)
    def _():
        for k in range(STREAMS):
            for h in range(A_HEADS):
                shg_ref[k, h] = hg_s[k, h * A_DK:(h + 1) * A_DK, :]
            for h in range(B_HEADS):
                sgla_ref[k, h] = gla_s[k, h * B_DK:(h + 1) * B_DK, :]
            sre_ref[k] = cre_s[k, :1, :]
            sim_ref[k] = cim_s[k, :1, :]


def _sample_kernel(layer, final,
                   x_ref, ng_ref, win_ref, lbl_ref, hgain_ref, w2_ref, b2_ref, ggain_ref,
                   bd_ref, cb_ref, dd_ref, wglu_ref, bglu_ref, wout_ref, fg_ref, aseg_ref,
                   shg_in, sgla_in, sre_in, sim_in,
                   y_ref, shg_out, sgla_out, sre_out, sim_out,
                   p_ref, cma_ref, cmb_ref, fa_s, gb_s, mixt_ref, lbc_s):
    h = pl.program_id(0)

    @pl.when(h == 0)
    def _():
        _lower_bound_consts(lbl_ref, layer, lbc_s)
        _project(x_ref[...], ng_ref, win_ref, p_ref)
        cma_ref[...] = p_ref[:, :A_CM].T
        cmb_ref[...] = p_ref[:, OFF["qb"]:OFF["qb"] + B_CM].T
        log_f, k_a = _hgrn_gates(cma_ref[A_WIDTH:2 * A_WIDTH, :], lbc_s)
        fa_s[...] = jnp.exp(log_f)
        cma_ref[A_WIDTH:2 * A_WIDTH, :] = k_a
        cma_ref[:A_WIDTH, :] = jax.nn.silu(cma_ref[:A_WIDTH, :])
        gb_s[...] = jnp.exp(_gla_log_gate_cm(p_ref[:, _cols("rb")], w2_ref, b2_ref))

    ra = pl.multiple_of(h * A_DK, A_DK)
    v_a = cma_ref[pl.ds(2 * A_WIDTH + ra, A_DV), :]

    def hg_body(k, o):
        r = ra + k
        rows = pl.ds(pl.multiple_of(k * A_DV, A_DV), A_DV)
        new = fa_s[pl.ds(r, 1), :] * shg_in[rows, :] + cma_ref[pl.ds(A_WIDTH + r, 1), :] * v_a
        shg_out[rows, :] = new
        return o + cma_ref[pl.ds(r, 1), :] * new

    o = lax.fori_loop(0, A_DK, hg_body, jnp.zeros((A_DV, LANES), F32))
    mixt_ref[pl.ds(ra, A_DV), :] = _head_rms(o)

    rb = pl.multiple_of(h * B_DK, B_DK)
    vb0 = pl.multiple_of(h * B_DV, B_DV)
    v_b = cmb_ref[pl.ds(2 * B_QK + vb0, B_DV), :]

    def gla_body(k, o):
        r = rb + k
        rows = pl.ds(pl.multiple_of(k * B_DV, B_DV), B_DV)
        new = gb_s[pl.ds(r, 1), :] * sgla_in[rows, :] + cmb_ref[pl.ds(B_QK + r, 1), :] * v_b
        sgla_out[rows, :] = new
        return o + (cmb_ref[pl.ds(r, 1), :] * (B_DK ** -0.5)) * new

    o = lax.fori_loop(0, B_DK, gla_body, jnp.zeros((B_DV, LANES), F32))
    mixt_ref[pl.ds(A_WIDTH + vb0, B_DV), :] = _head_rms(o)

    @pl.when(h == pl.num_programs(0) - 1)
    def _():
        u = p_ref[:, _cols("uc")]
        bu = _dot(u.astype(BF16), bd_ref[...])
        x_re, x_im = _cmul_add(aseg_ref[0, :1], aseg_ref[1, :1], sre_in[...], sim_in[...],
                               bu[:, :S5_DIM], bu[:, S5_DIM:])
        sre_out[...] = x_re
        sim_out[...] = x_im
        oc = _s5_gate_tokens(_s5_readout(x_re, x_im, cb_ref), u, p_ref[:, _cols("zc")],
                             dd_ref, wglu_ref, bglu_ref)
        y_ref[...] = _mix_and_project(
            x_ref[...], mixt_ref, p_ref[:, _cols("za")], p_ref[:, _cols("zb")], oc,
            hgain_ref, ggain_ref, wout_ref, fg_ref, final)


def _lane_bcast(v):
    return jnp.broadcast_to(v.astype(F32)[..., None], v.shape + (LANES,))


def _row(v):
    return v.astype(F32).reshape(1, -1)


def _powers(ar, ai, n):
    pr, pi = ar[..., None], ai[..., None]
    while pr.shape[-1] < n:
        lr, li = pr[..., -1:], pi[..., -1:]
        pr, pi = (jnp.concatenate([pr, pr * lr - pi * li], axis=-1),
                  jnp.concatenate([pi, pr * li + pi * lr], axis=-1))
    return pr[..., :n], pi[..., :n]


def _group_block_diag(m):
    g, r, c = m.shape
    eye = jnp.eye(g, dtype=m.dtype)
    return (m[:, :, None, :] * eye[:, None, :, None]).reshape(g * r, g * c)


def _s5_tables(A_re, A_im, B_re, B_im, C_re, C_im, log_dt):
    dt = jnp.exp(log_dt.astype(F32))[:, None]
    lr, li = A_re.astype(F32), A_im.astype(F32)
    mag = jnp.exp(lr * dt)
    ab_re = mag * jnp.cos(li * dt)
    ab_im = mag * jnp.sin(li * dt)
    den = lr * lr + li * li
    nr = ab_re - 1.0
    co_re = (nr * lr + ab_im * li) / den
    co_im = (ab_im * lr - nr * li) / den
    Br, Bi = B_re.astype(F32), B_im.astype(F32)
    bb_re = co_re[..., None] * Br - co_im[..., None] * Bi
    bb_im = co_re[..., None] * Bi + co_im[..., None] * Br
    bd = jnp.concatenate([_group_block_diag(bb_re.transpose(0, 2, 1)),
                          _group_block_diag(bb_im.transpose(0, 2, 1))], axis=1).astype(BF16)
    cb = jnp.concatenate([_group_block_diag(C_re.astype(F32).transpose(0, 2, 1)),
                          _group_block_diag(-C_im.astype(F32).transpose(0, 2, 1))], axis=0).astype(BF16)
    pr, pi = _powers(ab_re.reshape(-1), ab_im.reshape(-1), LANES)
    pw = jnp.stack([pr.T, pi.T])
    aseg = jnp.broadcast_to(pw[:, :1], (2, SUBLANES, S5_DIM))
    sub = jnp.arange(SUBLANES)[None, :, None]
    tsub = jnp.stack([jnp.where(sub >= (1 << i), pw[:, SEG * (1 << i) - 1][:, None, :], 0.0) for i in range(3)])
    tcar = pw[:, SEG - 1::SEG]
    tpos = pw[:, :SEG]
    return bd, cb, aseg, tsub, tcar, tpos


def _const_spec(a):
    nd = a.ndim
    return pl.BlockSpec(a.shape, lambda *_, nd=nd: (0,) * nd)


def _layer_operands(l, norm_g, w_in, hg_lb_logits, hg_norm_g, gla_w2, gla_b2, gla_norm_g,
                    s5_A_re, s5_A_im, s5_B_re, s5_B_im, s5_C_re, s5_C_im, s5_D, s5_log_dt,
                    s5_w_glu, s5_b_glu, w_out, final_norm_g):
    bd, cb, aseg, tsub, tcar, tpos = _s5_tables(s5_A_re[l], s5_A_im[l], s5_B_re[l], s5_B_im[l],
                                                s5_C_re[l], s5_C_im[l], s5_log_dt[l])
    w = w_in[l]
    w_ker = jnp.concatenate([w[:, _REF_OFF[b]:_REF_OFF[b] + _SIZE[b]] for b in KER_BLOCKS], axis=1)
    pad = 2 * LANES - B_QK
    common = (
        _row(norm_g[l]),
        w_ker.astype(BF16),
        _lane_bcast(hg_lb_logits),
        _row(hg_norm_g[l]),
        jnp.pad(gla_w2[l], ((0, 0), (0, pad))).astype(BF16),
        jnp.pad(_row(gla_b2[l]), ((0, 0), (0, pad))),
        _row(gla_norm_g[l]),
        bd, cb,
        _row(s5_D[l]),
        s5_w_glu[l].astype(BF16),
        _row(s5_b_glu[l]),
        w_out[l].astype(BF16),
        _row(final_norm_g),
        aseg,
    )
    return common, (tsub, tcar, tpos)


def _prompt_layer(layer, final, x, common, scan_tables, chunk_tables):
    bsz, t, _ = x.shape
    assert t % LANES == 0 and bsz % STREAMS == 0
    nb, nt = bsz // STREAMS, t // LANES
    consts = common + scan_tables + chunk_tables
    out_shape = (
        jax.ShapeDtypeStruct((bsz, t, D_MODEL), F32),
        jax.ShapeDtypeStruct((bsz, A_HEADS, A_DK, A_DV), F32),
        jax.ShapeDtypeStruct((bsz, B_HEADS, B_DK, B_DV), F32),
        jax.ShapeDtypeStruct((bsz, 1, S5_DIM), F32),
        jax.ShapeDtypeStruct((bsz, 1, S5_DIM), F32),
    )
    x_spec = pl.BlockSpec((STREAMS, LANES, D_MODEL), lambda b, i: (b, i, 0))

    def next_block(b, i):
        nxt = jnp.minimum(b * nt + i + 1, nb * nt - 1)
        return (nxt // nt, nxt % nt, 0)

    out_specs = (
        x_spec,
        pl.BlockSpec((STREAMS, A_HEADS, A_DK, A_DV), lambda b, i: (b, 0, 0, 0)),
        pl.BlockSpec((STREAMS, B_HEADS, B_DK, B_DV), lambda b, i: (b, 0, 0, 0)),
        pl.BlockSpec((STREAMS, 1, S5_DIM), lambda b, i: (b, 0, 0)),
        pl.BlockSpec((STREAMS, 1, S5_DIM), lambda b, i: (b, 0, 0)),
    )
    scratch = [
        pltpu.VMEM((2, STREAMS * LANES, IN_TOTAL), F32),
        pltpu.VMEM((STREAMS * LANES, D_MODEL), BF16),
        pltpu.VMEM((STREAMS, A_CM, LANES), F32),
        pltpu.VMEM((STREAMS, B_CM, LANES), F32),
        pltpu.VMEM((STREAMS, A_WIDTH, N_WIN * LANES), F32),
        pltpu.VMEM((STREAMS, B_QK, N_WIN * LANES), F32),
        pltpu.VMEM((STREAMS, A_WIDTH + B_WIDTH, LANES), F32),
        pltpu.VMEM((STREAMS, LANES, 2 * S5_DIM), F32),
        pltpu.VMEM((STREAMS, A_WIDTH, A_DV), F32),
        pltpu.VMEM((STREAMS, B_QK, B_DV), F32),
        pltpu.VMEM((STREAMS, SUBLANES, S5_DIM), F32),
        pltpu.VMEM((STREAMS, SUBLANES, S5_DIM), F32),
        pltpu.VMEM((3, A_WIDTH, LANES), F32),
    ]
    return pl.pallas_call(
        functools.partial(_prompt_kernel, layer, final),
        grid=(nb, nt),
        in_specs=[x_spec, pl.BlockSpec((STREAMS, LANES, D_MODEL), next_block)] + [_const_spec(a) for a in consts],
        out_specs=out_specs,
        out_shape=out_shape,
        scratch_shapes=scratch,
        compiler_params=pltpu.CompilerParams(
            dimension_semantics=("arbitrary", "arbitrary"), vmem_limit_bytes=VMEM_LIMIT),
    )(x, x, *consts)


def _sample_layer(layer, final, xs, common, shg, sgla, sre, sim):
    n = xs.shape[0]
    assert n == LANES
    hg_rows, gla_rows = A_DK * A_DV, B_DK * B_DV
    out_shape = (
        jax.ShapeDtypeStruct((n, D_MODEL), F32),
        jax.ShapeDtypeStruct((A_HEADS * hg_rows, LANES), F32),
        jax.ShapeDtypeStruct((B_HEADS * gla_rows, LANES), F32),
        jax.ShapeDtypeStruct((n, S5_DIM), F32),
        jax.ShapeDtypeStruct((n, S5_DIM), F32),
    )
    full = lambda shape: pl.BlockSpec(shape, lambda h: (0, 0))
    state_specs = [
        pl.BlockSpec((hg_rows, LANES), lambda h: (h, 0)),
        pl.BlockSpec((gla_rows, LANES), lambda h: (h, 0)),
        full((n, S5_DIM)),
        full((n, S5_DIM)),
    ]
    scratch = [
        pltpu.VMEM((n, IN_TOTAL), F32),
        pltpu.VMEM((A_CM, LANES), F32),
        pltpu.VMEM((B_CM, LANES), F32),
        pltpu.VMEM((A_WIDTH, LANES), F32),
        pltpu.VMEM((B_QK, LANES), F32),
        pltpu.VMEM((A_WIDTH + B_WIDTH, LANES), F32),
        pltpu.VMEM((3, A_WIDTH, LANES), F32),
    ]
    return pl.pallas_call(
        functools.partial(_sample_kernel, layer, final),
        grid=(A_HEADS,),
        in_specs=[full((n, D_MODEL))] + [_const_spec(a) for a in common] + state_specs,
        out_specs=[full((n, D_MODEL))] + state_specs,
        out_shape=out_shape,
        scratch_shapes=scratch,
        compiler_params=pltpu.CompilerParams(
            dimension_semantics=("arbitrary",), vmem_limit_bytes=VMEM_LIMIT),
    )(xs, *common, shg, sgla, sre, sim)


def kernel(x_prompt, x_sample, state_hgrn, state_gla, state_s5_re, state_s5_im, norm_g, w_in, hg_lb_logits, hg_norm_g, gla_w2, gla_b2, gla_norm_g, s5_A_re, s5_A_im, s5_B_re, s5_B_im, s5_C_re, s5_C_im, s5_D, s5_log_dt, s5_w_glu, s5_b_glu, w_out, final_norm_g):
    depth = w_in.shape[0]
    bsz = x_prompt.shape[0]
    n = x_sample.shape[0]
    assert x_sample.shape[1] == 1
    win_np, lv_np, perm_np = _chunk_tables()
    chunk_tables = (jnp.asarray(perm_np, BF16), jnp.asarray(win_np, BF16), jnp.asarray(lv_np))

    yp = x_prompt.astype(F32)
    ys = x_sample.astype(F32).reshape(n, D_MODEL)
    prompt_states, sample_states = [], []
    for l in range(depth):
        final = l == depth - 1
        common, scan_tables = _layer_operands(
            l, norm_g, w_in, hg_lb_logits, hg_norm_g, gla_w2, gla_b2, gla_norm_g,
            s5_A_re, s5_A_im, s5_B_re, s5_B_im, s5_C_re, s5_C_im, s5_D, s5_log_dt,
            s5_w_glu, s5_b_glu, w_out, final_norm_g)
        yp, hg_p, gla_p, re_p, im_p = _prompt_layer(l, final, yp, common, scan_tables, chunk_tables)
        prompt_states.append((hg_p, gla_p,
                              re_p.reshape(bsz, C_GROUPS, C_STATE), im_p.reshape(bsz, C_GROUPS, C_STATE)))
        to_cm = lambda s: s.astype(F32).reshape(n, -1).T
        ys, hg_s, gla_s, re_s, im_s = _sample_layer(
            l, final, ys, common, to_cm(state_hgrn[l]), to_cm(state_gla[l]),
            state_s5_re[l].astype(F32).reshape(n, S5_DIM), state_s5_im[l].astype(F32).reshape(n, S5_DIM))
        sample_states.append((hg_s.T.reshape(n, A_HEADS, A_DK, A_DV),
                              gla_s.T.reshape(n, B_HEADS, B_DK, B_DV),
                              re_s.reshape(n, C_GROUPS, C_STATE),
                              im_s.reshape(n, C_GROUPS, C_STATE)))

    stack = lambda states, i: jnp.stack([s[i] for s in states], axis=0)
    return (yp, ys.reshape(n, 1, D_MODEL),
            stack(prompt_states, 0), stack(prompt_states, 1), stack(prompt_states, 2), stack(prompt_states, 3),
            stack(sample_states, 0), stack(sample_states, 1), stack(sample_states, 2), stack(sample_states, 3))
```

```python
import functools

import numpy as np
import jax
import jax.numpy as jnp
from jax import lax
from jax.experimental import pallas as pl
from jax.experimental.pallas import tpu as pltpu

F32 = jnp.float32
BF16 = jnp.bfloat16

D_MODEL = 1024
A_WIDTH, A_DK, A_DV, A_HEADS = 384, 64, 64, 6
B_WIDTH, B_DV, B_HEADS, B_DK, B_QK = 384, 64, 6, 32, 192
GLA_RANK = 16
GLA_TAU = 16.0
C_WIDTH, C_GROUP, C_GROUPS, C_STATE = 256, 16, 16, 64
S5_DIM = C_GROUPS * C_STATE
NORM_EPS = 1e-6
REF_BLOCKS = ("qa", "fa", "ia", "za", "qb", "kb", "vb", "zb", "rb", "uc", "zc")
REF_SIZES = (A_WIDTH, A_WIDTH, A_WIDTH, A_WIDTH, B_QK, B_QK, B_WIDTH, B_WIDTH, GLA_RANK, C_WIDTH, C_WIDTH)
KER_BLOCKS = ("qa", "fa", "ia", "za", "qb", "kb", "vb", "zb", "uc", "zc", "rb")
IN_TOTAL = sum(REF_SIZES)
_SIZE = dict(zip(REF_BLOCKS, REF_SIZES))
_REF_OFF = dict(zip(REF_BLOCKS, np.concatenate([[0], np.cumsum(REF_SIZES)[:-1]]).tolist()))
OFF = dict(zip(KER_BLOCKS, np.concatenate([[0], np.cumsum([_SIZE[b] for b in KER_BLOCKS])[:-1]]).tolist()))
A_CM = 3 * A_WIDTH
B_CM = 2 * B_QK + B_WIDTH

LANES = 128
SUBLANES = 8
N_LEVELS = 7
N_WIN = N_LEVELS + 1
SEG = LANES // SUBLANES
STREAMS = 2
PROJ_COLS = 256
PROJ_GAP = 6
VMEM_LIMIT = 56 * 1024 * 1024


def _chunk_tables():
    s = np.arange(LANES)[:, None]
    t = np.arange(LANES)[None, :]
    wins = []
    for l in range(N_LEVELS):
        h = (LANES // 2) >> l
        mid = (t // (2 * h)) * (2 * h) + h - 1
        role_q = (t & h) != 0
        wins.append(np.where(role_q, (s > mid) & (s <= t), (s > t) & (s <= mid)))
    wins.append(s <= t)
    win = np.concatenate(wins, axis=1).astype(np.float32)
    x = s ^ t
    lv = np.full((LANES, LANES), -1, np.int32)
    for l in range(N_LEVELS):
        h = (LANES // 2) >> l
        lv[(x >= h) & (x < 2 * h) & (t > s)] = l
    perm = np.zeros((LANES, LANES), np.float32)
    i, j = np.divmod(np.arange(LANES), SEG)
    perm[SUBLANES * j + i, np.arange(LANES)] = 1.0
    return np.concatenate([win, win], axis=0), np.tile(lv, (1, 2)), np.stack([perm, perm.T])


def _dot(a, b):
    return jnp.dot(a, b, preferred_element_type=F32)


def _dot_nt(a, b):
    return lax.dot_general(a, b, (((1,), (1,)), ((), ())), preferred_element_type=F32)


def _rmsnorm_rows(x, g):
    r = lax.rsqrt(jnp.mean(x * x, axis=-1, keepdims=True) + NORM_EPS)
    return x * r * g


def _lower_bound_consts(lbl_ref, layer, out_ref):
    rows = [lbl_ref[l] for l in range(lbl_ref.shape[0])]
    m = functools.reduce(jnp.maximum, rows)
    es = [jnp.exp(r - m) for r in rows]
    tot = functools.reduce(lambda a, b: a + b, es)
    sm = [e / tot for e in es]
    cum, lbs = sm[0], [sm[0] - sm[0]]
    for l in range(1, len(sm)):
        cum = cum + sm[l]
        lbs.append(cum - sm[0])
    if isinstance(layer, int):
        lb = lbs[layer]
    else:
        lb = lbs[0]
        for l in range(1, len(lbs)):
            lb = jnp.where(layer == l, lbs[l], lb)
    out_ref[0] = jnp.log(lb)
    out_ref[1] = jnp.log(1.0 - lb)
    out_ref[2] = 1.0 - lb


def _hgrn_gates(fa, lbc_ref):
    e = jnp.exp(-jnp.abs(fa))
    logsig = jnp.minimum(fa, 0.0) - jnp.log(1.0 + e)
    a = lbc_ref[0]
    c = lbc_ref[1] + logsig
    log_f = jnp.maximum(a, c) + jnp.log(1.0 + jnp.exp(-jnp.abs(a - c)))
    k_a = lbc_ref[2] * (jnp.where(fa >= 0.0, e, 1.0) / (1.0 + e))
    return log_f, k_a


def _gla_log_gate_cm(rb, w2_ref, b2_ref):
    pre = _dot(rb.astype(BF16), w2_ref[...]) + b2_ref[...]
    return (jax.nn.log_sigmoid(pre) / GLA_TAU).T[:B_QK]


def _window_sums(g, win_ref):
    hi = g.astype(BF16)
    lo = (g - hi.astype(F32)).astype(BF16)
    return _dot(jnp.concatenate([hi, lo], axis=1), win_ref[...])


def _gated_chunk(load_q, load_k, load_v, e_ref, s_ref, heads, dk, dv, lv_ref, lane, emit):
    for p in range(heads // 2):
        rows = slice(2 * p * dk, (2 * p + 2) * dk)
        q, k = load_q(rows), load_k(rows)
        att = jnp.zeros((LANES, 2 * LANES), F32)
        for l in range(N_LEVELS):
            h = (LANES // 2) >> l
            z = jnp.where((lane & h) != 0, q, k) * jnp.exp(e_ref[rows, l * LANES:(l + 1) * LANES])
            zb = z.astype(BF16)
            zero = jnp.zeros((dk, LANES), BF16)
            rhs = jnp.concatenate([jnp.concatenate([zb[:dk], zero], axis=1),
                                   jnp.concatenate([zero, zb[dk:]], axis=1)], axis=0)
            gram = _dot(z.T.astype(BF16), rhs)
            att = jnp.where(lv_ref[...] == l, gram, att)
            yield
        attb = att.astype(BF16)
        bq = e_ref[rows, N_LEVELS * LANES:(N_LEVELS + 1) * LANES]
        b_last = bq[:, LANES - 1:LANES]
        qs = (q * jnp.exp(bq)).astype(BF16)
        kd = (k * jnp.exp(b_last - bq)).astype(BF16)
        dec = jnp.exp(b_last)
        qk = q * k
        for i in range(2):
            hd = 2 * p + i
            r = slice(i * dk, (i + 1) * dk)
            rs = slice(hd * dk, (hd + 1) * dk)
            v = load_v(slice(hd * dv, (hd + 1) * dv))
            vb = v.astype(BF16)
            s_old = s_ref[rs, :]
            o = _dot(s_old.T.astype(BF16), qs[r])
            o = o + _dot(vb, attb[:, i * LANES:(i + 1) * LANES])
            o = o + jnp.sum(qk[r], axis=0, keepdims=True) * v
            s_ref[rs, :] = dec[r] * s_old + _dot_nt(kd[r], vb)
            emit(hd, o)
            yield


def _head_rms(o):
    return o * lax.rsqrt(jnp.mean(o * o, axis=0, keepdims=True) + NORM_EPS)


def _cmul_add(ar, ai, xr, xi, br, bi):
    return br + ar * xr - ai * xi, bi + ar * xi + ai * xr


def _s5_readout(x_re, x_im, cb_ref):
    return _dot(jnp.concatenate([x_re, x_im], axis=1).astype(BF16), cb_ref[...])


def _s5_gate_tokens(y, u, zc, d_ref, wglu_ref, bglu_ref):
    y = y + d_ref[...] * u
    y = jax.nn.gelu(y)
    y = y * jax.nn.sigmoid(_dot(y.astype(BF16), wglu_ref[...]) + bglu_ref[...])
    return y * jax.nn.silu(zc)


def _project(x, ng_ref, win_ref, p_ref):
    h = _rmsnorm_rows(x, ng_ref[...])
    p_ref[...] = _dot(h.astype(BF16), win_ref[...])


def _project_steps(x, ng_ref, win_ref, hb_ref, p_ref):
    hb_ref[...] = _rmsnorm_rows(x, ng_ref[...]).astype(BF16)
    for c0 in range(0, IN_TOTAL, PROJ_COLS):
        for _ in range(PROJ_GAP + 1):
            yield
        cols = slice(c0, min(c0 + PROJ_COLS, IN_TOTAL))
        p_ref[:, cols] = _dot(hb_ref[...], win_ref[:, cols])


def _mix_and_project(x, mixt_ref, za, zb, oc, hgain_ref, ggain_ref, wout_ref, fg_ref, final):
    o_t = mixt_ref[...].T
    o_a = o_t[:, :A_WIDTH] * hgain_ref[...] * jax.nn.silu(za)
    o_b = o_t[:, A_WIDTH:] * ggain_ref[...] * jax.nn.silu(zb)
    mix = jnp.concatenate([o_a, o_b, oc], axis=1).astype(BF16)
    out = x + _dot(mix, wout_ref[...])
    if final:
        out = _rmsnorm_rows(out, fg_ref[...])
    return out


def _cols(name):
    return slice(OFF[name], OFF[name] + _SIZE[name])


def _interleave(gens):
    live = list(gens)
    while live:
        live = [g for g in live if next(g, StopIteration) is not StopIteration]


def _prompt_kernel(layer, final,
                   x_ref, xn_ref, ng_ref, win_ref, lbl_ref, hgain_ref, w2_ref, b2_ref, ggain_ref,
                   bd_ref, cb_ref, dd_ref, wglu_ref, bglu_ref, wout_ref, fg_ref, aseg_ref,
                   tsub_ref, tcar_ref, tpos_ref, perm_ref, wwin_ref, lv_ref,
                   y_ref, shg_ref, sgla_ref, sre_ref, sim_ref,
                   p_ref, hb_ref, cma_ref, cmb_ref, ea_ref, eb_ref, mixt_ref, bu_ref,
                   hg_s, gla_s, cre_s, cim_s, lbc_s):
    t_idx = pl.program_id(1)
    step = pl.program_id(0) * pl.num_programs(1) + t_idx
    cur = step % 2

    @pl.when(t_idx == 0)
    def _():
        hg_s[...] = jnp.zeros_like(hg_s)
        gla_s[...] = jnp.zeros_like(gla_s)
        cre_s[...] = jnp.zeros_like(cre_s)
        cim_s[...] = jnp.zeros_like(cim_s)
        _lower_bound_consts(lbl_ref, layer, lbc_s)

    @pl.when(step == 0)
    def _():
        _project(x_ref[...].reshape(STREAMS * LANES, D_MODEL), ng_ref, win_ref, p_ref.at[0])

    lane = lax.broadcasted_iota(jnp.int32, (1, LANES), 1)

    def chunk_steps(k):
        p_cur = p_ref.at[cur, k * LANES:(k + 1) * LANES]
        cma, cmb, ea, eb, mixt, bu = (r.at[k] for r in (cma_ref, cmb_ref, ea_ref, eb_ref, mixt_ref, bu_ref))
        hg, gla, cre, cim = (r.at[k] for r in (hg_s, gla_s, cre_s, cim_s))
        cma[...] = p_cur[:, :A_CM].T
        cmb[...] = p_cur[:, OFF["qb"]:OFF["qb"] + B_CM].T
        yield

        log_f, k_a = _hgrn_gates(cma[A_WIDTH:2 * A_WIDTH, :], lbc_s)
        cma[A_WIDTH:2 * A_WIDTH, :] = k_a
        ea[...] = _window_sums(log_f, wwin_ref)
        yield

        def emit_a(hd, o):
            mixt[hd * A_DV:(hd + 1) * A_DV, :] = _head_rms(o)

        yield from _gated_chunk(lambda r: jax.nn.silu(cma[r, :]),
                                lambda r: cma[A_WIDTH + r.start:A_WIDTH + r.stop, :],
                                lambda r: cma[2 * A_WIDTH + r.start:2 * A_WIDTH + r.stop, :],
                                ea, hg, A_HEADS, A_DK, A_DV, lv_ref, lane, emit_a)

        eb[...] = _window_sums(_gla_log_gate_cm(p_cur[:, _cols("rb")], w2_ref, b2_ref), wwin_ref)
        yield

        def emit_b(hd, o):
            mixt[A_WIDTH + hd * B_DV:A_WIDTH + (hd + 1) * B_DV, :] = _head_rms(o)

        yield from _gated_chunk(lambda r: cmb[r, :] * (B_DK ** -0.5),
                                lambda r: cmb[B_QK + r.start:B_QK + r.stop, :],
                                lambda r: cmb[2 * B_QK + r.start:2 * B_QK + r.stop, :],
                                eb, gla, B_HEADS, B_DK, B_DV, lv_ref, lane, emit_b)

        u = p_cur[:, _cols("uc")]
        u_seg = _dot(perm_ref[0], u.astype(BF16)).astype(BF16)
        bu[...] = _dot(u_seg, bd_ref[...])
        yield
        ar, ai = aseg_ref[0], aseg_ref[1]
        xr = jnp.zeros((SUBLANES, S5_DIM), F32)
        xi = jnp.zeros((SUBLANES, S5_DIM), F32)
        for j in range(SEG):
            rows = slice(j * SUBLANES, (j + 1) * SUBLANES)
            xr, xi = _cmul_add(ar, ai, xr, xi, bu[rows, :S5_DIM], bu[rows, S5_DIM:])
            bu[rows, :S5_DIM] = xr
            bu[rows, S5_DIM:] = xi
            yield
        for i in range(3):
            xr, xi = _cmul_add(tsub_ref[i, 0], tsub_ref[i, 1],
                               pltpu.roll(xr, 1 << i, 0), pltpu.roll(xi, 1 << i, 0), xr, xi)
        cr, ci = cre[...], cim[...]
        xr, xi = _cmul_add(tcar_ref[0], tcar_ref[1], cr, ci, xr, xi)
        first = lax.broadcasted_iota(jnp.int32, (SUBLANES, S5_DIM), 0) == 0
        pr = jnp.where(first, cr, pltpu.roll(xr, 1, 0))
        pi = jnp.where(first, ci, pltpu.roll(xi, 1, 0))
        cre[...] = jnp.broadcast_to(xr[SUBLANES - 1:], (SUBLANES, S5_DIM))
        cim[...] = jnp.broadcast_to(xi[SUBLANES - 1:], (SUBLANES, S5_DIM))
        yield
        for j in range(SEG):
            rows = slice(j * SUBLANES, (j + 1) * SUBLANES)
            tr = jnp.broadcast_to(tpos_ref[0, j:j + 1, :], (SUBLANES, S5_DIM))
            ti = jnp.broadcast_to(tpos_ref[1, j:j + 1, :], (SUBLANES, S5_DIM))
            fr, fi = _cmul_add(tr, ti, pr, pi, bu[rows, :S5_DIM], bu[rows, S5_DIM:])
            bu[rows, :S5_DIM] = fr
            bu[rows, S5_DIM:] = fi
            yield
        y_seg = _dot(bu[...].astype(BF16), cb_ref[...])
        y1 = y_seg.astype(BF16)
        r1 = y_seg - y1.astype(F32)
        y2 = r1.astype(BF16)
        y3 = (r1 - y2.astype(F32)).astype(BF16)
        yield
        yp = _dot(perm_ref[1], jnp.concatenate([y1, y2, y3], axis=1))
        y_tok = yp[:, :C_WIDTH] + yp[:, C_WIDTH:2 * C_WIDTH] + yp[:, 2 * C_WIDTH:]
        yield
        oc = _s5_gate_tokens(y_tok, u, p_cur[:, _cols("zc")], dd_ref, wglu_ref, bglu_ref)
        yield
        y_ref[k] = _mix_and_project(
            x_ref[k], mixt, p_cur[:, _cols("za")], p_cur[:, _cols("zb")], oc,
            hgain_ref, ggain_ref, wout_ref, fg_ref, final)

    project_next = _project_steps(xn_ref[...].reshape(STREAMS * LANES, D_MODEL), ng_ref, win_ref,
                                  hb_ref, p_ref.at[1 - cur])
    _interleave([chunk_steps(k) for k in range(STREAMS)] + [project_next])

    @pl.when(t_idx == pl.num_programs(1) - 1)
    def _():
        for k in range(STREAMS):
            for h in range(A_HEADS):
                shg_ref[k, h] = hg_s[k, h * A_DK:(h + 1) * A_DK, :]
            for h in range(B_HEADS):
                sgla_ref[k, h] = gla_s[k, h * B_DK:(h + 1) * B_DK, :]
            sre_ref[k] = cre_s[k, :1, :]
            sim_ref[k] = cim_s[k, :1, :]


def _sample_kernel(x_ref, ng_ref, win_ref, lbl_ref, hgain_ref, w2_ref, b2_ref, ggain_ref,
                   bd_ref, cb_ref, dd_ref, wglu_ref, bglu_ref, wout_ref, fg_ref, aseg_ref,
                   shg_in, sgla_in, sre_in, sim_in,
                   y_ref, shg_out, sgla_out, sre_out, sim_out,
                   xs_s, p_ref, cma_ref, cmb_ref, fa_s, gb_s, mixt_ref, sa_s, sb_s, lbc_s):
    l = pl.program_id(0)
    h = pl.program_id(1)

    @pl.when((l == 0) & (h == 0))
    def _():
        xs_s[...] = x_ref[...]

    @pl.when(h == 0)
    def _():
        _lower_bound_consts(lbl_ref, l, lbc_s)
        _project(xs_s[...], ng_ref, win_ref, p_ref)
        cma_ref[...] = p_ref[:, :A_CM].T
        cmb_ref[...] = p_ref[:, OFF["qb"]:OFF["qb"] + B_CM].T
        log_f, k_a = _hgrn_gates(cma_ref[A_WIDTH:2 * A_WIDTH, :], lbc_s)
        fa_s[...] = jnp.exp(log_f)
        cma_ref[A_WIDTH:2 * A_WIDTH, :] = k_a
        cma_ref[:A_WIDTH, :] = jax.nn.silu(cma_ref[:A_WIDTH, :])
        gb_s[...] = jnp.exp(_gla_log_gate_cm(p_ref[:, _cols("rb")], w2_ref, b2_ref))

    sa_s[...] = shg_in[...].T
    ra = pl.multiple_of(h * A_DK, A_DK)
    v_a = cma_ref[pl.ds(2 * A_WIDTH + ra, A_DV), :]

    def hg_body(k, o):
        r = ra + k
        rows = pl.ds(pl.multiple_of(k * A_DV, A_DV), A_DV)
        new = fa_s[pl.ds(r, 1), :] * sa_s[rows, :] + cma_ref[pl.ds(A_WIDTH + r, 1), :] * v_a
        sa_s[rows, :] = new
        return o + cma_ref[pl.ds(r, 1), :] * new

    o = lax.fori_loop(0, A_DK, hg_body, jnp.zeros((A_DV, LANES), F32))
    mixt_ref[pl.ds(ra, A_DV), :] = _head_rms(o)
    shg_out[...] = sa_s[...].T

    sb_s[...] = sgla_in[...].T
    rb = pl.multiple_of(h * B_DK, B_DK)
    vb0 = pl.multiple_of(h * B_DV, B_DV)
    v_b = cmb_ref[pl.ds(2 * B_QK + vb0, B_DV), :]

    def gla_body(k, o):
        r = rb + k
        rows = pl.ds(pl.multiple_of(k * B_DV, B_DV), B_DV)
        new = gb_s[pl.ds(r, 1), :] * sb_s[rows, :] + cmb_ref[pl.ds(B_QK + r, 1), :] * v_b
        sb_s[rows, :] = new
        return o + (cmb_ref[pl.ds(r, 1), :] * (B_DK ** -0.5)) * new

    o = lax.fori_loop(0, B_DK, gla_body, jnp.zeros((B_DV, LANES), F32))
    mixt_ref[pl.ds(A_WIDTH + vb0, B_DV), :] = _head_rms(o)
    sgla_out[...] = sb_s[...].T

    @pl.when(h == pl.num_programs(1) - 1)
    def _():
        u = p_ref[:, _cols("uc")]
        bu = _dot(u.astype(BF16), bd_ref[...])
        x_re, x_im = _cmul_add(aseg_ref[0, :1], aseg_ref[1, :1], sre_in[...], sim_in[...],
                               bu[:, :S5_DIM], bu[:, S5_DIM:])
        sre_out[...] = x_re
        sim_out[...] = x_im
        oc = _s5_gate_tokens(_s5_readout(x_re, x_im, cb_ref), u, p_ref[:, _cols("zc")],
                             dd_ref, wglu_ref, bglu_ref)
        out = _mix_and_project(
            xs_s[...], mixt_ref, p_ref[:, _cols("za")], p_ref[:, _cols("zb")], oc,
            hgain_ref, ggain_ref, wout_ref, fg_ref, False)
        xs_s[...] = out

        @pl.when(l == pl.num_programs(0) - 1)
        def _():
            y_ref[...] = _rmsnorm_rows(out, fg_ref[...])


def _lane_bcast(v):
    return jnp.broadcast_to(v.astype(F32)[..., None], v.shape + (LANES,))


def _row(v):
    return v.astype(F32).reshape(1, -1)


def _powers(ar, ai, n):
    pr, pi = ar[..., None], ai[..., None]
    while pr.shape[-1] < n:
        lr, li = pr[..., -1:], pi[..., -1:]
        pr, pi = (jnp.concatenate([pr, pr * lr - pi * li], axis=-1),
                  jnp.concatenate([pi, pr * li + pi * lr], axis=-1))
    return pr[..., :n], pi[..., :n]


def _group_block_diag(m):
    g, r, c = m.shape
    eye = jnp.eye(g, dtype=m.dtype)
    return (m[:, :, None, :] * eye[:, None, :, None]).reshape(g * r, g * c)


def _s5_tables(A_re, A_im, B_re, B_im, C_re, C_im, log_dt):
    dt = jnp.exp(log_dt.astype(F32))[:, None]
    lr, li = A_re.astype(F32), A_im.astype(F32)
    mag = jnp.exp(lr * dt)
    ab_re = mag * jnp.cos(li * dt)
    ab_im = mag * jnp.sin(li * dt)
    den = lr * lr + li * li
    nr = ab_re - 1.0
    co_re = (nr * lr + ab_im * li) / den
    co_im = (ab_im * lr - nr * li) / den
    Br, Bi = B_re.astype(F32), B_im.astype(F32)
    bb_re = co_re[..., None] * Br - co_im[..., None] * Bi
    bb_im = co_re[..., None] * Bi + co_im[..., None] * Br
    bd = jnp.concatenate([_group_block_diag(bb_re.transpose(0, 2, 1)),
                          _group_block_diag(bb_im.transpose(0, 2, 1))], axis=1).astype(BF16)
    cb = jnp.concatenate([_group_block_diag(C_re.astype(F32).transpose(0, 2, 1)),
                          _group_block_diag(-C_im.astype(F32).transpose(0, 2, 1))], axis=0).astype(BF16)
    pr, pi = _powers(ab_re.reshape(-1), ab_im.reshape(-1), LANES)
    pw = jnp.stack([pr.T, pi.T])
    aseg = jnp.broadcast_to(pw[:, :1], (2, SUBLANES, S5_DIM))
    sub = jnp.arange(SUBLANES)[None, :, None]
    tsub = jnp.stack([jnp.where(sub >= (1 << i), pw[:, SEG * (1 << i) - 1][:, None, :], 0.0) for i in range(3)])
    tcar = pw[:, SEG - 1::SEG]
    tpos = pw[:, :SEG]
    return bd, cb, aseg, tsub, tcar, tpos


def _layer_spec(a, layer):
    nd = a.ndim - 1
    if layer is None:
        return pl.BlockSpec((None,) + a.shape[1:], lambda l, *_, nd=nd: (l,) + (0,) * nd)
    return pl.BlockSpec((None,) + a.shape[1:], lambda *_, nd=nd: (layer,) + (0,) * nd)


def _const_spec(a):
    nd = a.ndim
    return pl.BlockSpec(a.shape, lambda *_, nd=nd: (0,) * nd)


def _operands(norm_g, w_in, hg_lb_logits, hg_norm_g, gla_w2, gla_b2, gla_norm_g,
              s5_A_re, s5_A_im, s5_B_re, s5_B_im, s5_C_re, s5_C_im, s5_D, s5_log_dt,
              s5_w_glu, s5_b_glu, w_out, final_norm_g):
    bd, cb, aseg, tsub, tcar, tpos = jax.vmap(_s5_tables)(
        s5_A_re, s5_A_im, s5_B_re, s5_B_im, s5_C_re, s5_C_im, s5_log_dt)
    w_ker = jnp.concatenate([w_in[..., _REF_OFF[b]:_REF_OFF[b] + _SIZE[b]] for b in KER_BLOCKS], axis=-1)
    pad = 2 * LANES - B_QK
    rows = lambda v: v.astype(F32)[:, None, :]
    common = (
        (rows(norm_g), True),
        (w_ker.astype(BF16), True),
        (_lane_bcast(hg_lb_logits), False),
        (rows(hg_norm_g), True),
        (jnp.pad(gla_w2, ((0, 0), (0, 0), (0, pad))).astype(BF16), True),
        (jnp.pad(rows(gla_b2), ((0, 0), (0, 0), (0, pad))), True),
        (rows(gla_norm_g), True),
        (bd, True), (cb, True),
        (rows(s5_D), True),
        (s5_w_glu.astype(BF16), True),
        (rows(s5_b_glu), True),
        (w_out.astype(BF16), True),
        (_row(final_norm_g), False),
        (aseg, True),
    )
    return common, ((tsub, True), (tcar, True), (tpos, True))


def _prompt_layer(layer, final, x, common, scan_tables, chunk_tables):
    bsz, t, _ = x.shape
    assert t % LANES == 0 and bsz % STREAMS == 0
    nb, nt = bsz // STREAMS, t // LANES
    consts = common + scan_tables + tuple((a, False) for a in chunk_tables)
    const_specs = [_layer_spec(a, layer) if stacked else _const_spec(a) for a, stacked in consts]
    out_shape = (
        jax.ShapeDtypeStruct((bsz, t, D_MODEL), F32),
        jax.ShapeDtypeStruct((bsz, A_HEADS, A_DK, A_DV), F32),
        jax.ShapeDtypeStruct((bsz, B_HEADS, B_DK, B_DV), F32),
        jax.ShapeDtypeStruct((bsz, 1, S5_DIM), F32),
        jax.ShapeDtypeStruct((bsz, 1, S5_DIM), F32),
    )
    x_spec = pl.BlockSpec((STREAMS, LANES, D_MODEL), lambda b, i: (b, i, 0))

    def next_block(b, i):
        nxt = jnp.minimum(b * nt + i + 1, nb * nt - 1)
        return (nxt // nt, nxt % nt, 0)

    out_specs = (
        x_spec,
        pl.BlockSpec((STREAMS, A_HEADS, A_DK, A_DV), lambda b, i: (b, 0, 0, 0)),
        pl.BlockSpec((STREAMS, B_HEADS, B_DK, B_DV), lambda b, i: (b, 0, 0, 0)),
        pl.BlockSpec((STREAMS, 1, S5_DIM), lambda b, i: (b, 0, 0)),
        pl.BlockSpec((STREAMS, 1, S5_DIM), lambda b, i: (b, 0, 0)),
    )
    scratch = [
        pltpu.VMEM((2, STREAMS * LANES, IN_TOTAL), F32),
        pltpu.VMEM((STREAMS * LANES, D_MODEL), BF16),
        pltpu.VMEM((STREAMS, A_CM, LANES), F32),
        pltpu.VMEM((STREAMS, B_CM, LANES), F32),
        pltpu.VMEM((STREAMS, A_WIDTH, N_WIN * LANES), F32),
        pltpu.VMEM((STREAMS, B_QK, N_WIN * LANES), F32),
        pltpu.VMEM((STREAMS, A_WIDTH + B_WIDTH, LANES), F32),
        pltpu.VMEM((STREAMS, LANES, 2 * S5_DIM), F32),
        pltpu.VMEM((STREAMS, A_WIDTH, A_DV), F32),
        pltpu.VMEM((STREAMS, B_QK, B_DV), F32),
        pltpu.VMEM((STREAMS, SUBLANES, S5_DIM), F32),
        pltpu.VMEM((STREAMS, SUBLANES, S5_DIM), F32),
        pltpu.VMEM((3, A_WIDTH, LANES), F32),
    ]
    return pl.pallas_call(
        functools.partial(_prompt_kernel, layer, final),
        grid=(nb, nt),
        in_specs=[x_spec, pl.BlockSpec((STREAMS, LANES, D_MODEL), next_block)] + const_specs,
        out_specs=out_specs,
        out_shape=out_shape,
        scratch_shapes=scratch,
        compiler_params=pltpu.CompilerParams(
            dimension_semantics=("arbitrary", "arbitrary"), vmem_limit_bytes=VMEM_LIMIT),
    )(x, x, *(a for a, _ in consts))


def _sample_layers(xs, common, shg, sgla, sre, sim):
    n = xs.shape[0]
    depth = shg.shape[0]
    assert n == LANES
    hg_cols, gla_cols = A_DK * A_DV, B_DK * B_DV
    state_shapes = [s.shape for s in (shg, sgla, sre, sim)]
    out_shape = [jax.ShapeDtypeStruct((n, D_MODEL), F32)] + [jax.ShapeDtypeStruct(s, F32) for s in state_shapes]
    full = pl.BlockSpec((n, D_MODEL), lambda l, h: (0, 0))
    state_specs = [
        pl.BlockSpec((None, n, hg_cols), lambda l, h: (l, 0, h)),
        pl.BlockSpec((None, n, gla_cols), lambda l, h: (l, 0, h)),
        pl.BlockSpec((None, n, S5_DIM), lambda l, h: (l, 0, 0)),
        pl.BlockSpec((None, n, S5_DIM), lambda l, h: (l, 0, 0)),
    ]
    scratch = [
        pltpu.VMEM((n, D_MODEL), F32),
        pltpu.VMEM((n, IN_TOTAL), F32),
        pltpu.VMEM((A_CM, LANES), F32),
        pltpu.VMEM((B_CM, LANES), F32),
        pltpu.VMEM((A_WIDTH, LANES), F32),
        pltpu.VMEM((B_QK, LANES), F32),
        pltpu.VMEM((A_WIDTH + B_WIDTH, LANES), F32),
        pltpu.VMEM((hg_cols, LANES), F32),
        pltpu.VMEM((gla_cols, LANES), F32),
        pltpu.VMEM((3, A_WIDTH, LANES), F32),
    ]
    return pl.pallas_call(
        _sample_kernel,
        grid=(depth, A_HEADS),
        in_specs=([full] + [_layer_spec(a, None) if stacked else _const_spec(a) for a, stacked in common]
                  + state_specs),
        out_specs=[full] + state_specs,
        out_shape=out_shape,
        scratch_shapes=scratch,
        compiler_params=pltpu.CompilerParams(
            dimension_semantics=("arbitrary", "arbitrary"), vmem_limit_bytes=VMEM_LIMIT),
    )(xs, *(a for a, _ in common), shg, sgla, sre, sim)


def kernel(x_prompt, x_sample, state_hgrn, state_gla, state_s5_re, state_s5_im, norm_g, w_in, hg_lb_logits, hg_norm_g, gla_w2, gla_b2, gla_norm_g, s5_A_re, s5_A_im, s5_B_re, s5_B_im, s5_C_re, s5_C_im, s5_D, s5_log_dt, s5_w_glu, s5_b_glu, w_out, final_norm_g):
    depth = w_in.shape[0]
    bsz = x_prompt.shape[0]
    n = x_sample.shape[0]
    assert x_sample.shape[1] == 1 and A_HEADS == B_HEADS
    win_np, lv_np, perm_np = _chunk_tables()
    chunk_tables = (jnp.asarray(perm_np, BF16), jnp.asarray(win_np, BF16), jnp.asarray(lv_np))
    common, scan_tables = _operands(
        norm_g, w_in, hg_lb_logits, hg_norm_g, gla_w2, gla_b2, gla_norm_g,
        s5_A_re, s5_A_im, s5_B_re, s5_B_im, s5_C_re, s5_C_im, s5_D, s5_log_dt,
        s5_w_glu, s5_b_glu, w_out, final_norm_g)

    yp = x_prompt.astype(F32)
    prompt_states = []
    for l in range(depth):
        yp, *states = _prompt_layer(l, l == depth - 1, yp, common, scan_tables, chunk_tables)
        prompt_states.append(states)
    hg_p, gla_p, re_p, im_p = (jnp.stack([s[i] for s in prompt_states], axis=0) for i in range(4))

    flat = lambda s: s.astype(F32).reshape(depth, n, -1)
    ys, hg_s, gla_s, re_s, im_s = _sample_layers(
        x_sample.astype(F32).reshape(n, D_MODEL), common,
        flat(state_hgrn), flat(state_gla), flat(state_s5_re), flat(state_s5_im))

    return (yp, ys.reshape(n, 1, D_MODEL),
            hg_p, gla_p, re_p.reshape(depth, bsz, C_GROUPS, C_STATE), im_p.reshape(depth, bsz, C_GROUPS, C_STATE),
            hg_s.reshape(state_hgrn.shape), gla_s.reshape(state_gla.shape),
            re_s.reshape(state_s5_re.shape), im_s.reshape(state_s5_im.shape))
```

```python
import functools

import numpy as np
import jax
import jax.numpy as jnp
from jax import lax
from jax.experimental import pallas as pl
from jax.experimental.pallas import tpu as pltpu

F32 = jnp.float32
BF16 = jnp.bfloat16

D_MODEL = 1024
A_WIDTH, A_DK, A_DV, A_HEADS = 384, 64, 64, 6
B_WIDTH, B_DV, B_HEADS, B_DK, B_QK = 384, 64, 6, 32, 192
GLA_RANK = 16
GLA_TAU = 16.0
C_WIDTH, C_GROUP, C_GROUPS, C_STATE = 256, 16, 16, 64
S5_DIM = C_GROUPS * C_STATE
NORM_EPS = 1e-6
REF_BLOCKS = ("qa", "fa", "ia", "za", "qb", "kb", "vb", "zb", "rb", "uc", "zc")
REF_SIZES = (A_WIDTH, A_WIDTH, A_WIDTH, A_WIDTH, B_QK, B_QK, B_WIDTH, B_WIDTH, GLA_RANK, C_WIDTH, C_WIDTH)
KER_BLOCKS = ("qa", "fa", "ia", "za", "qb", "kb", "vb", "zb", "uc", "zc", "rb")
IN_TOTAL = sum(REF_SIZES)
_SIZE = dict(zip(REF_BLOCKS, REF_SIZES))
_REF_OFF = dict(zip(REF_BLOCKS, np.concatenate([[0], np.cumsum(REF_SIZES)[:-1]]).tolist()))
OFF = dict(zip(KER_BLOCKS, np.concatenate([[0], np.cumsum([_SIZE[b] for b in KER_BLOCKS])[:-1]]).tolist()))
A_CM = 3 * A_WIDTH
B_CM = 2 * B_QK + B_WIDTH

LANES = 128
SUBLANES = 8
N_LEVELS = 7
N_WIN = N_LEVELS + 1
SEG = LANES // SUBLANES
STREAMS = 2
PROJ_COLS = 256
PROJ_GAP = 6
VMEM_LIMIT = 56 * 1024 * 1024


def _chunk_tables():
    s = np.arange(LANES)[:, None]
    t = np.arange(LANES)[None, :]
    wins = []
    for l in range(N_LEVELS):
        h = (LANES // 2) >> l
        mid = (t // (2 * h)) * (2 * h) + h - 1
        role_q = (t & h) != 0
        wins.append(np.where(role_q, (s > mid) & (s <= t), (s > t) & (s <= mid)))
    wins.append(s <= t)
    win = np.concatenate(wins, axis=1).astype(np.float32)
    x = s ^ t
    lv = np.full((LANES, LANES), -1, np.int32)
    for l in range(N_LEVELS):
        h = (LANES // 2) >> l
        lv[(x >= h) & (x < 2 * h) & (t > s)] = l
    perm = np.zeros((LANES, LANES), np.float32)
    i, j = np.divmod(np.arange(LANES), SEG)
    perm[SUBLANES * j + i, np.arange(LANES)] = 1.0
    return np.concatenate([win, win], axis=0), np.tile(lv, (1, 2)), np.stack([perm, perm.T])


def _dot(a, b):
    return jnp.dot(a, b, preferred_element_type=F32)


def _dot_nt(a, b):
    return lax.dot_general(a, b, (((1,), (1,)), ((), ())), preferred_element_type=F32)


def _rmsnorm_rows(x, g):
    r = lax.rsqrt(jnp.mean(x * x, axis=-1, keepdims=True) + NORM_EPS)
    return x * r * g


def _lower_bound_consts(lbl_ref, layer, out_ref):
    rows = [lbl_ref[l] for l in range(lbl_ref.shape[0])]
    m = functools.reduce(jnp.maximum, rows)
    es = [jnp.exp(r - m) for r in rows]
    tot = functools.reduce(lambda a, b: a + b, es)
    sm = [e / tot for e in es]
    cum, lbs = sm[0], [sm[0] - sm[0]]
    for l in range(1, len(sm)):
        cum = cum + sm[l]
        lbs.append(cum - sm[0])
    if isinstance(layer, int):
        lb = lbs[layer]
    else:
        lb = lbs[0]
        for l in range(1, len(lbs)):
            lb = jnp.where(layer == l, lbs[l], lb)
    out_ref[0] = jnp.log(lb)
    out_ref[1] = jnp.log(1.0 - lb)
    out_ref[2] = 1.0 - lb


def _hgrn_gates(fa, lbc_ref):
    e = jnp.exp(-jnp.abs(fa))
    logsig = jnp.minimum(fa, 0.0) - jnp.log(1.0 + e)
    a = lbc_ref[0]
    c = lbc_ref[1] + logsig
    log_f = jnp.maximum(a, c) + jnp.log(1.0 + jnp.exp(-jnp.abs(a - c)))
    k_a = lbc_ref[2] * (jnp.where(fa >= 0.0, e, 1.0) / (1.0 + e))
    return log_f, k_a


def _gla_log_gate_cm(rb, w2_ref, b2_ref):
    pre = _dot(rb.astype(BF16), w2_ref[...]) + b2_ref[...]
    return (jax.nn.log_sigmoid(pre) / GLA_TAU).T[:B_QK]


def _window_sums(g, win_ref):
    hi = g.astype(BF16)
    lo = (g - hi.astype(F32)).astype(BF16)
    return _dot(jnp.concatenate([hi, lo], axis=1), win_ref[...])


def _gated_chunk(load_q, load_k, load_v, e_ref, s_ref, heads, dk, dv, lv_ref, lane, emit):
    for p in range(heads // 2):
        rows = slice(2 * p * dk, (2 * p + 2) * dk)
        q, k = load_q(rows), load_k(rows)
        att = [jnp.zeros((SUBLANES, 2 * LANES), F32)] * (LANES // SUBLANES)
        for l in range(N_LEVELS):
            h = (LANES // 2) >> l
            z = jnp.where((lane & h) != 0, q, k) * jnp.exp(e_ref[rows, l * LANES:(l + 1) * LANES])
            zb = z.astype(BF16)
            zero = jnp.zeros((dk, LANES), BF16)
            rhs = jnp.concatenate([jnp.concatenate([zb[:dk], zero], axis=1),
                                   jnp.concatenate([zero, zb[dk:]], axis=1)], axis=0)
            zt = z.T
            blocks = [b for b in range(LANES // SUBLANES) if h < SUBLANES or (b * SUBLANES) & h == 0]
            lhs = jnp.concatenate([zt[b * SUBLANES:(b + 1) * SUBLANES] for b in blocks], axis=0)
            gram = _dot(lhs.astype(BF16), rhs)
            for i, b in enumerate(blocks):
                sl = slice(b * SUBLANES, (b + 1) * SUBLANES)
                att[b] = jnp.where(lv_ref[sl, :] == l, gram[i * SUBLANES:(i + 1) * SUBLANES], att[b])
            yield
        attb = jnp.concatenate(att, axis=0).astype(BF16)
        bq = e_ref[rows, N_LEVELS * LANES:(N_LEVELS + 1) * LANES]
        b_last = bq[:, LANES - 1:LANES]
        qs = (q * jnp.exp(bq)).astype(BF16)
        kd = (k * jnp.exp(b_last - bq)).astype(BF16)
        dec = jnp.exp(b_last)
        qk = q * k
        for i in range(2):
            hd = 2 * p + i
            r = slice(i * dk, (i + 1) * dk)
            rs = slice(hd * dk, (hd + 1) * dk)
            v = load_v(slice(hd * dv, (hd + 1) * dv))
            vb = v.astype(BF16)
            s_old = s_ref[rs, :]
            o = _dot(s_old.T.astype(BF16), qs[r])
            o = o + _dot(vb, attb[:, i * LANES:(i + 1) * LANES])
            o = o + jnp.sum(qk[r], axis=0, keepdims=True) * v
            s_ref[rs, :] = dec[r] * s_old + _dot_nt(kd[r], vb)
            emit(hd, o)
            yield


def _head_rms(o):
    return o * lax.rsqrt(jnp.mean(o * o, axis=0, keepdims=True) + NORM_EPS)


def _cmul_add(ar, ai, xr, xi, br, bi):
    return br + ar * xr - ai * xi, bi + ar * xi + ai * xr


def _s5_readout(x_re, x_im, cb_ref):
    return _dot(jnp.concatenate([x_re, x_im], axis=1).astype(BF16), cb_ref[...])


def _s5_gate_tokens(y, u, zc, d_ref, wglu_ref, bglu_ref):
    y = y + d_ref[...] * u
    y = jax.nn.gelu(y)
    y = y * jax.nn.sigmoid(_dot(y.astype(BF16), wglu_ref[...]) + bglu_ref[...])
    return y * jax.nn.silu(zc)


def _project(x, ng_ref, win_ref, p_ref):
    h = _rmsnorm_rows(x, ng_ref[...])
    p_ref[...] = _dot(h.astype(BF16), win_ref[...])


def _project_steps(x, ng_ref, win_ref, hb_ref, p_ref):
    hb_ref[...] = _rmsnorm_rows(x, ng_ref[...]).astype(BF16)
    for c0 in range(0, IN_TOTAL, PROJ_COLS):
        for _ in range(PROJ_GAP + 1):
            yield
        cols = slice(c0, min(c0 + PROJ_COLS, IN_TOTAL))
        p_ref[:, cols] = _dot(hb_ref[...], win_ref[:, cols])


def _mix_and_project(x, mixt_ref, za, zb, oc, hgain_ref, ggain_ref, wout_ref, fg_ref, final):
    o_t = mixt_ref[...].T
    o_a = o_t[:, :A_WIDTH] * hgain_ref[...] * jax.nn.silu(za)
    o_b = o_t[:, A_WIDTH:] * ggain_ref[...] * jax.nn.silu(zb)
    mix = jnp.concatenate([o_a, o_b, oc], axis=1).astype(BF16)
    out = x + _dot(mix, wout_ref[...])
    if final:
        out = _rmsnorm_rows(out, fg_ref[...])
    return out


def _cols(name):
    return slice(OFF[name], OFF[name] + _SIZE[name])


def _interleave(gens):
    live = list(gens)
    while live:
        live = [g for g in live if next(g, StopIteration) is not StopIteration]


def _prompt_kernel(layer, final,
                   x_ref, xn_ref, ng_ref, win_ref, lbl_ref, hgain_ref, w2_ref, b2_ref, ggain_ref,
                   bd_ref, cb_ref, dd_ref, wglu_ref, bglu_ref, wout_ref, fg_ref, aseg_ref,
                   tsub_ref, tcar_ref, tpos_ref, perm_ref, wwin_ref, lv_ref,
                   y_ref, shg_ref, sgla_ref, sre_ref, sim_ref,
                   p_ref, hb_ref, cma_ref, cmb_ref, ea_ref, eb_ref, mixt_ref, bu_ref,
                   hg_s, gla_s, cre_s, cim_s, lbc_s):
    t_idx = pl.program_id(1)
    step = pl.program_id(0) * pl.num_programs(1) + t_idx
    cur = step % 2

    @pl.when(t_idx == 0)
    def _():
        hg_s[...] = jnp.zeros_like(hg_s)
        gla_s[...] = jnp.zeros_like(gla_s)
        cre_s[...] = jnp.zeros_like(cre_s)
        cim_s[...] = jnp.zeros_like(cim_s)
        _lower_bound_consts(lbl_ref, layer, lbc_s)

    @pl.when(step == 0)
    def _():
        _project(x_ref[...].reshape(STREAMS * LANES, D_MODEL), ng_ref, win_ref, p_ref.at[0])

    lane = lax.broadcasted_iota(jnp.int32, (1, LANES), 1)

    def chunk_steps(k):
        p_cur = p_ref.at[cur, k * LANES:(k + 1) * LANES]
        cma, cmb, ea, eb, mixt, bu = (r.at[k] for r in (cma_ref, cmb_ref, ea_ref, eb_ref, mixt_ref, bu_ref))
        hg, gla, cre, cim = (r.at[k] for r in (hg_s, gla_s, cre_s, cim_s))
        cma[...] = p_cur[:, :A_CM].T
        cmb[...] = p_cur[:, OFF["qb"]:OFF["qb"] + B_CM].T
        yield

        log_f, k_a = _hgrn_gates(cma[A_WIDTH:2 * A_WIDTH, :], lbc_s)
        cma[A_WIDTH:2 * A_WIDTH, :] = k_a
        ea[...] = _window_sums(log_f, wwin_ref)
        yield

        def emit_a(hd, o):
            mixt[hd * A_DV:(hd + 1) * A_DV, :] = _head_rms(o)

        yield from _gated_chunk(lambda r: jax.nn.silu(cma[r, :]),
                                lambda r: cma[A_WIDTH + r.start:A_WIDTH + r.stop, :],
                                lambda r: cma[2 * A_WIDTH + r.start:2 * A_WIDTH + r.stop, :],
                                ea, hg, A_HEADS, A_DK, A_DV, lv_ref, lane, emit_a)

        eb[...] = _window_sums(_gla_log_gate_cm(p_cur[:, _cols("rb")], w2_ref, b2_ref), wwin_ref)
        yield

        def emit_b(hd, o):
            mixt[A_WIDTH + hd * B_DV:A_WIDTH + (hd + 1) * B_DV, :] = _head_rms(o)

        yield from _gated_chunk(lambda r: cmb[r, :] * (B_DK ** -0.5),
                                lambda r: cmb[B_QK + r.start:B_QK + r.stop, :],
                                lambda r: cmb[2 * B_QK + r.start:2 * B_QK + r.stop, :],
                                eb, gla, B_HEADS, B_DK, B_DV, lv_ref, lane, emit_b)

        u = p_cur[:, _cols("uc")]
        u_seg = _dot(perm_ref[0], u.astype(BF16)).astype(BF16)
        bu[...] = _dot(u_seg, bd_ref[...])
        yield
        ar, ai = aseg_ref[0], aseg_ref[1]
        xr = jnp.zeros((SUBLANES, S5_DIM), F32)
        xi = jnp.zeros((SUBLANES, S5_DIM), F32)
        for j in range(SEG):
            rows = slice(j * SUBLANES, (j + 1) * SUBLANES)
            xr, xi = _cmul_add(ar, ai, xr, xi, bu[rows, :S5_DIM], bu[rows, S5_DIM:])
            bu[rows, :S5_DIM] = xr
            bu[rows, S5_DIM:] = xi
            yield
        for i in range(3):
            xr, xi = _cmul_add(tsub_ref[i, 0], tsub_ref[i, 1],
                               pltpu.roll(xr, 1 << i, 0), pltpu.roll(xi, 1 << i, 0), xr, xi)
        cr, ci = cre[...], cim[...]
        xr, xi = _cmul_add(tcar_ref[0], tcar_ref[1], cr, ci, xr, xi)
        first = lax.broadcasted_iota(jnp.int32, (SUBLANES, S5_DIM), 0) == 0
        pr = jnp.where(first, cr, pltpu.roll(xr, 1, 0))
        pi = jnp.where(first, ci, pltpu.roll(xi, 1, 0))
        cre[...] = jnp.broadcast_to(xr[SUBLANES - 1:], (SUBLANES, S5_DIM))
        cim[...] = jnp.broadcast_to(xi[SUBLANES - 1:], (SUBLANES, S5_DIM))
        yield
        for j in range(SEG):
            rows = slice(j * SUBLANES, (j + 1) * SUBLANES)
            tr = jnp.broadcast_to(tpos_ref[0, j:j + 1, :], (SUBLANES, S5_DIM))
            ti = jnp.broadcast_to(tpos_ref[1, j:j + 1, :], (SUBLANES, S5_DIM))
            fr, fi = _cmul_add(tr, ti, pr, pi, bu[rows, :S5_DIM], bu[rows, S5_DIM:])
            bu[rows, :S5_DIM] = fr
            bu[rows, S5_DIM:] = fi
            yield
        y_seg = _dot(bu[...].astype(BF16), cb_ref[...])
        y1 = y_seg.astype(BF16)
        r1 = y_seg - y1.astype(F32)
        y2 = r1.astype(BF16)
        y3 = (r1 - y2.astype(F32)).astype(BF16)
        yield
        yp = _dot(perm_ref[1], jnp.concatenate([y1, y2, y3], axis=1))
        y_tok = yp[:, :C_WIDTH] + yp[:, C_WIDTH:2 * C_WIDTH] + yp[:, 2 * C_WIDTH:]
        yield
        oc = _s5_gate_tokens(y_tok, u, p_cur[:, _cols("zc")], dd_ref, wglu_ref, bglu_ref)
        yield
        y_ref[k] = _mix_and_project(
            x_ref[k], mixt, p_cur[:, _cols("za")], p_cur[:, _cols("zb")], oc,
            hgain_ref, ggain_ref, wout_ref, fg_ref, final)

    project_next = _project_steps(xn_ref[...].reshape(STREAMS * LANES, D_MODEL), ng_ref, win_ref,
                                  hb_ref, p_ref.at[1 - cur])
    _interleave([chunk_steps(k) for k in range(STREAMS)] + [project_next])

    @pl.when(t_idx == pl.num_programs(1) - 1)
    def _():
        for k in range(STREAMS):
            for h in range(A_HEADS):
                shg_ref[k, h] = hg_s[k, h * A_DK:(h + 1) * A_DK, :]
            for h in range(B_HEADS):
                sgla_ref[k, h] = gla_s[k, h * B_DK:(h + 1) * B_DK, :]
            sre_ref[k] = cre_s[k, :1, :]
            sim_ref[k] = cim_s[k, :1, :]


def _sample_kernel(x_ref, ng_ref, win_ref, lbl_ref, hgain_ref, w2_ref, b2_ref, ggain_ref,
                   bd_ref, cb_ref, dd_ref, wglu_ref, bglu_ref, wout_ref, fg_ref, aseg_ref,
                   shg_in, sgla_in, sre_in, sim_in,
                   y_ref, shg_out, sgla_out, sre_out, sim_out,
                   xs_s, p_ref, cma_ref, cmb_ref, fa_s, gb_s, mixt_ref, sa_s, sb_s, lbc_s):
    l = pl.program_id(0)
    h = pl.program_id(1)

    @pl.when((l == 0) & (h == 0))
    def _():
        xs_s[...] = x_ref[...]

    @pl.when(h == 0)
    def _():
        _lower_bound_consts(lbl_ref, l, lbc_s)
        _project(xs_s[...], ng_ref, win_ref, p_ref)
        cma_ref[...] = p_ref[:, :A_CM].T
        cmb_ref[...] = p_ref[:, OFF["qb"]:OFF["qb"] + B_CM].T
        log_f, k_a = _hgrn_gates(cma_ref[A_WIDTH:2 * A_WIDTH, :], lbc_s)
        fa_s[...] = jnp.exp(log_f)
        cma_ref[A_WIDTH:2 * A_WIDTH, :] = k_a
        cma_ref[:A_WIDTH, :] = jax.nn.silu(cma_ref[:A_WIDTH, :])
        gb_s[...] = jnp.exp(_gla_log_gate_cm(p_ref[:, _cols("rb")], w2_ref, b2_ref))

    sa_s[...] = shg_in[...].T
    ra = pl.multiple_of(h * A_DK, A_DK)
    v_a = cma_ref[pl.ds(2 * A_WIDTH + ra, A_DV), :]

    def hg_body(k, o):
        r = ra + k
        rows = pl.ds(pl.multiple_of(k * A_DV, A_DV), A_DV)
        new = fa_s[pl.ds(r, 1), :] * sa_s[rows, :] + cma_ref[pl.ds(A_WIDTH + r, 1), :] * v_a
        sa_s[rows, :] = new
        return o + cma_ref[pl.ds(r, 1), :] * new

    o = lax.fori_loop(0, A_DK, hg_body, jnp.zeros((A_DV, LANES), F32))
    mixt_ref[pl.ds(ra, A_DV), :] = _head_rms(o)
    shg_out[...] = sa_s[...].T

    sb_s[...] = sgla_in[...].T
    rb = pl.multiple_of(h * B_DK, B_DK)
    vb0 = pl.multiple_of(h * B_DV, B_DV)
    v_b = cmb_ref[pl.ds(2 * B_QK + vb0, B_DV), :]

    def gla_body(k, o):
        r = rb + k
        rows = pl.ds(pl.multiple_of(k * B_DV, B_DV), B_DV)
        new = gb_s[pl.ds(r, 1), :] * sb_s[rows, :] + cmb_ref[pl.ds(B_QK + r, 1), :] * v_b
        sb_s[rows, :] = new
        return o + (cmb_ref[pl.ds(r, 1), :] * (B_DK ** -0.5)) * new

    o = lax.fori_loop(0, B_DK, gla_body, jnp.zeros((B_DV, LANES), F32))
    mixt_ref[pl.ds(A_WIDTH + vb0, B_DV), :] = _head_rms(o)
    sgla_out[...] = sb_s[...].T

    @pl.when(h == pl.num_programs(1) - 1)
    def _():
        u = p_ref[:, _cols("uc")]
        bu = _dot(u.astype(BF16), bd_ref[...])
        x_re, x_im = _cmul_add(aseg_ref[0, :1], aseg_ref[1, :1], sre_in[...], sim_in[...],
                               bu[:, :S5_DIM], bu[:, S5_DIM:])
        sre_out[...] = x_re
        sim_out[...] = x_im
        oc = _s5_gate_tokens(_s5_readout(x_re, x_im, cb_ref), u, p_ref[:, _cols("zc")],
                             dd_ref, wglu_ref, bglu_ref)
        out = _mix_and_project(
            xs_s[...], mixt_ref, p_ref[:, _cols("za")], p_ref[:, _cols("zb")], oc,
            hgain_ref, ggain_ref, wout_ref, fg_ref, False)
        xs_s[...] = out

        @pl.when(l == pl.num_programs(0) - 1)
        def _():
            y_ref[...] = _rmsnorm_rows(out, fg_ref[...])


def _lane_bcast(v):
    return jnp.broadcast_to(v.astype(F32)[..., None], v.shape + (LANES,))


def _row(v):
    return v.astype(F32).reshape(1, -1)


def _powers(ar, ai, n):
    pr, pi = ar[..., None], ai[..., None]
    while pr.shape[-1] < n:
        lr, li = pr[..., -1:], pi[..., -1:]
        pr, pi = (jnp.concatenate([pr, pr * lr - pi * li], axis=-1),
                  jnp.concatenate([pi, pr * li + pi * lr], axis=-1))
    return pr[..., :n], pi[..., :n]


def _group_block_diag(m):
    g, r, c = m.shape
    eye = jnp.eye(g, dtype=m.dtype)
    return (m[:, :, None, :] * eye[:, None, :, None]).reshape(g * r, g * c)


def _s5_tables(A_re, A_im, B_re, B_im, C_re, C_im, log_dt):
    dt = jnp.exp(log_dt.astype(F32))[:, None]
    lr, li = A_re.astype(F32), A_im.astype(F32)
    mag = jnp.exp(lr * dt)
    ab_re = mag * jnp.cos(li * dt)
    ab_im = mag * jnp.sin(li * dt)
    den = lr * lr + li * li
    nr = ab_re - 1.0
    co_re = (nr * lr + ab_im * li) / den
    co_im = (ab_im * lr - nr * li) / den
    Br, Bi = B_re.astype(F32), B_im.astype(F32)
    bb_re = co_re[..., None] * Br - co_im[..., None] * Bi
    bb_im = co_re[..., None] * Bi + co_im[..., None] * Br
    bd = jnp.concatenate([_group_block_diag(bb_re.transpose(0, 2, 1)),
                          _group_block_diag(bb_im.transpose(0, 2, 1))], axis=1).astype(BF16)
    cb = jnp.concatenate([_group_block_diag(C_re.astype(F32).transpose(0, 2, 1)),
                          _group_block_diag(-C_im.astype(F32).transpose(0, 2, 1))], axis=0).astype(BF16)
    pr, pi = _powers(ab_re.reshape(-1), ab_im.reshape(-1), LANES)
    pw = jnp.stack([pr.T, pi.T])
    aseg = jnp.broadcast_to(pw[:, :1], (2, SUBLANES, S5_DIM))
    sub = jnp.arange(SUBLANES)[None, :, None]
    tsub = jnp.stack([jnp.where(sub >= (1 << i), pw[:, SEG * (1 << i) - 1][:, None, :], 0.0) for i in range(3)])
    tcar = pw[:, SEG - 1::SEG]
    tpos = pw[:, :SEG]
    return bd, cb, aseg, tsub, tcar, tpos


def _layer_spec(a, layer):
    nd = a.ndim - 1
    if layer is None:
        return pl.BlockSpec((None,) + a.shape[1:], lambda l, *_, nd=nd: (l,) + (0,) * nd)
    return pl.BlockSpec((None,) + a.shape[1:], lambda *_, nd=nd: (layer,) + (0,) * nd)


def _const_spec(a):
    nd = a.ndim
    return pl.BlockSpec(a.shape, lambda *_, nd=nd: (0,) * nd)


def _operands(norm_g, w_in, hg_lb_logits, hg_norm_g, gla_w2, gla_b2, gla_norm_g,
              s5_A_re, s5_A_im, s5_B_re, s5_B_im, s5_C_re, s5_C_im, s5_D, s5_log_dt,
              s5_w_glu, s5_b_glu, w_out, final_norm_g):
    bd, cb, aseg, tsub, tcar, tpos = jax.vmap(_s5_tables)(
        s5_A_re, s5_A_im, s5_B_re, s5_B_im, s5_C_re, s5_C_im, s5_log_dt)
    w_ker = jnp.concatenate([w_in[..., _REF_OFF[b]:_REF_OFF[b] + _SIZE[b]] for b in KER_BLOCKS], axis=-1)
    pad = 2 * LANES - B_QK
    rows = lambda v: v.astype(F32)[:, None, :]
    common = (
        (rows(norm_g), True),
        (w_ker.astype(BF16), True),
        (_lane_bcast(hg_lb_logits), False),
        (rows(hg_norm_g), True),
        (jnp.pad(gla_w2, ((0, 0), (0, 0), (0, pad))).astype(BF16), True),
        (jnp.pad(rows(gla_b2), ((0, 0), (0, 0), (0, pad))), True),
        (rows(gla_norm_g), True),
        (bd, True), (cb, True),
        (rows(s5_D), True),
        (s5_w_glu.astype(BF16), True),
        (rows(s5_b_glu), True),
        (w_out.astype(BF16), True),
        (_row(final_norm_g), False),
        (aseg, True),
    )
    return common, ((tsub, True), (tcar, True), (tpos, True))


def _prompt_layer(layer, final, x, common, scan_tables, chunk_tables):
    bsz, t, _ = x.shape
    assert t % LANES == 0 and bsz % STREAMS == 0
    nb, nt = bsz // STREAMS, t // LANES
    consts = common + scan_tables + tuple((a, False) for a in chunk_tables)
    const_specs = [_layer_spec(a, layer) if stacked else _const_spec(a) for a, stacked in consts]
    out_shape = (
        jax.ShapeDtypeStruct((bsz, t, D_MODEL), F32),
        jax.ShapeDtypeStruct((bsz, A_HEADS, A_DK, A_DV), F32),
        jax.ShapeDtypeStruct((bsz, B_HEADS, B_DK, B_DV), F32),
        jax.ShapeDtypeStruct((bsz, 1, S5_DIM), F32),
        jax.ShapeDtypeStruct((bsz, 1, S5_DIM), F32),
    )
    x_spec = pl.BlockSpec((STREAMS, LANES, D_MODEL), lambda b, i: (b, i, 0))

    def next_block(b, i):
        nxt = jnp.minimum(b * nt + i + 1, nb * nt - 1)
        return (nxt // nt, nxt % nt, 0)

    out_specs = (
        x_spec,
        pl.BlockSpec((STREAMS, A_HEADS, A_DK, A_DV), lambda b, i: (b, 0, 0, 0)),
        pl.BlockSpec((STREAMS, B_HEADS, B_DK, B_DV), lambda b, i: (b, 0, 0, 0)),
        pl.BlockSpec((STREAMS, 1, S5_DIM), lambda b, i: (b, 0, 0)),
        pl.BlockSpec((STREAMS, 1, S5_DIM), lambda b, i: (b, 0, 0)),
    )
    scratch = [
        pltpu.VMEM((2, STREAMS * LANES, IN_TOTAL), F32),
        pltpu.VMEM((STREAMS * LANES, D_MODEL), BF16),
        pltpu.VMEM((STREAMS, A_CM, LANES), F32),
        pltpu.VMEM((STREAMS, B_CM, LANES), F32),
        pltpu.VMEM((STREAMS, A_WIDTH, N_WIN * LANES), F32),
        pltpu.VMEM((STREAMS, B_QK, N_WIN * LANES), F32),
        pltpu.VMEM((STREAMS, A_WIDTH + B_WIDTH, LANES), F32),
        pltpu.VMEM((STREAMS, LANES, 2 * S5_DIM), F32),
        pltpu.VMEM((STREAMS, A_WIDTH, A_DV), F32),
        pltpu.VMEM((STREAMS, B_QK, B_DV), F32),
        pltpu.VMEM((STREAMS, SUBLANES, S5_DIM), F32),
        pltpu.VMEM((STREAMS, SUBLANES, S5_DIM), F32),
        pltpu.VMEM((3, A_WIDTH, LANES), F32),
    ]
    return pl.pallas_call(
        functools.partial(_prompt_kernel, layer, final),
        grid=(nb, nt),
        in_specs=[x_spec, pl.BlockSpec((STREAMS, LANES, D_MODEL), next_block)] + const_specs,
        out_specs=out_specs,
        out_shape=out_shape,
        scratch_shapes=scratch,
        compiler_params=pltpu.CompilerParams(
            dimension_semantics=("arbitrary", "arbitrary"), vmem_limit_bytes=VMEM_LIMIT),
    )(x, x, *(a for a, _ in consts))


def _sample_layers(xs, common, shg, sgla, sre, sim):
    n = xs.shape[0]
    depth = shg.shape[0]
    assert n == LANES
    hg_cols, gla_cols = A_DK * A_DV, B_DK * B_DV
    state_shapes = [s.shape for s in (shg, sgla, sre, sim)]
    out_shape = [jax.ShapeDtypeStruct((n, D_MODEL), F32)] + [jax.ShapeDtypeStruct(s, F32) for s in state_shapes]
    full = pl.BlockSpec((n, D_MODEL), lambda l, h: (0, 0))
    state_specs = [
        pl.BlockSpec((None, n, hg_cols), lambda l, h: (l, 0, h)),
        pl.BlockSpec((None, n, gla_cols), lambda l, h: (l, 0, h)),
        pl.BlockSpec((None, n, S5_DIM), lambda l, h: (l, 0, 0)),
        pl.BlockSpec((None, n, S5_DIM), lambda l, h: (l, 0, 0)),
    ]
    scratch = [
        pltpu.VMEM((n, D_MODEL), F32),
        pltpu.VMEM((n, IN_TOTAL), F32),
        pltpu.VMEM((A_CM, LANES), F32),
        pltpu.VMEM((B_CM, LANES), F32),
        pltpu.VMEM((A_WIDTH, LANES), F32),
        pltpu.VMEM((B_QK, LANES), F32),
        pltpu.VMEM((A_WIDTH + B_WIDTH, LANES), F32),
        pltpu.VMEM((hg_cols, LANES), F32),
        pltpu.VMEM((gla_cols, LANES), F32),
        pltpu.VMEM((3, A_WIDTH, LANES), F32),
    ]
    return pl.pallas_call(
        _sample_kernel,
        grid=(depth, A_HEADS),
        in_specs=([full] + [_layer_spec(a, None) if stacked else _const_spec(a) for a, stacked in common]
                  + state_specs),
        out_specs=[full] + state_specs,
        out_shape=out_shape,
        scratch_shapes=scratch,
        compiler_params=pltpu.CompilerParams(
            dimension_semantics=("arbitrary", "arbitrary"), vmem_limit_bytes=VMEM_LIMIT),
    )(xs, *(a for a, _ in common), shg, sgla, sre, sim)


def kernel(x_prompt, x_sample, state_hgrn, state_gla, state_s5_re, state_s5_im, norm_g, w_in, hg_lb_logits, hg_norm_g, gla_w2, gla_b2, gla_norm_g, s5_A_re, s5_A_im, s5_B_re, s5_B_im, s5_C_re, s5_C_im, s5_D, s5_log_dt, s5_w_glu, s5_b_glu, w_out, final_norm_g):
    depth = w_in.shape[0]
    bsz = x_prompt.shape[0]
    n = x_sample.shape[0]
    assert x_sample.shape[1] == 1 and A_HEADS == B_HEADS
    win_np, lv_np, perm_np = _chunk_tables()
    chunk_tables = (jnp.asarray(perm_np, BF16), jnp.asarray(win_np, BF16), jnp.asarray(lv_np))
    common, scan_tables = _operands(
        norm_g, w_in, hg_lb_logits, hg_norm_g, gla_w2, gla_b2, gla_norm_g,
        s5_A_re, s5_A_im, s5_B_re, s5_B_im, s5_C_re, s5_C_im, s5_D, s5_log_dt,
        s5_w_glu, s5_b_glu, w_out, final_norm_g)

    yp = x_prompt.astype(F32)
    prompt_states = []
    for l in range(depth):
        yp, *states = _prompt_layer(l, l == depth - 1, yp, common, scan_tables, chunk_tables)
        prompt_states.append(states)
    hg_p, gla_p, re_p, im_p = (jnp.stack([s[i] for s in prompt_states], axis=0) for i in range(4))

    flat = lambda s: s.astype(F32).reshape(depth, n, -1)
    ys, hg_s, gla_s, re_s, im_s = _sample_layers(
        x_sample.astype(F32).reshape(n, D_MODEL), common,
        flat(state_hgrn), flat(state_gla), flat(state_s5_re), flat(state_s5_im))

    return (yp, ys.reshape(n, 1, D_MODEL),
            hg_p, gla_p, re_p.reshape(depth, bsz, C_GROUPS, C_STATE), im_p.reshape(depth, bsz, C_GROUPS, C_STATE),
            hg_s.reshape(state_hgrn.shape), gla_s.reshape(state_gla.shape),
            re_s.reshape(state_s5_re.shape), im_s.reshape(state_s5_im.shape))
```

```python
import functools

import numpy as np
import jax
import jax.numpy as jnp
from jax import lax
from jax.experimental import pallas as pl
from jax.experimental.pallas import tpu as pltpu

F32 = jnp.float32
BF16 = jnp.bfloat16

D_MODEL = 1024
A_WIDTH, A_DK, A_DV, A_HEADS = 384, 64, 64, 6
B_WIDTH, B_DV, B_HEADS, B_DK, B_QK = 384, 64, 6, 32, 192
GLA_RANK = 16
GLA_TAU = 16.0
C_WIDTH, C_GROUP, C_GROUPS, C_STATE = 256, 16, 16, 64
S5_DIM = C_GROUPS * C_STATE
NORM_EPS = 1e-6
REF_BLOCKS = ("qa", "fa", "ia", "za", "qb", "kb", "vb", "zb", "rb", "uc", "zc")
REF_SIZES = (A_WIDTH, A_WIDTH, A_WIDTH, A_WIDTH, B_QK, B_QK, B_WIDTH, B_WIDTH, GLA_RANK, C_WIDTH, C_WIDTH)
KER_BLOCKS = ("qa", "fa", "ia", "za", "qb", "kb", "vb", "zb", "uc", "zc", "rb")
IN_TOTAL = sum(REF_SIZES)
_SIZE = dict(zip(REF_BLOCKS, REF_SIZES))
_REF_OFF = dict(zip(REF_BLOCKS, np.concatenate([[0], np.cumsum(REF_SIZES)[:-1]]).tolist()))
OFF = dict(zip(KER_BLOCKS, np.concatenate([[0], np.cumsum([_SIZE[b] for b in KER_BLOCKS])[:-1]]).tolist()))
A_CM = 3 * A_WIDTH
B_CM = 2 * B_QK + B_WIDTH

LANES = 128
SUBLANES = 8
N_LEVELS = 7
N_WIN = N_LEVELS + 1
SEG = LANES // SUBLANES
STREAMS = 4
PROJ_COLS = 256
PROJ_GAP = 6
VMEM_LIMIT = 56 * 1024 * 1024


def _chunk_tables():
    s = np.arange(LANES)[:, None]
    t = np.arange(LANES)[None, :]
    wins = []
    for l in range(N_LEVELS):
        h = (LANES // 2) >> l
        mid = (t // (2 * h)) * (2 * h) + h - 1
        role_q = (t & h) != 0
        wins.append(np.where(role_q, (s > mid) & (s <= t), (s > t) & (s <= mid)))
    wins.append(s <= t)
    win = np.concatenate(wins, axis=1).astype(np.float32)
    x = s ^ t
    lv = np.full((LANES, LANES), -1, np.int32)
    for l in range(N_LEVELS):
        h = (LANES // 2) >> l
        lv[(x >= h) & (x < 2 * h) & (t > s)] = l
    perm = np.zeros((LANES, LANES), np.float32)
    i, j = np.divmod(np.arange(LANES), SEG)
    perm[SUBLANES * j + i, np.arange(LANES)] = 1.0
    return np.concatenate([win, win], axis=0), np.tile(lv, (1, 2)), np.stack([perm, perm.T])


def _dot(a, b):
    return jnp.dot(a, b, preferred_element_type=F32)


def _dot_nt(a, b):
    return lax.dot_general(a, b, (((1,), (1,)), ((), ())), preferred_element_type=F32)


def _rmsnorm_rows(x, g):
    r = lax.rsqrt(jnp.mean(x * x, axis=-1, keepdims=True) + NORM_EPS)
    return x * r * g


def _lower_bound_consts(lbl_ref, layer, out_ref):
    rows = [lbl_ref[l] for l in range(lbl_ref.shape[0])]
    m = functools.reduce(jnp.maximum, rows)
    es = [jnp.exp(r - m) for r in rows]
    tot = functools.reduce(lambda a, b: a + b, es)
    sm = [e / tot for e in es]
    cum, lbs = sm[0], [sm[0] - sm[0]]
    for l in range(1, len(sm)):
        cum = cum + sm[l]
        lbs.append(cum - sm[0])
    if isinstance(layer, int):
        lb = lbs[layer]
    else:
        lb = lbs[0]
        for l in range(1, len(lbs)):
            lb = jnp.where(layer == l, lbs[l], lb)
    out_ref[0] = jnp.log(lb)
    out_ref[1] = jnp.log(1.0 - lb)
    out_ref[2] = 1.0 - lb


def _hgrn_gates(fa, lbc_ref):
    e = jnp.exp(-jnp.abs(fa))
    logsig = jnp.minimum(fa, 0.0) - jnp.log(1.0 + e)
    a = lbc_ref[0]
    c = lbc_ref[1] + logsig
    log_f = jnp.maximum(a, c) + jnp.log(1.0 + jnp.exp(-jnp.abs(a - c)))
    k_a = lbc_ref[2] * (jnp.where(fa >= 0.0, e, 1.0) / (1.0 + e))
    return log_f, k_a


def _gla_log_gate_cm(rb, w2_ref, b2_ref):
    pre = _dot(rb.astype(BF16), w2_ref[...]) + b2_ref[...]
    return (jax.nn.log_sigmoid(pre) / GLA_TAU).T[:B_QK]


def _window_sums(g, win_ref):
    hi = g.astype(BF16)
    lo = (g - hi.astype(F32)).astype(BF16)
    return _dot(jnp.concatenate([hi, lo], axis=1), win_ref[...])


def _gated_chunk(load_q, load_k, load_v, e_ref, s_ref, heads, dk, dv, lv_ref, lane, emit):
    for p in range(heads // 2):
        rows = slice(2 * p * dk, (2 * p + 2) * dk)
        q, k = load_q(rows), load_k(rows)
        att = [jnp.zeros((SUBLANES, 2 * LANES), F32)] * (LANES // SUBLANES)
        for l in range(N_LEVELS):
            h = (LANES // 2) >> l
            z = jnp.where((lane & h) != 0, q, k) * jnp.exp(e_ref[rows, l * LANES:(l + 1) * LANES])
            zb = z.astype(BF16)
            zero = jnp.zeros((dk, LANES), BF16)
            rhs = jnp.concatenate([jnp.concatenate([zb[:dk], zero], axis=1),
                                   jnp.concatenate([zero, zb[dk:]], axis=1)], axis=0)
            zt = z.T
            blocks = [b for b in range(LANES // SUBLANES) if h < SUBLANES or (b * SUBLANES) & h == 0]
            lhs = jnp.concatenate([zt[b * SUBLANES:(b + 1) * SUBLANES] for b in blocks], axis=0)
            gram = _dot(lhs.astype(BF16), rhs)
            for i, b in enumerate(blocks):
                sl = slice(b * SUBLANES, (b + 1) * SUBLANES)
                att[b] = jnp.where(lv_ref[sl, :] == l, gram[i * SUBLANES:(i + 1) * SUBLANES], att[b])
            yield
        attb = jnp.concatenate(att, axis=0).astype(BF16)
        bq = e_ref[rows, N_LEVELS * LANES:(N_LEVELS + 1) * LANES]
        b_last = bq[:, LANES - 1:LANES]
        qs = (q * jnp.exp(bq)).astype(BF16)
        kd = (k * jnp.exp(b_last - bq)).astype(BF16)
        dec = jnp.exp(b_last)
        qk = q * k
        for i in range(2):
            hd = 2 * p + i
            r = slice(i * dk, (i + 1) * dk)
            rs = slice(hd * dk, (hd + 1) * dk)
            v = load_v(slice(hd * dv, (hd + 1) * dv))
            vb = v.astype(BF16)
            s_old = s_ref[rs, :]
            o = _dot(s_old.T.astype(BF16), qs[r])
            o = o + _dot(vb, attb[:, i * LANES:(i + 1) * LANES])
            o = o + jnp.sum(qk[r], axis=0, keepdims=True) * v
            s_ref[rs, :] = dec[r] * s_old + _dot_nt(kd[r], vb)
            emit(hd, o)
            yield


def _head_rms(o):
    return o * lax.rsqrt(jnp.mean(o * o, axis=0, keepdims=True) + NORM_EPS)


def _cmul_add(ar, ai, xr, xi, br, bi):
    return br + ar * xr - ai * xi, bi + ar * xi + ai * xr


def _s5_readout(x_re, x_im, cb_ref):
    return _dot(jnp.concatenate([x_re, x_im], axis=1).astype(BF16), cb_ref[...])


def _s5_gate_tokens(y, u, zc, d_ref, wglu_ref, bglu_ref):
    y = y + d_ref[...] * u
    y = jax.nn.gelu(y)
    y = y * jax.nn.sigmoid(_dot(y.astype(BF16), wglu_ref[...]) + bglu_ref[...])
    return y * jax.nn.silu(zc)


def _project(x, ng_ref, win_ref, p_ref):
    h = _rmsnorm_rows(x, ng_ref[...])
    p_ref[...] = _dot(h.astype(BF16), win_ref[...])


def _project_steps(x, ng_ref, win_ref, hb_ref, p_ref):
    hb_ref[...] = _rmsnorm_rows(x, ng_ref[...]).astype(BF16)
    for c0 in range(0, IN_TOTAL, PROJ_COLS):
        for _ in range(PROJ_GAP + 1):
            yield
        cols = slice(c0, min(c0 + PROJ_COLS, IN_TOTAL))
        p_ref[:, cols] = _dot(hb_ref[...], win_ref[:, cols])


def _mix_and_project(x, mixt_ref, za, zb, oc, hgain_ref, ggain_ref, wout_ref, fg_ref, final):
    o_t = mixt_ref[...].T
    o_a = o_t[:, :A_WIDTH] * hgain_ref[...] * jax.nn.silu(za)
    o_b = o_t[:, A_WIDTH:] * ggain_ref[...] * jax.nn.silu(zb)
    mix = jnp.concatenate([o_a, o_b, oc], axis=1).astype(BF16)
    out = x + _dot(mix, wout_ref[...])
    if final:
        out = _rmsnorm_rows(out, fg_ref[...])
    return out


def _cols(name):
    return slice(OFF[name], OFF[name] + _SIZE[name])


def _interleave(gens):
    live = list(gens)
    while live:
        live = [g for g in live if next(g, StopIteration) is not StopIteration]


def _prompt_kernel(layer, final,
                   x_ref, xn_ref, ng_ref, win_ref, lbl_ref, hgain_ref, w2_ref, b2_ref, ggain_ref,
                   bd_ref, cb_ref, dd_ref, wglu_ref, bglu_ref, wout_ref, fg_ref, aseg_ref,
                   tsub_ref, tcar_ref, tpos_ref, perm_ref, wwin_ref, lv_ref,
                   y_ref, shg_ref, sgla_ref, sre_ref, sim_ref,
                   p_ref, hb_ref, cma_ref, cmb_ref, work_ref, mixt_ref,
                   hg_s, gla_s, cre_s, cim_s, lbc_s):
    t_idx = pl.program_id(1)
    step = pl.program_id(0) * pl.num_programs(1) + t_idx
    cur = step % 2

    @pl.when(t_idx == 0)
    def _():
        hg_s[...] = jnp.zeros_like(hg_s)
        gla_s[...] = jnp.zeros_like(gla_s)
        cre_s[...] = jnp.zeros_like(cre_s)
        cim_s[...] = jnp.zeros_like(cim_s)
        _lower_bound_consts(lbl_ref, layer, lbc_s)

    @pl.when(step == 0)
    def _():
        _project(x_ref[...].reshape(STREAMS * LANES, D_MODEL), ng_ref, win_ref, p_ref.at[0])

    lane = lax.broadcasted_iota(jnp.int32, (1, LANES), 1)

    def chunk_steps(k):
        p_cur = p_ref.at[cur, k * LANES:(k + 1) * LANES]
        cma, cmb, mixt = (r.at[k] for r in (cma_ref, cmb_ref, mixt_ref))
        ea = work_ref.at[k]
        eb = work_ref.at[k, :B_QK]
        bu_re, bu_im = work_ref.at[k, :LANES], work_ref.at[k, LANES:2 * LANES]
        hg, gla, cre, cim = (r.at[k] for r in (hg_s, gla_s, cre_s, cim_s))
        cma[...] = p_cur[:, :A_CM].T
        cmb[...] = p_cur[:, OFF["qb"]:OFF["qb"] + B_CM].T
        yield

        log_f, k_a = _hgrn_gates(cma[A_WIDTH:2 * A_WIDTH, :], lbc_s)
        cma[A_WIDTH:2 * A_WIDTH, :] = k_a
        ea[...] = _window_sums(log_f, wwin_ref)
        yield

        def emit_a(hd, o):
            mixt[hd * A_DV:(hd + 1) * A_DV, :] = _head_rms(o)

        yield from _gated_chunk(lambda r: jax.nn.silu(cma[r, :]),
                                lambda r: cma[A_WIDTH + r.start:A_WIDTH + r.stop, :],
                                lambda r: cma[2 * A_WIDTH + r.start:2 * A_WIDTH + r.stop, :],
                                ea, hg, A_HEADS, A_DK, A_DV, lv_ref, lane, emit_a)

        eb[...] = _window_sums(_gla_log_gate_cm(p_cur[:, _cols("rb")], w2_ref, b2_ref), wwin_ref)
        yield

        def emit_b(hd, o):
            mixt[A_WIDTH + hd * B_DV:A_WIDTH + (hd + 1) * B_DV, :] = _head_rms(o)

        yield from _gated_chunk(lambda r: cmb[r, :] * (B_DK ** -0.5),
                                lambda r: cmb[B_QK + r.start:B_QK + r.stop, :],
                                lambda r: cmb[2 * B_QK + r.start:2 * B_QK + r.stop, :],
                                eb, gla, B_HEADS, B_DK, B_DV, lv_ref, lane, emit_b)

        u = p_cur[:, _cols("uc")]
        u_seg = _dot(perm_ref[0], u.astype(BF16)).astype(BF16)
        b = _dot(u_seg, bd_ref[...])
        bu_re[...] = b[:, :S5_DIM]
        bu_im[...] = b[:, S5_DIM:]
        yield
        ar, ai = aseg_ref[0], aseg_ref[1]
        xr = jnp.zeros((SUBLANES, S5_DIM), F32)
        xi = jnp.zeros((SUBLANES, S5_DIM), F32)
        for j in range(SEG):
            rows = slice(j * SUBLANES, (j + 1) * SUBLANES)
            xr, xi = _cmul_add(ar, ai, xr, xi, bu_re[rows, :], bu_im[rows, :])
            bu_re[rows, :] = xr
            bu_im[rows, :] = xi
            yield
        for i in range(3):
            xr, xi = _cmul_add(tsub_ref[i, 0], tsub_ref[i, 1],
                               pltpu.roll(xr, 1 << i, 0), pltpu.roll(xi, 1 << i, 0), xr, xi)
        cr, ci = cre[...], cim[...]
        xr, xi = _cmul_add(tcar_ref[0], tcar_ref[1], cr, ci, xr, xi)
        first = lax.broadcasted_iota(jnp.int32, (SUBLANES, S5_DIM), 0) == 0
        pr = jnp.where(first, cr, pltpu.roll(xr, 1, 0))
        pi = jnp.where(first, ci, pltpu.roll(xi, 1, 0))
        cre[...] = jnp.broadcast_to(xr[SUBLANES - 1:], (SUBLANES, S5_DIM))
        cim[...] = jnp.broadcast_to(xi[SUBLANES - 1:], (SUBLANES, S5_DIM))
        yield
        for j in range(SEG):
            rows = slice(j * SUBLANES, (j + 1) * SUBLANES)
            tr = jnp.broadcast_to(tpos_ref[0, j:j + 1, :], (SUBLANES, S5_DIM))
            ti = jnp.broadcast_to(tpos_ref[1, j:j + 1, :], (SUBLANES, S5_DIM))
            fr, fi = _cmul_add(tr, ti, pr, pi, bu_re[rows, :], bu_im[rows, :])
            bu_re[rows, :] = fr
            bu_im[rows, :] = fi
            yield
        y_seg = _s5_readout(bu_re[...], bu_im[...], cb_ref)
        y1 = y_seg.astype(BF16)
        r1 = y_seg - y1.astype(F32)
        y2 = r1.astype(BF16)
        y3 = (r1 - y2.astype(F32)).astype(BF16)
        yield
        yp = _dot(perm_ref[1], jnp.concatenate([y1, y2, y3], axis=1))
        y_tok = yp[:, :C_WIDTH] + yp[:, C_WIDTH:2 * C_WIDTH] + yp[:, 2 * C_WIDTH:]
        yield
        oc = _s5_gate_tokens(y_tok, u, p_cur[:, _cols("zc")], dd_ref, wglu_ref, bglu_ref)
        yield
        y_ref[k] = _mix_and_project(
            x_ref[k], mixt, p_cur[:, _cols("za")], p_cur[:, _cols("zb")], oc,
            hgain_ref, ggain_ref, wout_ref, fg_ref, final)

    project_next = _project_steps(xn_ref[...].reshape(STREAMS * LANES, D_MODEL), ng_ref, win_ref,
                                  hb_ref, p_ref.at[1 - cur])
    _interleave([chunk_steps(k) for k in range(STREAMS)] + [project_next])

    @pl.when(t_idx == pl.num_programs(1) - 1)
    def _():
        for k in range(STREAMS):
            for h in range(A_HEADS):
                shg_ref[k, h] = hg_s[k, h * A_DK:(h + 1) * A_DK, :]
            for h in range(B_HEADS):
                sgla_ref[k, h] = gla_s[k, h * B_DK:(h + 1) * B_DK, :]
            sre_ref[k] = cre_s[k, :1, :]
            sim_ref[k] = cim_s[k, :1, :]


def _sample_kernel(x_ref, ng_ref, win_ref, lbl_ref, hgain_ref, w2_ref, b2_ref, ggain_ref,
                   bd_ref, cb_ref, dd_ref, wglu_ref, bglu_ref, wout_ref, fg_ref, aseg_ref,
                   shg_in, sgla_in, sre_in, sim_in,
                   y_ref, shg_out, sgla_out, sre_out, sim_out,
                   xs_s, p_ref, cma_ref, cmb_ref, fa_s, gb_s, mixt_ref, sa_s, sb_s, lbc_s):
    l = pl.program_id(0)
    h = pl.program_id(1)

    @pl.when((l == 0) & (h == 0))
    def _():
        xs_s[...] = x_ref[...]

    @pl.when(h == 0)
    def _():
        _lower_bound_consts(lbl_ref, l, lbc_s)
        _project(xs_s[...], ng_ref, win_ref, p_ref)
        cma_ref[...] = p_ref[:, :A_CM].T
        cmb_ref[...] = p_ref[:, OFF["qb"]:OFF["qb"] + B_CM].T
        log_f, k_a = _hgrn_gates(cma_ref[A_WIDTH:2 * A_WIDTH, :], lbc_s)
        fa_s[...] = jnp.exp(log_f)
        cma_ref[A_WIDTH:2 * A_WIDTH, :] = k_a
        cma_ref[:A_WIDTH, :] = jax.nn.silu(cma_ref[:A_WIDTH, :])
        gb_s[...] = jnp.exp(_gla_log_gate_cm(p_ref[:, _cols("rb")], w2_ref, b2_ref))

    sa_s[...] = shg_in[...].T
    ra = pl.multiple_of(h * A_DK, A_DK)
    v_a = cma_ref[pl.ds(2 * A_WIDTH + ra, A_DV), :]

    def hg_body(k, o):
        r = ra + k
        rows = pl.ds(pl.multiple_of(k * A_DV, A_DV), A_DV)
        new = fa_s[pl.ds(r, 1), :] * sa_s[rows, :] + cma_ref[pl.ds(A_WIDTH + r, 1), :] * v_a
        sa_s[rows, :] = new
        return o + cma_ref[pl.ds(r, 1), :] * new

    o = lax.fori_loop(0, A_DK, hg_body, jnp.zeros((A_DV, LANES), F32))
    mixt_ref[pl.ds(ra, A_DV), :] = _head_rms(o)
    shg_out[...] = sa_s[...].T

    sb_s[...] = sgla_in[...].T
    rb = pl.multiple_of(h * B_DK, B_DK)
    vb0 = pl.multiple_of(h * B_DV, B_DV)
    v_b = cmb_ref[pl.ds(2 * B_QK + vb0, B_DV), :]

    def gla_body(k, o):
        r = rb + k
        rows = pl.ds(pl.multiple_of(k * B_DV, B_DV), B_DV)
        new = gb_s[pl.ds(r, 1), :] * sb_s[rows, :] + cmb_ref[pl.ds(B_QK + r, 1), :] * v_b
        sb_s[rows, :] = new
        return o + (cmb_ref[pl.ds(r, 1), :] * (B_DK ** -0.5)) * new

    o = lax.fori_loop(0, B_DK, gla_body, jnp.zeros((B_DV, LANES), F32))
    mixt_ref[pl.ds(A_WIDTH + vb0, B_DV), :] = _head_rms(o)
    sgla_out[...] = sb_s[...].T

    @pl.when(h == pl.num_programs(1) - 1)
    def _():
        u = p_ref[:, _cols("uc")]
        bu = _dot(u.astype(BF16), bd_ref[...])
        x_re, x_im = _cmul_add(aseg_ref[0, :1], aseg_ref[1, :1], sre_in[...], sim_in[...],
                               bu[:, :S5_DIM], bu[:, S5_DIM:])
        sre_out[...] = x_re
        sim_out[...] = x_im
        oc = _s5_gate_tokens(_s5_readout(x_re, x_im, cb_ref), u, p_ref[:, _cols("zc")],
                             dd_ref, wglu_ref, bglu_ref)
        out = _mix_and_project(
            xs_s[...], mixt_ref, p_ref[:, _cols("za")], p_ref[:, _cols("zb")], oc,
            hgain_ref, ggain_ref, wout_ref, fg_ref, False)
        xs_s[...] = out

        @pl.when(l == pl.num_programs(0) - 1)
        def _():
            y_ref[...] = _rmsnorm_rows(out, fg_ref[...])


def _lane_bcast(v):
    return jnp.broadcast_to(v.astype(F32)[..., None], v.shape + (LANES,))


def _row(v):
    return v.astype(F32).reshape(1, -1)


def _powers(ar, ai, n):
    pr, pi = ar[..., None], ai[..., None]
    while pr.shape[-1] < n:
        lr, li = pr[..., -1:], pi[..., -1:]
        pr, pi = (jnp.concatenate([pr, pr * lr - pi * li], axis=-1),
                  jnp.concatenate([pi, pr * li + pi * lr], axis=-1))
    return pr[..., :n], pi[..., :n]


def _group_block_diag(m):
    g, r, c = m.shape
    eye = jnp.eye(g, dtype=m.dtype)
    return (m[:, :, None, :] * eye[:, None, :, None]).reshape(g * r, g * c)


def _s5_tables(A_re, A_im, B_re, B_im, C_re, C_im, log_dt):
    dt = jnp.exp(log_dt.astype(F32))[:, None]
    lr, li = A_re.astype(F32), A_im.astype(F32)
    mag = jnp.exp(lr * dt)
    ab_re = mag * jnp.cos(li * dt)
    ab_im = mag * jnp.sin(li * dt)
    den = lr * lr + li * li
    nr = ab_re - 1.0
    co_re = (nr * lr + ab_im * li) / den
    co_im = (ab_im * lr - nr * li) / den
    Br, Bi = B_re.astype(F32), B_im.astype(F32)
    bb_re = co_re[..., None] * Br - co_im[..., None] * Bi
    bb_im = co_re[..., None] * Bi + co_im[..., None] * Br
    bd = jnp.concatenate([_group_block_diag(bb_re.transpose(0, 2, 1)),
                          _group_block_diag(bb_im.transpose(0, 2, 1))], axis=1).astype(BF16)
    cb = jnp.concatenate([_group_block_diag(C_re.astype(F32).transpose(0, 2, 1)),
                          _group_block_diag(-C_im.astype(F32).transpose(0, 2, 1))], axis=0).astype(BF16)
    pr, pi = _powers(ab_re.reshape(-1), ab_im.reshape(-1), LANES)
    pw = jnp.stack([pr.T, pi.T])
    aseg = jnp.broadcast_to(pw[:, :1], (2, SUBLANES, S5_DIM))
    sub = jnp.arange(SUBLANES)[None, :, None]
    tsub = jnp.stack([jnp.where(sub >= (1 << i), pw[:, SEG * (1 << i) - 1][:, None, :], 0.0) for i in range(3)])
    tcar = pw[:, SEG - 1::SEG]
    tpos = pw[:, :SEG]
    return bd, cb, aseg, tsub, tcar, tpos


def _layer_spec(a, layer):
    nd = a.ndim - 1
    if layer is None:
        return pl.BlockSpec((None,) + a.shape[1:], lambda l, *_, nd=nd: (l,) + (0,) * nd)
    return pl.BlockSpec((None,) + a.shape[1:], lambda *_, nd=nd: (layer,) + (0,) * nd)


def _const_spec(a):
    nd = a.ndim
    return pl.BlockSpec(a.shape, lambda *_, nd=nd: (0,) * nd)


def _operands(norm_g, w_in, hg_lb_logits, hg_norm_g, gla_w2, gla_b2, gla_norm_g,
              s5_A_re, s5_A_im, s5_B_re, s5_B_im, s5_C_re, s5_C_im, s5_D, s5_log_dt,
              s5_w_glu, s5_b_glu, w_out, final_norm_g):
    bd, cb, aseg, tsub, tcar, tpos = jax.vmap(_s5_tables)(
        s5_A_re, s5_A_im, s5_B_re, s5_B_im, s5_C_re, s5_C_im, s5_log_dt)
    w_ker = jnp.concatenate([w_in[..., _REF_OFF[b]:_REF_OFF[b] + _SIZE[b]] for b in KER_BLOCKS], axis=-1)
    pad = 2 * LANES - B_QK
    rows = lambda v: v.astype(F32)[:, None, :]
    common = (
        (rows(norm_g), True),
        (w_ker.astype(BF16), True),
        (_lane_bcast(hg_lb_logits), False),
        (rows(hg_norm_g), True),
        (jnp.pad(gla_w2, ((0, 0), (0, 0), (0, pad))).astype(BF16), True),
        (jnp.pad(rows(gla_b2), ((0, 0), (0, 0), (0, pad))), True),
        (rows(gla_norm_g), True),
        (bd, True), (cb, True),
        (rows(s5_D), True),
        (s5_w_glu.astype(BF16), True),
        (rows(s5_b_glu), True),
        (w_out.astype(BF16), True),
        (_row(final_norm_g), False),
        (aseg, True),
    )
    return common, ((tsub, True), (tcar, True), (tpos, True))


def _prompt_layer(layer, final, x, common, scan_tables, chunk_tables):
    bsz, t, _ = x.shape
    assert t % LANES == 0 and bsz % STREAMS == 0
    nb, nt = bsz // STREAMS, t // LANES
    consts = common + scan_tables + tuple((a, False) for a in chunk_tables)
    const_specs = [_layer_spec(a, layer) if stacked else _const_spec(a) for a, stacked in consts]
    out_shape = (
        jax.ShapeDtypeStruct((bsz, t, D_MODEL), F32),
        jax.ShapeDtypeStruct((bsz, A_HEADS, A_DK, A_DV), F32),
        jax.ShapeDtypeStruct((bsz, B_HEADS, B_DK, B_DV), F32),
        jax.ShapeDtypeStruct((bsz, 1, S5_DIM), F32),
        jax.ShapeDtypeStruct((bsz, 1, S5_DIM), F32),
    )
    x_spec = pl.BlockSpec((STREAMS, LANES, D_MODEL), lambda b, i: (b, i, 0))

    def next_block(b, i):
        nxt = jnp.minimum(b * nt + i + 1, nb * nt - 1)
        return (nxt // nt, nxt % nt, 0)

    out_specs = (
        x_spec,
        pl.BlockSpec((STREAMS, A_HEADS, A_DK, A_DV), lambda b, i: (b, 0, 0, 0)),
        pl.BlockSpec((STREAMS, B_HEADS, B_DK, B_DV), lambda b, i: (b, 0, 0, 0)),
        pl.BlockSpec((STREAMS, 1, S5_DIM), lambda b, i: (b, 0, 0)),
        pl.BlockSpec((STREAMS, 1, S5_DIM), lambda b, i: (b, 0, 0)),
    )
    scratch = [
        pltpu.VMEM((2, STREAMS * LANES, IN_TOTAL), F32),
        pltpu.VMEM((STREAMS * LANES, D_MODEL), BF16),
        pltpu.VMEM((STREAMS, A_CM, LANES), F32),
        pltpu.VMEM((STREAMS, B_CM, LANES), F32),
        pltpu.VMEM((STREAMS, A_WIDTH, N_WIN * LANES), F32),
        pltpu.VMEM((STREAMS, A_WIDTH + B_WIDTH, LANES), F32),
        pltpu.VMEM((STREAMS, A_WIDTH, A_DV), F32),
        pltpu.VMEM((STREAMS, B_QK, B_DV), F32),
        pltpu.VMEM((STREAMS, SUBLANES, S5_DIM), F32),
        pltpu.VMEM((STREAMS, SUBLANES, S5_DIM), F32),
        pltpu.VMEM((3, A_WIDTH, LANES), F32),
    ]
    return pl.pallas_call(
        functools.partial(_prompt_kernel, layer, final),
        grid=(nb, nt),
        in_specs=[x_spec, pl.BlockSpec((STREAMS, LANES, D_MODEL), next_block)] + const_specs,
        out_specs=out_specs,
        out_shape=out_shape,
        scratch_shapes=scratch,
        compiler_params=pltpu.CompilerParams(
            dimension_semantics=("arbitrary", "arbitrary"), vmem_limit_bytes=VMEM_LIMIT),
    )(x, x, *(a for a, _ in consts))


def _sample_layers(xs, common, shg, sgla, sre, sim):
    n = xs.shape[0]
    depth = shg.shape[0]
    assert n == LANES
    hg_cols, gla_cols = A_DK * A_DV, B_DK * B_DV
    state_shapes = [s.shape for s in (shg, sgla, sre, sim)]
    out_shape = [jax.ShapeDtypeStruct((n, D_MODEL), F32)] + [jax.ShapeDtypeStruct(s, F32) for s in state_shapes]
    full = pl.BlockSpec((n, D_MODEL), lambda l, h: (0, 0))
    state_specs = [
        pl.BlockSpec((None, n, hg_cols), lambda l, h: (l, 0, h)),
        pl.BlockSpec((None, n, gla_cols), lambda l, h: (l, 0, h)),
        pl.BlockSpec((None, n, S5_DIM), lambda l, h: (l, 0, 0)),
        pl.BlockSpec((None, n, S5_DIM), lambda l, h: (l, 0, 0)),
    ]
    scratch = [
        pltpu.VMEM((n, D_MODEL), F32),
        pltpu.VMEM((n, IN_TOTAL), F32),
        pltpu.VMEM((A_CM, LANES), F32),
        pltpu.VMEM((B_CM, LANES), F32),
        pltpu.VMEM((A_WIDTH, LANES), F32),
        pltpu.VMEM((B_QK, LANES), F32),
        pltpu.VMEM((A_WIDTH + B_WIDTH, LANES), F32),
        pltpu.VMEM((hg_cols, LANES), F32),
        pltpu.VMEM((gla_cols, LANES), F32),
        pltpu.VMEM((3, A_WIDTH, LANES), F32),
    ]
    return pl.pallas_call(
        _sample_kernel,
        grid=(depth, A_HEADS),
        in_specs=([full] + [_layer_spec(a, None) if stacked else _const_spec(a) for a, stacked in common]
                  + state_specs),
        out_specs=[full] + state_specs,
        out_shape=out_shape,
        scratch_shapes=scratch,
        compiler_params=pltpu.CompilerParams(
            dimension_semantics=("arbitrary", "arbitrary"), vmem_limit_bytes=VMEM_LIMIT),
    )(xs, *(a for a, _ in common), shg, sgla, sre, sim)


def kernel(x_prompt, x_sample, state_hgrn, state_gla, state_s5_re, state_s5_im, norm_g, w_in, hg_lb_logits, hg_norm_g, gla_w2, gla_b2, gla_norm_g, s5_A_re, s5_A_im, s5_B_re, s5_B_im, s5_C_re, s5_C_im, s5_D, s5_log_dt, s5_w_glu, s5_b_glu, w_out, final_norm_g):
    depth = w_in.shape[0]
    bsz = x_prompt.shape[0]
    n = x_sample.shape[0]
    assert x_sample.shape[1] == 1 and A_HEADS == B_HEADS
    win_np, lv_np, perm_np = _chunk_tables()
    chunk_tables = (jnp.asarray(perm_np, BF16), jnp.asarray(win_np, BF16), jnp.asarray(lv_np))
    common, scan_tables = _operands(
        norm_g, w_in, hg_lb_logits, hg_norm_g, gla_w2, gla_b2, gla_norm_g,
        s5_A_re, s5_A_im, s5_B_re, s5_B_im, s5_C_re, s5_C_im, s5_D, s5_log_dt,
        s5_w_glu, s5_b_glu, w_out, final_norm_g)

    yp = x_prompt.astype(F32)
    prompt_states = []
    for l in range(depth):
        yp, *states = _prompt_layer(l, l == depth - 1, yp, common, scan_tables, chunk_tables)
        prompt_states.append(states)
    hg_p, gla_p, re_p, im_p = (jnp.stack([s[i] for s in prompt_states], axis=0) for i in range(4))

    flat = lambda s: s.astype(F32).reshape(depth, n, -1)
    ys, hg_s, gla_s, re_s, im_s = _sample_layers(
        x_sample.astype(F32).reshape(n, D_MODEL), common,
        flat(state_hgrn), flat(state_gla), flat(state_s5_re), flat(state_s5_im))

    return (yp, ys.reshape(n, 1, D_MODEL),
            hg_p, gla_p, re_p.reshape(depth, bsz, C_GROUPS, C_STATE), im_p.reshape(depth, bsz, C_GROUPS, C_STATE),
            hg_s.reshape(state_hgrn.shape), gla_s.reshape(state_gla.shape),
            re_s.reshape(state_s5_re.shape), im_s.reshape(state_s5_im.shape))
```

```python
import functools

import numpy as np
import jax
import jax.numpy as jnp
from jax import lax
from jax.experimental import pallas as pl
from jax.experimental.pallas import tpu as pltpu

F32 = jnp.float32
BF16 = jnp.bfloat16

D_MODEL = 1024
A_WIDTH, A_DK, A_DV, A_HEADS = 384, 64, 64, 6
B_WIDTH, B_DV, B_HEADS, B_DK, B_QK = 384, 64, 6, 32, 192
GLA_RANK = 16
GLA_TAU = 16.0
C_WIDTH, C_GROUP, C_GROUPS, C_STATE = 256, 16, 16, 64
S5_DIM = C_GROUPS * C_STATE
NORM_EPS = 1e-6
REF_BLOCKS = ("qa", "fa", "ia", "za", "qb", "kb", "vb", "zb", "rb", "uc", "zc")
REF_SIZES = (A_WIDTH, A_WIDTH, A_WIDTH, A_WIDTH, B_QK, B_QK, B_WIDTH, B_WIDTH, GLA_RANK, C_WIDTH, C_WIDTH)
KER_BLOCKS = ("qa", "fa", "ia", "za", "qb", "kb", "vb", "zb", "uc", "zc", "rb")
IN_TOTAL = sum(REF_SIZES)
_SIZE = dict(zip(REF_BLOCKS, REF_SIZES))
_REF_OFF = dict(zip(REF_BLOCKS, np.concatenate([[0], np.cumsum(REF_SIZES)[:-1]]).tolist()))
OFF = dict(zip(KER_BLOCKS, np.concatenate([[0], np.cumsum([_SIZE[b] for b in KER_BLOCKS])[:-1]]).tolist()))
A_CM = 3 * A_WIDTH
B_CM = 2 * B_QK + B_WIDTH

LANES = 128
SUBLANES = 8
N_LEVELS = 7
N_WIN = N_LEVELS + 1
SEG = LANES // SUBLANES
STREAMS = 4
PROJ_COLS = 256
PROJ_GAP = 6
VMEM_LIMIT = 56 * 1024 * 1024


def _chunk_tables():
    s = np.arange(LANES)[:, None]
    t = np.arange(LANES)[None, :]
    wins = []
    for l in range(N_LEVELS):
        h = (LANES // 2) >> l
        mid = (t // (2 * h)) * (2 * h) + h - 1
        role_q = (t & h) != 0
        wins.append(np.where(role_q, (s > mid) & (s <= t), (s > t) & (s <= mid)))
    wins.append(s <= t)
    win = np.concatenate(wins, axis=1).astype(np.float32)
    x = s ^ t
    lv = np.full((LANES, LANES), -1, np.int32)
    for l in range(N_LEVELS):
        h = (LANES // 2) >> l
        lv[(x >= h) & (x < 2 * h) & (t > s)] = l
    perm = np.zeros((LANES, LANES), np.float32)
    i, j = np.divmod(np.arange(LANES), SEG)
    perm[SUBLANES * j + i, np.arange(LANES)] = 1.0
    return np.concatenate([win, win], axis=0), np.tile(lv, (1, 2)), np.stack([perm, perm.T])


def _dot(a, b):
    return jnp.dot(a, b, preferred_element_type=F32)


def _dot_nt(a, b):
    return lax.dot_general(a, b, (((1,), (1,)), ((), ())), preferred_element_type=F32)


def _rmsnorm_rows(x, g):
    r = lax.rsqrt(jnp.mean(x * x, axis=-1, keepdims=True) + NORM_EPS)
    return x * r * g


def _lower_bound_consts(lbl_ref, layer, out_ref):
    rows = [lbl_ref[l] for l in range(lbl_ref.shape[0])]
    m = functools.reduce(jnp.maximum, rows)
    es = [jnp.exp(r - m) for r in rows]
    tot = functools.reduce(lambda a, b: a + b, es)
    sm = [e / tot for e in es]
    cum, lbs = sm[0], [sm[0] - sm[0]]
    for l in range(1, len(sm)):
        cum = cum + sm[l]
        lbs.append(cum - sm[0])
    if isinstance(layer, int):
        lb = lbs[layer]
    else:
        lb = lbs[0]
        for l in range(1, len(lbs)):
            lb = jnp.where(layer == l, lbs[l], lb)
    out_ref[0] = jnp.log(lb)
    out_ref[1] = jnp.log(1.0 - lb)
    out_ref[2] = 1.0 - lb


def _hgrn_gates(fa, lbc_ref):
    e = jnp.exp(-jnp.abs(fa))
    logsig = jnp.minimum(fa, 0.0) - jnp.log(1.0 + e)
    a = lbc_ref[0]
    c = lbc_ref[1] + logsig
    log_f = jnp.maximum(a, c) + jnp.log(1.0 + jnp.exp(-jnp.abs(a - c)))
    k_a = lbc_ref[2] * (jnp.where(fa >= 0.0, e, 1.0) / (1.0 + e))
    return log_f, k_a


def _gla_log_gate_cm(rb, w2_ref, b2_ref):
    pre = _dot(rb.astype(BF16), w2_ref[...]) + b2_ref[...]
    return (jax.nn.log_sigmoid(pre) / GLA_TAU).T[:B_QK]


def _window_sums(g, win_ref):
    hi = g.astype(BF16)
    lo = (g - hi.astype(F32)).astype(BF16)
    return _dot(jnp.concatenate([hi, lo], axis=1), win_ref[...])


def _gated_chunk(load_q, load_k, load_v, e_ref, s_ref, heads, dk, dv, lv_ref, lane, emit):
    for p in range(heads // 2):
        rows = slice(2 * p * dk, (2 * p + 2) * dk)
        q, k = load_q(rows), load_k(rows)
        att = [jnp.zeros((SUBLANES, 2 * LANES), F32)] * (LANES // SUBLANES)
        for l in range(N_LEVELS):
            h = (LANES // 2) >> l
            z = jnp.where((lane & h) != 0, q, k) * jnp.exp(e_ref[rows, l * LANES:(l + 1) * LANES])
            zb = z.astype(BF16)
            zero = jnp.zeros((dk, LANES), BF16)
            rhs = jnp.concatenate([jnp.concatenate([zb[:dk], zero], axis=1),
                                   jnp.concatenate([zero, zb[dk:]], axis=1)], axis=0)
            zt = z.T
            blocks = [b for b in range(LANES // SUBLANES) if h < SUBLANES or (b * SUBLANES) & h == 0]
            lhs = jnp.concatenate([zt[b * SUBLANES:(b + 1) * SUBLANES] for b in blocks], axis=0)
            gram = _dot(lhs.astype(BF16), rhs)
            for i, b in enumerate(blocks):
                sl = slice(b * SUBLANES, (b + 1) * SUBLANES)
                att[b] = jnp.where(lv_ref[sl, :] == l, gram[i * SUBLANES:(i + 1) * SUBLANES], att[b])
            yield
        attb = jnp.concatenate(att, axis=0).astype(BF16)
        bq = e_ref[rows, N_LEVELS * LANES:(N_LEVELS + 1) * LANES]
        b_last = bq[:, LANES - 1:LANES]
        qs = (q * jnp.exp(bq)).astype(BF16)
        kd = (k * jnp.exp(b_last - bq)).astype(BF16)
        dec = jnp.exp(b_last)
        qk = q * k
        for i in range(2):
            hd = 2 * p + i
            r = slice(i * dk, (i + 1) * dk)
            rs = slice(hd * dk, (hd + 1) * dk)
            v = load_v(slice(hd * dv, (hd + 1) * dv))
            vb = v.astype(BF16)
            s_old = s_ref[rs, :]
            o = _dot(s_old.T.astype(BF16), qs[r])
            o = o + _dot(vb, attb[:, i * LANES:(i + 1) * LANES])
            o = o + jnp.sum(qk[r], axis=0, keepdims=True) * v
            s_ref[rs, :] = dec[r] * s_old + _dot_nt(kd[r], vb)
            emit(hd, o)
            yield


def _head_rms(o):
    return o * lax.rsqrt(jnp.mean(o * o, axis=0, keepdims=True) + NORM_EPS)


def _cmul_add(ar, ai, xr, xi, br, bi):
    return br + ar * xr - ai * xi, bi + ar * xi + ai * xr


def _s5_readout(x_re, x_im, cb_ref):
    return _dot(jnp.concatenate([x_re, x_im], axis=1).astype(BF16), cb_ref[...])


def _s5_gate_tokens(y, u, zc, d_ref, wglu_ref, bglu_ref):
    y = y + d_ref[...] * u
    y = jax.nn.gelu(y)
    y = y * jax.nn.sigmoid(_dot(y.astype(BF16), wglu_ref[...]) + bglu_ref[...])
    return y * jax.nn.silu(zc)


def _project(x, ng_ref, win_ref, p_ref):
    h = _rmsnorm_rows(x, ng_ref[...])
    p_ref[...] = _dot(h.astype(BF16), win_ref[...])


def _project_steps(x, ng_ref, win_ref, hb_ref, p_ref):
    hb_ref[...] = _rmsnorm_rows(x, ng_ref[...]).astype(BF16)
    for c0 in range(0, IN_TOTAL, PROJ_COLS):
        for _ in range(PROJ_GAP + 1):
            yield
        cols = slice(c0, min(c0 + PROJ_COLS, IN_TOTAL))
        p_ref[:, cols] = _dot(hb_ref[...], win_ref[:, cols])


def _mix_and_project(x, mixt_ref, za, zb, oc, hgain_ref, ggain_ref, wout_ref, fg_ref, final):
    o_t = mixt_ref[...].T
    o_a = o_t[:, :A_WIDTH] * hgain_ref[...] * jax.nn.silu(za)
    o_b = o_t[:, A_WIDTH:] * ggain_ref[...] * jax.nn.silu(zb)
    mix = jnp.concatenate([o_a, o_b, oc], axis=1).astype(BF16)
    out = x + _dot(mix, wout_ref[...])
    if final:
        out = _rmsnorm_rows(out, fg_ref[...])
    return out


def _cols(name):
    return slice(OFF[name], OFF[name] + _SIZE[name])


def _interleave(gens):
    live = list(gens)
    while live:
        live = [g for g in live if next(g, StopIteration) is not StopIteration]


def _prompt_kernel(layer, final,
                   x_ref, xn_ref, ng_ref, win_ref, lbl_ref, hgain_ref, w2_ref, b2_ref, ggain_ref,
                   bd_ref, cb_ref, dd_ref, wglu_ref, bglu_ref, wout_ref, fg_ref, aseg_ref,
                   tsub_ref, tcar_ref, tpos_ref, perm_ref, wwin_ref, lv_ref,
                   y_ref, shg_ref, sgla_ref, sre_ref, sim_ref,
                   p_ref, hb_ref, cma_ref, cmb_ref, work_ref, mixt_ref,
                   hg_s, gla_s, cre_s, cim_s, lbc_s):
    t_idx = pl.program_id(1)
    step = pl.program_id(0) * pl.num_programs(1) + t_idx
    cur = step % 2

    @pl.when(t_idx == 0)
    def _():
        hg_s[...] = jnp.zeros_like(hg_s)
        gla_s[...] = jnp.zeros_like(gla_s)
        cre_s[...] = jnp.zeros_like(cre_s)
        cim_s[...] = jnp.zeros_like(cim_s)
        _lower_bound_consts(lbl_ref, layer, lbc_s)

    @pl.when(step == 0)
    def _():
        _project(x_ref[...].reshape(STREAMS * LANES, D_MODEL), ng_ref, win_ref, p_ref.at[0])

    lane = lax.broadcasted_iota(jnp.int32, (1, LANES), 1)

    def chunk_steps(k):
        p_cur = p_ref.at[cur, k * LANES:(k + 1) * LANES]
        cma, cmb, mixt = (r.at[k] for r in (cma_ref, cmb_ref, mixt_ref))
        ea = work_ref.at[k]
        eb = work_ref.at[k, :B_QK]
        bu_re, bu_im = work_ref.at[k, :LANES], work_ref.at[k, LANES:2 * LANES]
        hg, gla, cre, cim = (r.at[k] for r in (hg_s, gla_s, cre_s, cim_s))
        cma[...] = p_cur[:, :A_CM].T
        cmb[...] = p_cur[:, OFF["qb"]:OFF["qb"] + B_CM].T
        yield

        log_f, k_a = _hgrn_gates(cma[A_WIDTH:2 * A_WIDTH, :], lbc_s)
        cma[A_WIDTH:2 * A_WIDTH, :] = k_a
        ea[...] = _window_sums(log_f, wwin_ref)
        yield

        def emit_a(hd, o):
            mixt[hd * A_DV:(hd + 1) * A_DV, :] = _head_rms(o)

        yield from _gated_chunk(lambda r: jax.nn.silu(cma[r, :]),
                                lambda r: cma[A_WIDTH + r.start:A_WIDTH + r.stop, :],
                                lambda r: cma[2 * A_WIDTH + r.start:2 * A_WIDTH + r.stop, :],
                                ea, hg, A_HEADS, A_DK, A_DV, lv_ref, lane, emit_a)

        eb[...] = _window_sums(_gla_log_gate_cm(p_cur[:, _cols("rb")], w2_ref, b2_ref), wwin_ref)
        yield

        def emit_b(hd, o):
            mixt[A_WIDTH + hd * B_DV:A_WIDTH + (hd + 1) * B_DV, :] = _head_rms(o)

        yield from _gated_chunk(lambda r: cmb[r, :] * (B_DK ** -0.5),
                                lambda r: cmb[B_QK + r.start:B_QK + r.stop, :],
                                lambda r: cmb[2 * B_QK + r.start:2 * B_QK + r.stop, :],
                                eb, gla, B_HEADS, B_DK, B_DV, lv_ref, lane, emit_b)

        u = p_cur[:, _cols("uc")]
        u_seg = _dot(perm_ref[0], u.astype(BF16)).astype(BF16)
        b = _dot(u_seg, bd_ref[...])
        bu_re[...] = b[:, :S5_DIM]
        bu_im[...] = b[:, S5_DIM:]
        yield
        ar, ai = aseg_ref[0], aseg_ref[1]
        xr = jnp.zeros((SUBLANES, S5_DIM), F32)
        xi = jnp.zeros((SUBLANES, S5_DIM), F32)
        for j in range(SEG):
            rows = slice(j * SUBLANES, (j + 1) * SUBLANES)
            xr, xi = _cmul_add(ar, ai, xr, xi, bu_re[rows, :], bu_im[rows, :])
            bu_re[rows, :] = xr
            bu_im[rows, :] = xi
            yield
        for i in range(3):
            xr, xi = _cmul_add(tsub_ref[i, 0], tsub_ref[i, 1],
                               pltpu.roll(xr, 1 << i, 0), pltpu.roll(xi, 1 << i, 0), xr, xi)
        cr, ci = cre[...], cim[...]
        xr, xi = _cmul_add(tcar_ref[0], tcar_ref[1], cr, ci, xr, xi)
        first = lax.broadcasted_iota(jnp.int32, (SUBLANES, S5_DIM), 0) == 0
        pr = jnp.where(first, cr, pltpu.roll(xr, 1, 0))
        pi = jnp.where(first, ci, pltpu.roll(xi, 1, 0))
        cre[...] = jnp.broadcast_to(xr[SUBLANES - 1:], (SUBLANES, S5_DIM))
        cim[...] = jnp.broadcast_to(xi[SUBLANES - 1:], (SUBLANES, S5_DIM))
        yield
        for j in range(SEG):
            rows = slice(j * SUBLANES, (j + 1) * SUBLANES)
            tr = jnp.broadcast_to(tpos_ref[0, j:j + 1, :], (SUBLANES, S5_DIM))
            ti = jnp.broadcast_to(tpos_ref[1, j:j + 1, :], (SUBLANES, S5_DIM))
            fr, fi = _cmul_add(tr, ti, pr, pi, bu_re[rows, :], bu_im[rows, :])
            bu_re[rows, :] = fr
            bu_im[rows, :] = fi
            yield
        y_seg = _s5_readout(bu_re[...], bu_im[...], cb_ref)
        y1 = y_seg.astype(BF16)
        r1 = y_seg - y1.astype(F32)
        y2 = r1.astype(BF16)
        y3 = (r1 - y2.astype(F32)).astype(BF16)
        yield
        yp = _dot(perm_ref[1], jnp.concatenate([y1, y2, y3], axis=1))
        y_tok = yp[:, :C_WIDTH] + yp[:, C_WIDTH:2 * C_WIDTH] + yp[:, 2 * C_WIDTH:]
        yield
        oc = _s5_gate_tokens(y_tok, u, p_cur[:, _cols("zc")], dd_ref, wglu_ref, bglu_ref)
        yield
        y_ref[k] = _mix_and_project(
            x_ref[k], mixt, p_cur[:, _cols("za")], p_cur[:, _cols("zb")], oc,
            hgain_ref, ggain_ref, wout_ref, fg_ref, final)

    project_next = _project_steps(xn_ref[...].reshape(STREAMS * LANES, D_MODEL), ng_ref, win_ref,
                                  hb_ref, p_ref.at[1 - cur])
    _interleave([chunk_steps(k) for k in range(STREAMS)] + [project_next])

    @pl.when(t_idx == pl.num_programs(1) - 1)
    def _():
        for k in range(STREAMS):
            for h in range(A_HEADS):
                shg_ref[k, h] = hg_s[k, h * A_DK:(h + 1) * A_DK, :]
            for h in range(B_HEADS):
                sgla_ref[k, h] = gla_s[k, h * B_DK:(h + 1) * B_DK, :]
            sre_ref[k] = cre_s[k, :1, :]
            sim_ref[k] = cim_s[k, :1, :]


def _sample_kernel(x_ref, ng_ref, win_ref, lbl_ref, hgain_ref, w2_ref, b2_ref, ggain_ref,
                   bd_ref, cb_ref, dd_ref, wglu_ref, bglu_ref, wout_ref, fg_ref, aseg_ref,
                   shg_in, sgla_in, sre_in, sim_in,
                   y_ref, shg_out, sgla_out, sre_out, sim_out,
                   xs_s, p_ref, cma_ref, cmb_ref, fa_s, gb_s, mixt_ref, sa_s, sb_s, lbc_s):
    l = pl.program_id(0)
    h = pl.program_id(1)

    @pl.when((l == 0) & (h == 0))
    def _():
        xs_s[...] = x_ref[...]

    @pl.when(h == 0)
    def _():
        _lower_bound_consts(lbl_ref, l, lbc_s)
        _project(xs_s[...], ng_ref, win_ref, p_ref)
        cma_ref[...] = p_ref[:, :A_CM].T
        cmb_ref[...] = p_ref[:, OFF["qb"]:OFF["qb"] + B_CM].T
        log_f, k_a = _hgrn_gates(cma_ref[A_WIDTH:2 * A_WIDTH, :], lbc_s)
        fa_s[...] = jnp.exp(log_f)
        cma_ref[A_WIDTH:2 * A_WIDTH, :] = k_a
        cma_ref[:A_WIDTH, :] = jax.nn.silu(cma_ref[:A_WIDTH, :])
        gb_s[...] = jnp.exp(_gla_log_gate_cm(p_ref[:, _cols("rb")], w2_ref, b2_ref))

    sa_s[...] = shg_in[...].T
    ra = pl.multiple_of(h * A_DK, A_DK)
    v_a = cma_ref[pl.ds(2 * A_WIDTH + ra, A_DV), :]

    def hg_body(k, o):
        r = ra + k
        rows = pl.ds(pl.multiple_of(k * A_DV, A_DV), A_DV)
        new = fa_s[pl.ds(r, 1), :] * sa_s[rows, :] + cma_ref[pl.ds(A_WIDTH + r, 1), :] * v_a
        sa_s[rows, :] = new
        return o + cma_ref[pl.ds(r, 1), :] * new

    o = lax.fori_loop(0, A_DK, hg_body, jnp.zeros((A_DV, LANES), F32))
    mixt_ref[pl.ds(ra, A_DV), :] = _head_rms(o)
    shg_out[...] = sa_s[...].T

    sb_s[...] = sgla_in[...].T
    rb = pl.multiple_of(h * B_DK, B_DK)
    vb0 = pl.multiple_of(h * B_DV, B_DV)
    v_b = cmb_ref[pl.ds(2 * B_QK + vb0, B_DV), :]

    def gla_body(k, o):
        r = rb + k
        rows = pl.ds(pl.multiple_of(k * B_DV, B_DV), B_DV)
        new = gb_s[pl.ds(r, 1), :] * sb_s[rows, :] + cmb_ref[pl.ds(B_QK + r, 1), :] * v_b
        sb_s[rows, :] = new
        return o + (cmb_ref[pl.ds(r, 1), :] * (B_DK ** -0.5)) * new

    o = lax.fori_loop(0, B_DK, gla_body, jnp.zeros((B_DV, LANES), F32))
    mixt_ref[pl.ds(A_WIDTH + vb0, B_DV), :] = _head_rms(o)
    sgla_out[...] = sb_s[...].T

    @pl.when(h == pl.num_programs(1) - 1)
    def _():
        u = p_ref[:, _cols("uc")]
        bu = _dot(u.astype(BF16), bd_ref[...])
        x_re, x_im = _cmul_add(aseg_ref[0, :1], aseg_ref[1, :1], sre_in[...], sim_in[...],
                               bu[:, :S5_DIM], bu[:, S5_DIM:])
        sre_out[...] = x_re
        sim_out[...] = x_im
        oc = _s5_gate_tokens(_s5_readout(x_re, x_im, cb_ref), u, p_ref[:, _cols("zc")],
                             dd_ref, wglu_ref, bglu_ref)
        out = _mix_and_project(
            xs_s[...], mixt_ref, p_ref[:, _cols("za")], p_ref[:, _cols("zb")], oc,
            hgain_ref, ggain_ref, wout_ref, fg_ref, False)
        xs_s[...] = out

        @pl.when(l == pl.num_programs(0) - 1)
        def _():
            y_ref[...] = _rmsnorm_rows(out, fg_ref[...])


def _lane_bcast(v):
    return jnp.broadcast_to(v.astype(F32)[..., None], v.shape + (LANES,))


def _row(v):
    return v.astype(F32).reshape(1, -1)


def _group_block_diag(m):
    g, r, c = m.shape
    eye = jnp.eye(g, dtype=m.dtype)
    return (m[:, :, None, :] * eye[:, None, :, None]).reshape(g * r, g * c)


def _s5_tables(A_re, A_im, B_re, B_im, C_re, C_im, log_dt):
    dt = jnp.exp(log_dt.astype(F32))[:, None]
    lr, li = A_re.astype(F32), A_im.astype(F32)
    mag = jnp.exp(lr * dt)
    ab_re = mag * jnp.cos(li * dt)
    ab_im = mag * jnp.sin(li * dt)
    den = lr * lr + li * li
    nr = ab_re - 1.0
    co_re = (nr * lr + ab_im * li) / den
    co_im = (ab_im * lr - nr * li) / den
    Br, Bi = B_re.astype(F32), B_im.astype(F32)
    bb_re = co_re[..., None] * Br - co_im[..., None] * Bi
    bb_im = co_re[..., None] * Bi + co_im[..., None] * Br
    bd = jnp.concatenate([_group_block_diag(bb_re.transpose(0, 2, 1)),
                          _group_block_diag(bb_im.transpose(0, 2, 1))], axis=1).astype(BF16)
    cb = jnp.concatenate([_group_block_diag(C_re.astype(F32).transpose(0, 2, 1)),
                          _group_block_diag(-C_im.astype(F32).transpose(0, 2, 1))], axis=0).astype(BF16)
    ns = np.concatenate([np.arange(1, SEG + 1), np.arange(2 * SEG, LANES + 1, SEG)]).astype(np.float32)
    mg = jnp.exp(ns[:, None] * (lr * dt).reshape(-1)[None, :])
    ang = ns[:, None] * (li * dt).reshape(-1)[None, :]
    pw = jnp.stack([mg * jnp.cos(ang), mg * jnp.sin(ang)])
    aseg = jnp.broadcast_to(pw[:, :1], (2, SUBLANES, S5_DIM))
    sub = jnp.arange(SUBLANES)[None, :, None]
    row_of = {int(n): r for r, n in enumerate(ns)}
    tsub = jnp.stack([jnp.where(sub >= (1 << i), pw[:, row_of[SEG << i]][:, None, :], 0.0) for i in range(3)])
    tcar = pw[:, SEG - 1:]
    tpos = pw[:, :SEG]
    return bd, cb, aseg, tsub, tcar, tpos


def _win_prep_kernel(w_ref, o_ref):
    for b in KER_BLOCKS:
        o_ref[:, OFF[b]:OFF[b] + _SIZE[b]] = w_ref[:, _REF_OFF[b]:_REF_OFF[b] + _SIZE[b]].astype(BF16)


def _win_prep(w_in, rows=256):
    depth, d, n = w_in.shape
    spec = pl.BlockSpec((None, rows, n), lambda l, i: (l, i, 0))
    return pl.pallas_call(
        _win_prep_kernel, grid=(depth, d // rows), in_specs=[spec], out_specs=spec,
        out_shape=jax.ShapeDtypeStruct(w_in.shape, BF16),
        compiler_params=pltpu.CompilerParams(dimension_semantics=("arbitrary", "arbitrary")),
    )(w_in)


def _layer_spec(a, layer):
    nd = a.ndim - 1
    if layer is None:
        return pl.BlockSpec((None,) + a.shape[1:], lambda l, *_, nd=nd: (l,) + (0,) * nd)
    return pl.BlockSpec((None,) + a.shape[1:], lambda *_, nd=nd: (layer,) + (0,) * nd)


def _const_spec(a):
    nd = a.ndim
    return pl.BlockSpec(a.shape, lambda *_, nd=nd: (0,) * nd)


def _operands(norm_g, w_in, hg_lb_logits, hg_norm_g, gla_w2, gla_b2, gla_norm_g,
              s5_A_re, s5_A_im, s5_B_re, s5_B_im, s5_C_re, s5_C_im, s5_D, s5_log_dt,
              s5_w_glu, s5_b_glu, w_out, final_norm_g):
    bd, cb, aseg, tsub, tcar, tpos = jax.vmap(_s5_tables)(
        s5_A_re, s5_A_im, s5_B_re, s5_B_im, s5_C_re, s5_C_im, s5_log_dt)
    pad = 2 * LANES - B_QK
    rows = lambda v: v.astype(F32)[:, None, :]
    common = (
        (rows(norm_g), True),
        (_win_prep(w_in.astype(F32)), True),
        (_lane_bcast(hg_lb_logits), False),
        (rows(hg_norm_g), True),
        (jnp.pad(gla_w2, ((0, 0), (0, 0), (0, pad))).astype(BF16), True),
        (jnp.pad(rows(gla_b2), ((0, 0), (0, 0), (0, pad))), True),
        (rows(gla_norm_g), True),
        (bd, True), (cb, True),
        (rows(s5_D), True),
        (s5_w_glu.astype(BF16), True),
        (rows(s5_b_glu), True),
        (w_out.astype(BF16), True),
        (_row(final_norm_g), False),
        (aseg, True),
    )
    return common, ((tsub, True), (tcar, True), (tpos, True))


def _prompt_layer(layer, final, x, common, scan_tables, chunk_tables):
    bsz, t, _ = x.shape
    assert t % LANES == 0 and bsz % STREAMS == 0
    nb, nt = bsz // STREAMS, t // LANES
    consts = common + scan_tables + tuple((a, False) for a in chunk_tables)
    const_specs = [_layer_spec(a, layer) if stacked else _const_spec(a) for a, stacked in consts]
    out_shape = (
        jax.ShapeDtypeStruct((bsz, t, D_MODEL), F32),
        jax.ShapeDtypeStruct((bsz, A_HEADS, A_DK, A_DV), F32),
        jax.ShapeDtypeStruct((bsz, B_HEADS, B_DK, B_DV), F32),
        jax.ShapeDtypeStruct((bsz, 1, S5_DIM), F32),
        jax.ShapeDtypeStruct((bsz, 1, S5_DIM), F32),
    )
    x_spec = pl.BlockSpec((STREAMS, LANES, D_MODEL), lambda b, i: (b, i, 0))

    def next_block(b, i):
        nxt = jnp.minimum(b * nt + i + 1, nb * nt - 1)
        return (nxt // nt, nxt % nt, 0)

    out_specs = (
        x_spec,
        pl.BlockSpec((STREAMS, A_HEADS, A_DK, A_DV), lambda b, i: (b, 0, 0, 0)),
        pl.BlockSpec((STREAMS, B_HEADS, B_DK, B_DV), lambda b, i: (b, 0, 0, 0)),
        pl.BlockSpec((STREAMS, 1, S5_DIM), lambda b, i: (b, 0, 0)),
        pl.BlockSpec((STREAMS, 1, S5_DIM), lambda b, i: (b, 0, 0)),
    )
    scratch = [
        pltpu.VMEM((2, STREAMS * LANES, IN_TOTAL), F32),
        pltpu.VMEM((STREAMS * LANES, D_MODEL), BF16),
        pltpu.VMEM((STREAMS, A_CM, LANES), F32),
        pltpu.VMEM((STREAMS, B_CM, LANES), F32),
        pltpu.VMEM((STREAMS, A_WIDTH, N_WIN * LANES), F32),
        pltpu.VMEM((STREAMS, A_WIDTH + B_WIDTH, LANES), F32),
        pltpu.VMEM((STREAMS, A_WIDTH, A_DV), F32),
        pltpu.VMEM((STREAMS, B_QK, B_DV), F32),
        pltpu.VMEM((STREAMS, SUBLANES, S5_DIM), F32),
        pltpu.VMEM((STREAMS, SUBLANES, S5_DIM), F32),
        pltpu.VMEM((3, A_WIDTH, LANES), F32),
    ]
    return pl.pallas_call(
        functools.partial(_prompt_kernel, layer, final),
        grid=(nb, nt),
        in_specs=[x_spec, pl.BlockSpec((STREAMS, LANES, D_MODEL), next_block)] + const_specs,
        out_specs=out_specs,
        out_shape=out_shape,
        scratch_shapes=scratch,
        compiler_params=pltpu.CompilerParams(
            dimension_semantics=("arbitrary", "arbitrary"), vmem_limit_bytes=VMEM_LIMIT),
    )(x, x, *(a for a, _ in consts))


def _sample_layers(xs, common, shg, sgla, sre, sim):
    n = xs.shape[0]
    depth = shg.shape[0]
    assert n == LANES
    hg_cols, gla_cols = A_DK * A_DV, B_DK * B_DV
    state_shapes = [s.shape for s in (shg, sgla, sre, sim)]
    out_shape = [jax.ShapeDtypeStruct((n, D_MODEL), F32)] + [jax.ShapeDtypeStruct(s, F32) for s in state_shapes]
    full = pl.BlockSpec((n, D_MODEL), lambda l, h: (0, 0))
    state_specs = [
        pl.BlockSpec((None, n, hg_cols), lambda l, h: (l, 0, h)),
        pl.BlockSpec((None, n, gla_cols), lambda l, h: (l, 0, h)),
        pl.BlockSpec((None, n, S5_DIM), lambda l, h: (l, 0, 0)),
        pl.BlockSpec((None, n, S5_DIM), lambda l, h: (l, 0, 0)),
    ]
    scratch = [
        pltpu.VMEM((n, D_MODEL), F32),
        pltpu.VMEM((n, IN_TOTAL), F32),
        pltpu.VMEM((A_CM, LANES), F32),
        pltpu.VMEM((B_CM, LANES), F32),
        pltpu.VMEM((A_WIDTH, LANES), F32),
        pltpu.VMEM((B_QK, LANES), F32),
        pltpu.VMEM((A_WIDTH + B_WIDTH, LANES), F32),
        pltpu.VMEM((hg_cols, LANES), F32),
        pltpu.VMEM((gla_cols, LANES), F32),
        pltpu.VMEM((3, A_WIDTH, LANES), F32),
    ]
    return pl.pallas_call(
        _sample_kernel,
        grid=(depth, A_HEADS),
        in_specs=([full] + [_layer_spec(a, None) if stacked else _const_spec(a) for a, stacked in common]
                  + state_specs),
        out_specs=[full] + state_specs,
        out_shape=out_shape,
        scratch_shapes=scratch,
        compiler_params=pltpu.CompilerParams(
            dimension_semantics=("arbitrary", "arbitrary"), vmem_limit_bytes=VMEM_LIMIT),
    )(xs, *(a for a, _ in common), shg, sgla, sre, sim)


def kernel(x_prompt, x_sample, state_hgrn, state_gla, state_s5_re, state_s5_im, norm_g, w_in, hg_lb_logits, hg_norm_g, gla_w2, gla_b2, gla_norm_g, s5_A_re, s5_A_im, s5_B_re, s5_B_im, s5_C_re, s5_C_im, s5_D, s5_log_dt, s5_w_glu, s5_b_glu, w_out, final_norm_g):
    depth = w_in.shape[0]
    bsz = x_prompt.shape[0]
    n = x_sample.shape[0]
    assert x_sample.shape[1] == 1 and A_HEADS == B_HEADS
    win_np, lv_np, perm_np = _chunk_tables()
    chunk_tables = (jnp.asarray(perm_np, BF16), jnp.asarray(win_np, BF16), jnp.asarray(lv_np))
    common, scan_tables = _operands(
        norm_g, w_in, hg_lb_logits, hg_norm_g, gla_w2, gla_b2, gla_norm_g,
        s5_A_re, s5_A_im, s5_B_re, s5_B_im, s5_C_re, s5_C_im, s5_D, s5_log_dt,
        s5_w_glu, s5_b_glu, w_out, final_norm_g)

    yp = x_prompt.astype(F32)
    prompt_states = []
    for l in range(depth):
        yp, *states = _prompt_layer(l, l == depth - 1, yp, common, scan_tables, chunk_tables)
        prompt_states.append(states)
    hg_p, gla_p, re_p, im_p = (jnp.stack([s[i] for s in prompt_states], axis=0) for i in range(4))

    flat = lambda s: s.astype(F32).reshape(depth, n, -1)
    ys, hg_s, gla_s, re_s, im_s = _sample_layers(
        x_sample.astype(F32).reshape(n, D_MODEL), common,
        flat(state_hgrn), flat(state_gla), flat(state_s5_re), flat(state_s5_im))

    return (yp, ys.reshape(n, 1, D_MODEL),
            hg_p, gla_p, re_p.reshape(depth, bsz, C_GROUPS, C_STATE), im_p.reshape(depth, bsz, C_GROUPS, C_STATE),
            hg_s.reshape(state_hgrn.shape), gla_s.reshape(state_gla.shape),
            re_s.reshape(state_s5_re.shape), im_s.reshape(state_s5_im.shape))
```

```python
import functools

import numpy as np
import jax
import jax.numpy as jnp
from jax import lax
from jax.experimental import pallas as pl
from jax.experimental.pallas import tpu as pltpu

F32 = jnp.float32
BF16 = jnp.bfloat16

D_MODEL = 1024
A_WIDTH, A_DK, A_DV, A_HEADS = 384, 64, 64, 6
B_WIDTH, B_DV, B_HEADS, B_DK, B_QK = 384, 64, 6, 32, 192
GLA_RANK = 16
GLA_TAU = 16.0
C_WIDTH, C_GROUP, C_GROUPS, C_STATE = 256, 16, 16, 64
S5_DIM = C_GROUPS * C_STATE
NORM_EPS = 1e-6
BLOCKS = ("qa", "fa", "ia", "za", "qb", "kb", "vb", "zb", "rb", "uc", "zc")
BLOCK_SIZES = (A_WIDTH, A_WIDTH, A_WIDTH, A_WIDTH, B_QK, B_QK, B_WIDTH, B_WIDTH, GLA_RANK, C_WIDTH, C_WIDTH)
IN_TOTAL = sum(BLOCK_SIZES)
_SIZE = dict(zip(BLOCKS, BLOCK_SIZES))
OFF = dict(zip(BLOCKS, np.concatenate([[0], np.cumsum(BLOCK_SIZES)[:-1]]).tolist()))
A_CM = 3 * A_WIDTH
B_CM = 2 * B_QK + B_WIDTH

LANES = 128
SUBLANES = 8
N_LEVELS = 7
N_WIN = N_LEVELS + 1
SEG = LANES // SUBLANES
STREAMS = 4
PROJ_COLS = 256
PROJ_GAP = 6
VMEM_LIMIT = 56 * 1024 * 1024


def _chunk_tables():
    s = np.arange(LANES)[:, None]
    t = np.arange(LANES)[None, :]
    wins = []
    for l in range(N_LEVELS):
        h = (LANES // 2) >> l
        mid = (t // (2 * h)) * (2 * h) + h - 1
        role_q = (t & h) != 0
        wins.append(np.where(role_q, (s > mid) & (s <= t), (s > t) & (s <= mid)))
    wins.append(s <= t)
    win = np.concatenate(wins, axis=1).astype(np.float32)
    x = s ^ t
    lv = np.full((LANES, LANES), -1, np.int32)
    for l in range(N_LEVELS):
        h = (LANES // 2) >> l
        lv[(x >= h) & (x < 2 * h) & (t > s)] = l
    perm = np.zeros((LANES, LANES), np.float32)
    i, j = np.divmod(np.arange(LANES), SEG)
    perm[SUBLANES * j + i, np.arange(LANES)] = 1.0
    return np.concatenate([win, win], axis=0), np.tile(lv, (1, 2)), np.stack([perm, perm.T])


def _dot(a, b):
    return jnp.dot(a, b, preferred_element_type=F32)


def _dot_nt(a, b):
    return lax.dot_general(a, b, (((1,), (1,)), ((), ())), preferred_element_type=F32)


def _rmsnorm_rows(x, g):
    r = lax.rsqrt(jnp.mean(x * x, axis=-1, keepdims=True) + NORM_EPS)
    return x * r * g


def _lower_bound_consts(lbl_ref, layer, out_ref):
    rows = [lbl_ref[l] for l in range(lbl_ref.shape[0])]
    m = functools.reduce(jnp.maximum, rows)
    es = [jnp.exp(r - m) for r in rows]
    tot = functools.reduce(lambda a, b: a + b, es)
    sm = [e / tot for e in es]
    cum, lbs = sm[0], [sm[0] - sm[0]]
    for l in range(1, len(sm)):
        cum = cum + sm[l]
        lbs.append(cum - sm[0])
    if isinstance(layer, int):
        lb = lbs[layer]
    else:
        lb = lbs[0]
        for l in range(1, len(lbs)):
            lb = jnp.where(layer == l, lbs[l], lb)
    out_ref[0] = jnp.log(lb)
    out_ref[1] = jnp.log(1.0 - lb)
    out_ref[2] = 1.0 - lb


def _hgrn_gates(fa, lbc_ref):
    e = jnp.exp(-jnp.abs(fa))
    logsig = jnp.minimum(fa, 0.0) - jnp.log(1.0 + e)
    a = lbc_ref[0]
    c = lbc_ref[1] + logsig
    log_f = jnp.maximum(a, c) + jnp.log(1.0 + jnp.exp(-jnp.abs(a - c)))
    k_a = lbc_ref[2] * (jnp.where(fa >= 0.0, e, 1.0) / (1.0 + e))
    return log_f, k_a


def _gla_log_gate_cm(rb, w2_ref, b2_ref):
    pre = _dot(rb.astype(BF16), w2_ref[...]) + b2_ref[...]
    return (jax.nn.log_sigmoid(pre) / GLA_TAU).T[:B_QK]


def _window_sums(g, win_ref):
    hi = g.astype(BF16)
    lo = (g - hi.astype(F32)).astype(BF16)
    return _dot(jnp.concatenate([hi, lo], axis=1), win_ref[...])


def _gated_chunk(load_q, load_k, load_v, e_ref, s_ref, heads, dk, dv, lv_ref, lane, emit):
    for p in range(heads // 2):
        rows = slice(2 * p * dk, (2 * p + 2) * dk)
        q, k = load_q(rows), load_k(rows)
        att = [jnp.zeros((SUBLANES, 2 * LANES), F32)] * (LANES // SUBLANES)
        for l in range(N_LEVELS):
            h = (LANES // 2) >> l
            z = jnp.where((lane & h) != 0, q, k) * jnp.exp(e_ref[rows, l * LANES:(l + 1) * LANES])
            zb = z.astype(BF16)
            zero = jnp.zeros((dk, LANES), BF16)
            rhs = jnp.concatenate([jnp.concatenate([zb[:dk], zero], axis=1),
                                   jnp.concatenate([zero, zb[dk:]], axis=1)], axis=0)
            zt = z.T
            blocks = [b for b in range(LANES // SUBLANES) if h < SUBLANES or (b * SUBLANES) & h == 0]
            lhs = jnp.concatenate([zt[b * SUBLANES:(b + 1) * SUBLANES] for b in blocks], axis=0)
            gram = _dot(lhs.astype(BF16), rhs)
            for i, b in enumerate(blocks):
                sl = slice(b * SUBLANES, (b + 1) * SUBLANES)
                att[b] = jnp.where(lv_ref[sl, :] == l, gram[i * SUBLANES:(i + 1) * SUBLANES], att[b])
            yield
        attb = jnp.concatenate(att, axis=0).astype(BF16)
        bq = e_ref[rows, N_LEVELS * LANES:(N_LEVELS + 1) * LANES]
        b_last = bq[:, LANES - 1:LANES]
        qs = (q * jnp.exp(bq)).astype(BF16)
        kd = (k * jnp.exp(b_last - bq)).astype(BF16)
        dec = jnp.exp(b_last)
        qk = q * k
        for i in range(2):
            hd = 2 * p + i
            r = slice(i * dk, (i + 1) * dk)
            rs = slice(hd * dk, (hd + 1) * dk)
            v = load_v(slice(hd * dv, (hd + 1) * dv))
            vb = v.astype(BF16)
            s_old = s_ref[rs, :]
            o = _dot(s_old.T.astype(BF16), qs[r])
            o = o + _dot(vb, attb[:, i * LANES:(i + 1) * LANES])
            o = o + jnp.sum(qk[r], axis=0, keepdims=True) * v
            s_ref[rs, :] = dec[r] * s_old + _dot_nt(kd[r], vb)
            emit(hd, o)
            yield


def _head_rms(o):
    return o * lax.rsqrt(jnp.mean(o * o, axis=0, keepdims=True) + NORM_EPS)


def _cmul_add(ar, ai, xr, xi, br, bi):
    return br + ar * xr - ai * xi, bi + ar * xi + ai * xr


def _s5_readout(x_re, x_im, cb_ref):
    return _dot(jnp.concatenate([x_re, x_im], axis=1).astype(BF16), cb_ref[...])


def _s5_gate_tokens(y, u, zc, d_ref, wglu_ref, bglu_ref):
    y = y + d_ref[...] * u
    y = jax.nn.gelu(y)
    y = y * jax.nn.sigmoid(_dot(y.astype(BF16), wglu_ref[...]) + bglu_ref[...])
    return y * jax.nn.silu(zc)


def _project(x, ng_ref, win_ref, p_ref):
    h = _rmsnorm_rows(x, ng_ref[...])
    p_ref[...] = _dot(h.astype(BF16), win_ref[...])


def _project_steps(x, ng_ref, win_ref, hb_ref, p_ref):
    hb_ref[...] = _rmsnorm_rows(x, ng_ref[...]).astype(BF16)
    for c0 in range(0, IN_TOTAL, PROJ_COLS):
        for _ in range(PROJ_GAP + 1):
            yield
        cols = slice(c0, min(c0 + PROJ_COLS, IN_TOTAL))
        p_ref[:, cols] = _dot(hb_ref[...], win_ref[:, cols])


def _mix_and_project(x, mixt_ref, za, zb, oc, hgain_ref, ggain_ref, wout_ref, fg_ref, final):
    o_t = mixt_ref[...].T
    o_a = o_t[:, :A_WIDTH] * hgain_ref[...] * jax.nn.silu(za)
    o_b = o_t[:, A_WIDTH:] * ggain_ref[...] * jax.nn.silu(zb)
    mix = jnp.concatenate([o_a, o_b, oc], axis=1).astype(BF16)
    out = x + _dot(mix, wout_ref[...])
    if final:
        out = _rmsnorm_rows(out, fg_ref[...])
    return out


def _cols(name):
    return slice(OFF[name], OFF[name] + _SIZE[name])


def _interleave(gens):
    live = list(gens)
    while live:
        live = [g for g in live if next(g, StopIteration) is not StopIteration]


def _prompt_kernel(layer, final,
                   x_ref, xn_ref, ng_ref, win_ref, lbl_ref, hgain_ref, w2_ref, b2_ref, ggain_ref,
                   bd_ref, cb_ref, dd_ref, wglu_ref, bglu_ref, wout_ref, fg_ref, aseg_ref,
                   tsub_ref, tcar_ref, tpos_ref, perm_ref, wwin_ref, lv_ref,
                   y_ref, shg_ref, sgla_ref, sre_ref, sim_ref,
                   p_ref, hb_ref, cma_ref, cmb_ref, work_ref, mixt_ref,
                   hg_s, gla_s, cre_s, cim_s, lbc_s):
    t_idx = pl.program_id(1)
    step = pl.program_id(0) * pl.num_programs(1) + t_idx
    cur = step % 2

    @pl.when(t_idx == 0)
    def _():
        hg_s[...] = jnp.zeros_like(hg_s)
        gla_s[...] = jnp.zeros_like(gla_s)
        cre_s[...] = jnp.zeros_like(cre_s)
        cim_s[...] = jnp.zeros_like(cim_s)
        _lower_bound_consts(lbl_ref, layer, lbc_s)

    @pl.when(step == 0)
    def _():
        _project(x_ref[...].reshape(STREAMS * LANES, D_MODEL), ng_ref, win_ref, p_ref.at[0])

    lane = lax.broadcasted_iota(jnp.int32, (1, LANES), 1)

    def chunk_steps(k):
        p_cur = p_ref.at[cur, k * LANES:(k + 1) * LANES]
        cma, cmb, mixt = (r.at[k] for r in (cma_ref, cmb_ref, mixt_ref))
        ea = work_ref.at[k]
        eb = work_ref.at[k, :B_QK]
        bu_re, bu_im = work_ref.at[k, :LANES], work_ref.at[k, LANES:2 * LANES]
        hg, gla, cre, cim = (r.at[k] for r in (hg_s, gla_s, cre_s, cim_s))
        cma[...] = p_cur[:, :A_CM].T
        cmb[...] = p_cur[:, OFF["qb"]:OFF["qb"] + B_CM].T
        yield

        log_f, k_a = _hgrn_gates(cma[A_WIDTH:2 * A_WIDTH, :], lbc_s)
        cma[A_WIDTH:2 * A_WIDTH, :] = k_a
        ea[...] = _window_sums(log_f, wwin_ref)
        yield

        def emit_a(hd, o):
            mixt[hd * A_DV:(hd + 1) * A_DV, :] = _head_rms(o)

        yield from _gated_chunk(lambda r: jax.nn.silu(cma[r, :]),
                                lambda r: cma[A_WIDTH + r.start:A_WIDTH + r.stop, :],
                                lambda r: cma[2 * A_WIDTH + r.start:2 * A_WIDTH + r.stop, :],
                                ea, hg, A_HEADS, A_DK, A_DV, lv_ref, lane, emit_a)

        eb[...] = _window_sums(_gla_log_gate_cm(p_cur[:, _cols("rb")], w2_ref, b2_ref), wwin_ref)
        yield

        def emit_b(hd, o):
            mixt[A_WIDTH + hd * B_DV:A_WIDTH + (hd + 1) * B_DV, :] = _head_rms(o)

        yield from _gated_chunk(lambda r: cmb[r, :] * (B_DK ** -0.5),
                                lambda r: cmb[B_QK + r.start:B_QK + r.stop, :],
                                lambda r: cmb[2 * B_QK + r.start:2 * B_QK + r.stop, :],
                                eb, gla, B_HEADS, B_DK, B_DV, lv_ref, lane, emit_b)

        u = p_cur[:, _cols("uc")]
        u_seg = _dot(perm_ref[0], u.astype(BF16)).astype(BF16)
        b = _dot(u_seg, bd_ref[...])
        bu_re[...] = b[:, :S5_DIM]
        bu_im[...] = b[:, S5_DIM:]
        yield
        ar, ai = aseg_ref[0], aseg_ref[1]
        xr = jnp.zeros((SUBLANES, S5_DIM), F32)
        xi = jnp.zeros((SUBLANES, S5_DIM), F32)
        for j in range(SEG):
            rows = slice(j * SUBLANES, (j + 1) * SUBLANES)
            xr, xi = _cmul_add(ar, ai, xr, xi, bu_re[rows, :], bu_im[rows, :])
            bu_re[rows, :] = xr
            bu_im[rows, :] = xi
            yield
        for i in range(3):
            xr, xi = _cmul_add(tsub_ref[i, 0], tsub_ref[i, 1],
                               pltpu.roll(xr, 1 << i, 0), pltpu.roll(xi, 1 << i, 0), xr, xi)
        cr, ci = cre[...], cim[...]
        xr, xi = _cmul_add(tcar_ref[0], tcar_ref[1], cr, ci, xr, xi)
        first = lax.broadcasted_iota(jnp.int32, (SUBLANES, S5_DIM), 0) == 0
        pr = jnp.where(first, cr, pltpu.roll(xr, 1, 0))
        pi = jnp.where(first, ci, pltpu.roll(xi, 1, 0))
        cre[...] = jnp.broadcast_to(xr[SUBLANES - 1:], (SUBLANES, S5_DIM))
        cim[...] = jnp.broadcast_to(xi[SUBLANES - 1:], (SUBLANES, S5_DIM))
        yield
        for j in range(SEG):
            rows = slice(j * SUBLANES, (j + 1) * SUBLANES)
            tr = jnp.broadcast_to(tpos_ref[0, j:j + 1, :], (SUBLANES, S5_DIM))
            ti = jnp.broadcast_to(tpos_ref[1, j:j + 1, :], (SUBLANES, S5_DIM))
            fr, fi = _cmul_add(tr, ti, pr, pi, bu_re[rows, :], bu_im[rows, :])
            bu_re[rows, :] = fr
            bu_im[rows, :] = fi
            yield
        y_seg = _s5_readout(bu_re[...], bu_im[...], cb_ref)
        y1 = y_seg.astype(BF16)
        r1 = y_seg - y1.astype(F32)
        y2 = r1.astype(BF16)
        y3 = (r1 - y2.astype(F32)).astype(BF16)
        yield
        yp = _dot(perm_ref[1], jnp.concatenate([y1, y2, y3], axis=1))
        y_tok = yp[:, :C_WIDTH] + yp[:, C_WIDTH:2 * C_WIDTH] + yp[:, 2 * C_WIDTH:]
        yield
        oc = _s5_gate_tokens(y_tok, u, p_cur[:, _cols("zc")], dd_ref, wglu_ref, bglu_ref)
        yield
        y_ref[k] = _mix_and_project(
            x_ref[k], mixt, p_cur[:, _cols("za")], p_cur[:, _cols("zb")], oc,
            hgain_ref, ggain_ref, wout_ref, fg_ref, final)

    project_next = _project_steps(xn_ref[...].reshape(STREAMS * LANES, D_MODEL), ng_ref, win_ref,
                                  hb_ref, p_ref.at[1 - cur])
    _interleave([chunk_steps(k) for k in range(STREAMS)] + [project_next])

    @pl.when(t_idx == pl.num_programs(1) - 1)
    def _():
        for k in range(STREAMS):
            for h in range(A_HEADS):
                shg_ref[k, h] = hg_s[k, h * A_DK:(h + 1) * A_DK, :]
            for h in range(B_HEADS):
                sgla_ref[k, h] = gla_s[k, h * B_DK:(h + 1) * B_DK, :]
            sre_ref[k] = cre_s[k, :1, :]
            sim_ref[k] = cim_s[k, :1, :]


def _sample_kernel(x_ref, ng_ref, win_ref, lbl_ref, hgain_ref, w2_ref, b2_ref, ggain_ref,
                   bd_ref, cb_ref, dd_ref, wglu_ref, bglu_ref, wout_ref, fg_ref, aseg_ref,
                   shg_in, sgla_in, sre_in, sim_in,
                   y_ref, shg_out, sgla_out, sre_out, sim_out,
                   xs_s, p_ref, cma_ref, cmb_ref, fa_s, gb_s, mixt_ref, sa_s, sb_s, lbc_s):
    l = pl.program_id(0)
    h = pl.program_id(1)

    @pl.when((l == 0) & (h == 0))
    def _():
        xs_s[...] = x_ref[...]

    @pl.when(h == 0)
    def _():
        _lower_bound_consts(lbl_ref, l, lbc_s)
        _project(xs_s[...], ng_ref, win_ref, p_ref)
        cma_ref[...] = p_ref[:, :A_CM].T
        cmb_ref[...] = p_ref[:, OFF["qb"]:OFF["qb"] + B_CM].T
        log_f, k_a = _hgrn_gates(cma_ref[A_WIDTH:2 * A_WIDTH, :], lbc_s)
        fa_s[...] = jnp.exp(log_f)
        cma_ref[A_WIDTH:2 * A_WIDTH, :] = k_a
        cma_ref[:A_WIDTH, :] = jax.nn.silu(cma_ref[:A_WIDTH, :])
        gb_s[...] = jnp.exp(_gla_log_gate_cm(p_ref[:, _cols("rb")], w2_ref, b2_ref))

    sa_s[...] = shg_in[...].T
    ra = pl.multiple_of(h * A_DK, A_DK)
    v_a = cma_ref[pl.ds(2 * A_WIDTH + ra, A_DV), :]

    def hg_body(k, o):
        r = ra + k
        rows = pl.ds(pl.multiple_of(k * A_DV, A_DV), A_DV)
        new = fa_s[pl.ds(r, 1), :] * sa_s[rows, :] + cma_ref[pl.ds(A_WIDTH + r, 1), :] * v_a
        sa_s[rows, :] = new
        return o + cma_ref[pl.ds(r, 1), :] * new

    o = lax.fori_loop(0, A_DK, hg_body, jnp.zeros((A_DV, LANES), F32))
    mixt_ref[pl.ds(ra, A_DV), :] = _head_rms(o)
    shg_out[...] = sa_s[...].T

    sb_s[...] = sgla_in[...].T
    rb = pl.multiple_of(h * B_DK, B_DK)
    vb0 = pl.multiple_of(h * B_DV, B_DV)
    v_b = cmb_ref[pl.ds(2 * B_QK + vb0, B_DV), :]

    def gla_body(k, o):
        r = rb + k
        rows = pl.ds(pl.multiple_of(k * B_DV, B_DV), B_DV)
        new = gb_s[pl.ds(r, 1), :] * sb_s[rows, :] + cmb_ref[pl.ds(B_QK + r, 1), :] * v_b
        sb_s[rows, :] = new
        return o + (cmb_ref[pl.ds(r, 1), :] * (B_DK ** -0.5)) * new

    o = lax.fori_loop(0, B_DK, gla_body, jnp.zeros((B_DV, LANES), F32))
    mixt_ref[pl.ds(A_WIDTH + vb0, B_DV), :] = _head_rms(o)
    sgla_out[...] = sb_s[...].T

    @pl.when(h == pl.num_programs(1) - 1)
    def _():
        u = p_ref[:, _cols("uc")]
        bu = _dot(u.astype(BF16), bd_ref[...])
        x_re, x_im = _cmul_add(aseg_ref[0, :1], aseg_ref[1, :1], sre_in[...], sim_in[...],
                               bu[:, :S5_DIM], bu[:, S5_DIM:])
        sre_out[...] = x_re
        sim_out[...] = x_im
        oc = _s5_gate_tokens(_s5_readout(x_re, x_im, cb_ref), u, p_ref[:, _cols("zc")],
                             dd_ref, wglu_ref, bglu_ref)
        out = _mix_and_project(
            xs_s[...], mixt_ref, p_ref[:, _cols("za")], p_ref[:, _cols("zb")], oc,
            hgain_ref, ggain_ref, wout_ref, fg_ref, False)
        xs_s[...] = out

        @pl.when(l == pl.num_programs(0) - 1)
        def _():
            y_ref[...] = _rmsnorm_rows(out, fg_ref[...])


def _lane_bcast(v):
    return jnp.broadcast_to(v.astype(F32)[..., None], v.shape + (LANES,))


def _row(v):
    return v.astype(F32).reshape(1, -1)


def _group_block_diag(m):
    g, r, c = m.shape
    eye = jnp.eye(g, dtype=m.dtype)
    return (m[:, :, None, :] * eye[:, None, :, None]).reshape(g * r, g * c)


def _s5_tables(A_re, A_im, B_re, B_im, C_re, C_im, log_dt):
    dt = jnp.exp(log_dt.astype(F32))[:, None]
    lr, li = A_re.astype(F32), A_im.astype(F32)
    mag = jnp.exp(lr * dt)
    ab_re = mag * jnp.cos(li * dt)
    ab_im = mag * jnp.sin(li * dt)
    den = lr * lr + li * li
    nr = ab_re - 1.0
    co_re = (nr * lr + ab_im * li) / den
    co_im = (ab_im * lr - nr * li) / den
    Br, Bi = B_re.astype(F32), B_im.astype(F32)
    bb_re = co_re[..., None] * Br - co_im[..., None] * Bi
    bb_im = co_re[..., None] * Bi + co_im[..., None] * Br
    bd = jnp.concatenate([_group_block_diag(bb_re.transpose(0, 2, 1)),
                          _group_block_diag(bb_im.transpose(0, 2, 1))], axis=1).astype(BF16)
    cb = jnp.concatenate([_group_block_diag(C_re.astype(F32).transpose(0, 2, 1)),
                          _group_block_diag(-C_im.astype(F32).transpose(0, 2, 1))], axis=0).astype(BF16)
    ns = np.concatenate([np.arange(1, SEG + 1), np.arange(2 * SEG, LANES + 1, SEG)]).astype(np.float32)
    mg = jnp.exp(ns[:, None] * (lr * dt).reshape(-1)[None, :])
    ang = ns[:, None] * (li * dt).reshape(-1)[None, :]
    pw = jnp.stack([mg * jnp.cos(ang), mg * jnp.sin(ang)])
    aseg = jnp.broadcast_to(pw[:, :1], (2, SUBLANES, S5_DIM))
    sub = jnp.arange(SUBLANES)[None, :, None]
    row_of = {int(n): r for r, n in enumerate(ns)}
    tsub = jnp.stack([jnp.where(sub >= (1 << i), pw[:, row_of[SEG << i]][:, None, :], 0.0) for i in range(3)])
    tcar = pw[:, SEG - 1:]
    tpos = pw[:, :SEG]
    return bd, cb, aseg, tsub, tcar, tpos


def _layer_spec(a, layer):
    nd = a.ndim - 1
    if layer is None:
        return pl.BlockSpec((None,) + a.shape[1:], lambda l, *_, nd=nd: (l,) + (0,) * nd)
    return pl.BlockSpec((None,) + a.shape[1:], lambda *_, nd=nd: (layer,) + (0,) * nd)


def _const_spec(a):
    nd = a.ndim
    return pl.BlockSpec(a.shape, lambda *_, nd=nd: (0,) * nd)


def _operands(norm_g, w_in, hg_lb_logits, hg_norm_g, gla_w2, gla_b2, gla_norm_g,
              s5_A_re, s5_A_im, s5_B_re, s5_B_im, s5_C_re, s5_C_im, s5_D, s5_log_dt,
              s5_w_glu, s5_b_glu, w_out, final_norm_g):
    bd, cb, aseg, tsub, tcar, tpos = jax.vmap(_s5_tables)(
        s5_A_re, s5_A_im, s5_B_re, s5_B_im, s5_C_re, s5_C_im, s5_log_dt)
    pad = 2 * LANES - B_QK
    rows = lambda v: v.astype(F32)[:, None, :]
    common = (
        (rows(norm_g), True),
        (w_in.astype(BF16), True),
        (_lane_bcast(hg_lb_logits), False),
        (rows(hg_norm_g), True),
        (jnp.pad(gla_w2, ((0, 0), (0, 0), (0, pad))).astype(BF16), True),
        (jnp.pad(rows(gla_b2), ((0, 0), (0, 0), (0, pad))), True),
        (rows(gla_norm_g), True),
        (bd, True), (cb, True),
        (rows(s5_D), True),
        (s5_w_glu.astype(BF16), True),
        (rows(s5_b_glu), True),
        (w_out.astype(BF16), True),
        (_row(final_norm_g), False),
        (aseg, True),
    )
    return common, ((tsub, True), (tcar, True), (tpos, True))


def _prompt_layer(layer, final, x, common, scan_tables, chunk_tables):
    bsz, t, _ = x.shape
    assert t % LANES == 0 and bsz % STREAMS == 0
    nb, nt = bsz // STREAMS, t // LANES
    consts = common + scan_tables + tuple((a, False) for a in chunk_tables)
    const_specs = [_layer_spec(a, layer) if stacked else _const_spec(a) for a, stacked in consts]
    out_shape = (
        jax.ShapeDtypeStruct((bsz, t, D_MODEL), F32),
        jax.ShapeDtypeStruct((bsz, A_HEADS, A_DK, A_DV), F32),
        jax.ShapeDtypeStruct((bsz, B_HEADS, B_DK, B_DV), F32),
        jax.ShapeDtypeStruct((bsz, 1, S5_DIM), F32),
        jax.ShapeDtypeStruct((bsz, 1, S5_DIM), F32),
    )
    x_spec = pl.BlockSpec((STREAMS, LANES, D_MODEL), lambda b, i: (b, i, 0))

    def next_block(b, i):
        nxt = jnp.minimum(b * nt + i + 1, nb * nt - 1)
        return (nxt // nt, nxt % nt, 0)

    out_specs = (
        x_spec,
        pl.BlockSpec((STREAMS, A_HEADS, A_DK, A_DV), lambda b, i: (b, 0, 0, 0)),
        pl.BlockSpec((STREAMS, B_HEADS, B_DK, B_DV), lambda b, i: (b, 0, 0, 0)),
        pl.BlockSpec((STREAMS, 1, S5_DIM), lambda b, i: (b, 0, 0)),
        pl.BlockSpec((STREAMS, 1, S5_DIM), lambda b, i: (b, 0, 0)),
    )
    scratch = [
        pltpu.VMEM((2, STREAMS * LANES, IN_TOTAL), F32),
        pltpu.VMEM((STREAMS * LANES, D_MODEL), BF16),
        pltpu.VMEM((STREAMS, A_CM, LANES), F32),
        pltpu.VMEM((STREAMS, B_CM, LANES), F32),
        pltpu.VMEM((STREAMS, A_WIDTH, N_WIN * LANES), F32),
        pltpu.VMEM((STREAMS, A_WIDTH + B_WIDTH, LANES), F32),
        pltpu.VMEM((STREAMS, A_WIDTH, A_DV), F32),
        pltpu.VMEM((STREAMS, B_QK, B_DV), F32),
        pltpu.VMEM((STREAMS, SUBLANES, S5_DIM), F32),
        pltpu.VMEM((STREAMS, SUBLANES, S5_DIM), F32),
        pltpu.VMEM((3, A_WIDTH, LANES), F32),
    ]
    return pl.pallas_call(
        functools.partial(_prompt_kernel, layer, final),
        grid=(nb, nt),
        in_specs=[x_spec, pl.BlockSpec((STREAMS, LANES, D_MODEL), next_block)] + const_specs,
        out_specs=out_specs,
        out_shape=out_shape,
        scratch_shapes=scratch,
        compiler_params=pltpu.CompilerParams(
            dimension_semantics=("arbitrary", "arbitrary"), vmem_limit_bytes=VMEM_LIMIT),
    )(x, x, *(a for a, _ in consts))


def _sample_layers(xs, common, shg, sgla, sre, sim):
    n = xs.shape[0]
    depth = shg.shape[0]
    assert n == LANES
    hg_cols, gla_cols = A_DK * A_DV, B_DK * B_DV
    state_shapes = [s.shape for s in (shg, sgla, sre, sim)]
    out_shape = [jax.ShapeDtypeStruct((n, D_MODEL), F32)] + [jax.ShapeDtypeStruct(s, F32) for s in state_shapes]
    full = pl.BlockSpec((n, D_MODEL), lambda l, h: (0, 0))
    state_specs = [
        pl.BlockSpec((None, n, hg_cols), lambda l, h: (l, 0, h)),
        pl.BlockSpec((None, n, gla_cols), lambda l, h: (l, 0, h)),
        pl.BlockSpec((None, n, S5_DIM), lambda l, h: (l, 0, 0)),
        pl.BlockSpec((None, n, S5_DIM), lambda l, h: (l, 0, 0)),
    ]
    scratch = [
        pltpu.VMEM((n, D_MODEL), F32),
        pltpu.VMEM((n, IN_TOTAL), F32),
        pltpu.VMEM((A_CM, LANES), F32),
        pltpu.VMEM((B_CM, LANES), F32),
        pltpu.VMEM((A_WIDTH, LANES), F32),
        pltpu.VMEM((B_QK, LANES), F32),
        pltpu.VMEM((A_WIDTH + B_WIDTH, LANES), F32),
        pltpu.VMEM((hg_cols, LANES), F32),
        pltpu.VMEM((gla_cols, LANES), F32),
        pltpu.VMEM((3, A_WIDTH, LANES), F32),
    ]
    return pl.pallas_call(
        _sample_kernel,
        grid=(depth, A_HEADS),
        in_specs=([full] + [_layer_spec(a, None) if stacked else _const_spec(a) for a, stacked in common]
                  + state_specs),
        out_specs=[full] + state_specs,
        out_shape=out_shape,
        scratch_shapes=scratch,
        compiler_params=pltpu.CompilerParams(
            dimension_semantics=("arbitrary", "arbitrary"), vmem_limit_bytes=VMEM_LIMIT),
    )(xs, *(a for a, _ in common), shg, sgla, sre, sim)


def kernel(x_prompt, x_sample, state_hgrn, state_gla, state_s5_re, state_s5_im, norm_g, w_in, hg_lb_logits, hg_norm_g, gla_w2, gla_b2, gla_norm_g, s5_A_re, s5_A_im, s5_B_re, s5_B_im, s5_C_re, s5_C_im, s5_D, s5_log_dt, s5_w_glu, s5_b_glu, w_out, final_norm_g):
    depth = w_in.shape[0]
    bsz = x_prompt.shape[0]
    n = x_sample.shape[0]
    assert x_sample.shape[1] == 1 and A_HEADS == B_HEADS
    win_np, lv_np, perm_np = _chunk_tables()
    chunk_tables = (jnp.asarray(perm_np, BF16), jnp.asarray(win_np, BF16), jnp.asarray(lv_np))
    common, scan_tables = _operands(
        norm_g, w_in, hg_lb_logits, hg_norm_g, gla_w2, gla_b2, gla_norm_g,
        s5_A_re, s5_A_im, s5_B_re, s5_B_im, s5_C_re, s5_C_im, s5_D, s5_log_dt,
        s5_w_glu, s5_b_glu, w_out, final_norm_g)

    yp = x_prompt.astype(F32)
    prompt_states = []
    for l in range(depth):
        yp, *states = _prompt_layer(l, l == depth - 1, yp, common, scan_tables, chunk_tables)
        prompt_states.append(states)
    hg_p, gla_p, re_p, im_p = (jnp.stack([s[i] for s in prompt_states], axis=0) for i in range(4))

    flat = lambda s: s.astype(F32).reshape(depth, n, -1)
    ys, hg_s, gla_s, re_s, im_s = _sample_layers(
        x_sample.astype(F32).reshape(n, D_MODEL), common,
        flat(state_hgrn), flat(state_gla), flat(state_s5_re), flat(state_s5_im))

    return (yp, ys.reshape(n, 1, D_MODEL),
            hg_p, gla_p, re_p.reshape(depth, bsz, C_GROUPS, C_STATE), im_p.reshape(depth, bsz, C_GROUPS, C_STATE),
            hg_s.reshape(state_hgrn.shape), gla_s.reshape(state_gla.shape),
            re_s.reshape(state_s5_re.shape), im_s.reshape(state_s5_im.shape))
```

```python
import functools

import numpy as np
import jax
import jax.numpy as jnp
from jax import lax
from jax.experimental import pallas as pl
from jax.experimental.pallas import tpu as pltpu

F32 = jnp.float32
BF16 = jnp.bfloat16

D_MODEL = 1024
A_WIDTH, A_DK, A_DV, A_HEADS = 384, 64, 64, 6
B_WIDTH, B_DV, B_HEADS, B_DK, B_QK = 384, 64, 6, 32, 192
GLA_RANK = 16
GLA_TAU = 16.0
C_WIDTH, C_GROUP, C_GROUPS, C_STATE = 256, 16, 16, 64
S5_DIM = C_GROUPS * C_STATE
NORM_EPS = 1e-6
BLOCKS = ("qa", "fa", "ia", "za", "qb", "kb", "vb", "zb", "rb", "uc", "zc")
BLOCK_SIZES = (A_WIDTH, A_WIDTH, A_WIDTH, A_WIDTH, B_QK, B_QK, B_WIDTH, B_WIDTH, GLA_RANK, C_WIDTH, C_WIDTH)
IN_TOTAL = sum(BLOCK_SIZES)
_SIZE = dict(zip(BLOCKS, BLOCK_SIZES))
OFF = dict(zip(BLOCKS, np.concatenate([[0], np.cumsum(BLOCK_SIZES)[:-1]]).tolist()))
A_CM = 3 * A_WIDTH
B_CM = 2 * B_QK + B_WIDTH

LANES = 128
SUBLANES = 8
N_LEVELS = 7
N_WIN = N_LEVELS + 1
SEG = LANES // SUBLANES
STREAMS = 4
PROJ_COLS = 256
PROJ_GAP = 6
VMEM_LIMIT = 56 * 1024 * 1024


def _chunk_tables():
    s = np.arange(LANES)[:, None]
    t = np.arange(LANES)[None, :]
    wins = []
    for l in range(N_LEVELS):
        h = (LANES // 2) >> l
        mid = (t // (2 * h)) * (2 * h) + h - 1
        role_q = (t & h) != 0
        wins.append(np.where(role_q, (s > mid) & (s <= t), (s > t) & (s <= mid)))
    wins.append(s <= t)
    win = np.concatenate(wins, axis=1).astype(np.float32)
    x = s ^ t
    lv = np.full((LANES, LANES), -1, np.int32)
    for l in range(N_LEVELS):
        h = (LANES // 2) >> l
        lv[(x >= h) & (x < 2 * h) & (t > s)] = l
    perm = np.zeros((LANES, LANES), np.float32)
    i, j = np.divmod(np.arange(LANES), SEG)
    perm[SUBLANES * j + i, np.arange(LANES)] = 1.0
    return np.concatenate([win, win], axis=0), np.tile(lv, (1, 2)), np.stack([perm, perm.T])


def _dot(a, b):
    return jnp.dot(a, b, preferred_element_type=F32)


def _dot_nt(a, b):
    return lax.dot_general(a, b, (((1,), (1,)), ((), ())), preferred_element_type=F32)


def _rmsnorm_rows(x, g):
    r = lax.rsqrt(jnp.mean(x * x, axis=-1, keepdims=True) + NORM_EPS)
    return x * r * g


def _lower_bound_consts(lbl_ref, layer, out_ref):
    rows = [lbl_ref[l] for l in range(lbl_ref.shape[0])]
    m = functools.reduce(jnp.maximum, rows)
    es = [jnp.exp(r - m) for r in rows]
    tot = functools.reduce(lambda a, b: a + b, es)
    sm = [e / tot for e in es]
    cum, lbs = sm[0], [sm[0] - sm[0]]
    for l in range(1, len(sm)):
        cum = cum + sm[l]
        lbs.append(cum - sm[0])
    if isinstance(layer, int):
        lb = lbs[layer]
    else:
        lb = lbs[0]
        for l in range(1, len(lbs)):
            lb = jnp.where(layer == l, lbs[l], lb)
    out_ref[0] = jnp.log(lb)
    out_ref[1] = jnp.log(1.0 - lb)
    out_ref[2] = 1.0 - lb


def _hgrn_gates(fa, lbc_ref):
    e = jnp.exp(-jnp.abs(fa))
    logsig = jnp.minimum(fa, 0.0) - jnp.log(1.0 + e)
    a = lbc_ref[0]
    c = lbc_ref[1] + logsig
    log_f = jnp.maximum(a, c) + jnp.log(1.0 + jnp.exp(-jnp.abs(a - c)))
    k_a = lbc_ref[2] * (jnp.where(fa >= 0.0, e, 1.0) / (1.0 + e))
    return log_f, k_a


def _gla_log_gate_cm(rb, w2_ref, b2_ref):
    pre = _dot(rb.astype(BF16), w2_ref[...]) + b2_ref[...]
    return (jax.nn.log_sigmoid(pre) / GLA_TAU).T[:B_QK]


def _window_sums(g, win_ref):
    hi = g.astype(BF16)
    lo = (g - hi.astype(F32)).astype(BF16)
    return _dot(jnp.concatenate([hi, lo], axis=1), win_ref[...])


def _gated_chunk(load_q, load_k, load_v, e_ref, s_ref, heads, dk, dv, lv_ref, lane, emit):
    for p in range(heads // 2):
        rows = slice(2 * p * dk, (2 * p + 2) * dk)
        q, k = load_q(rows), load_k(rows)
        att = [jnp.zeros((SUBLANES, 2 * LANES), F32)] * (LANES // SUBLANES)
        for l in range(N_LEVELS):
            h = (LANES // 2) >> l
            z = jnp.where((lane & h) != 0, q, k) * jnp.exp(e_ref[rows, l * LANES:(l + 1) * LANES])
            zb = z.astype(BF16)
            zero = jnp.zeros((dk, LANES), BF16)
            rhs = jnp.concatenate([jnp.concatenate([zb[:dk], zero], axis=1),
                                   jnp.concatenate([zero, zb[dk:]], axis=1)], axis=0)
            zt = z.T
            blocks = [b for b in range(LANES // SUBLANES) if h < SUBLANES or (b * SUBLANES) & h == 0]
            lhs = jnp.concatenate([zt[b * SUBLANES:(b + 1) * SUBLANES] for b in blocks], axis=0)
            gram = _dot(lhs.astype(BF16), rhs)
            for i, b in enumerate(blocks):
                sl = slice(b * SUBLANES, (b + 1) * SUBLANES)
                att[b] = jnp.where(lv_ref[sl, :] == l, gram[i * SUBLANES:(i + 1) * SUBLANES], att[b])
            yield
        attb = jnp.concatenate(att, axis=0).astype(BF16)
        bq = e_ref[rows, N_LEVELS * LANES:(N_LEVELS + 1) * LANES]
        b_last = bq[:, LANES - 1:LANES]
        qs = (q * jnp.exp(bq)).astype(BF16)
        kd = (k * jnp.exp(b_last - bq)).astype(BF16)
        dec = jnp.exp(b_last)
        qk = q * k
        for i in range(2):
            hd = 2 * p + i
            r = slice(i * dk, (i + 1) * dk)
            rs = slice(hd * dk, (hd + 1) * dk)
            v = load_v(slice(hd * dv, (hd + 1) * dv))
            vb = v.astype(BF16)
            s_old = s_ref[rs, :]
            o = _dot(s_old.T.astype(BF16), qs[r])
            o = o + _dot(vb, attb[:, i * LANES:(i + 1) * LANES])
            o = o + jnp.sum(qk[r], axis=0, keepdims=True) * v
            s_ref[rs, :] = dec[r] * s_old + _dot_nt(kd[r], vb)
            emit(hd, o)
            yield


def _head_rms(o):
    return o * lax.rsqrt(jnp.mean(o * o, axis=0, keepdims=True) + NORM_EPS)


def _cmul_add(ar, ai, xr, xi, br, bi):
    return br + ar * xr - ai * xi, bi + ar * xi + ai * xr


def _s5_readout(x_re, x_im, cb_ref):
    return _dot(jnp.concatenate([x_re, x_im], axis=1).astype(BF16), cb_ref[...])


def _s5_gate_tokens(y, u, zc, d_ref, wglu_ref, bglu_ref):
    y = y + d_ref[...] * u
    y = jax.nn.gelu(y)
    y = y * jax.nn.sigmoid(_dot(y.astype(BF16), wglu_ref[...]) + bglu_ref[...])
    return y * jax.nn.silu(zc)


def _project(x, ng_ref, win_ref, p_ref):
    h = _rmsnorm_rows(x, ng_ref[...])
    p_ref[...] = _dot(h.astype(BF16), win_ref[...])


def _project_steps(x, ng_ref, win_ref, hb_ref, p_ref):
    hb_ref[...] = _rmsnorm_rows(x, ng_ref[...]).astype(BF16)
    for c0 in range(0, IN_TOTAL, PROJ_COLS):
        for _ in range(PROJ_GAP + 1):
            yield
        cols = slice(c0, min(c0 + PROJ_COLS, IN_TOTAL))
        p_ref[:, cols] = _dot(hb_ref[...], win_ref[:, cols])


def _mix_and_project(x, mixt_ref, za, zb, oc, hgain_ref, ggain_ref, wout_ref, fg_ref, final):
    o_t = mixt_ref[...].T
    o_a = o_t[:, :A_WIDTH] * hgain_ref[...] * jax.nn.silu(za)
    o_b = o_t[:, A_WIDTH:] * ggain_ref[...] * jax.nn.silu(zb)
    mix = jnp.concatenate([o_a, o_b, oc], axis=1).astype(BF16)
    out = x + _dot(mix, wout_ref[...])
    if final:
        out = _rmsnorm_rows(out, fg_ref[...])
    return out


def _cols(name):
    return slice(OFF[name], OFF[name] + _SIZE[name])


def _interleave(gens):
    live = list(gens)
    while live:
        live = [g for g in live if next(g, StopIteration) is not StopIteration]


def _prompt_kernel(layer, final,
                   x_ref, xn_ref, ng_ref, win_ref, lbl_ref, hgain_ref, w2_ref, b2_ref, ggain_ref,
                   bd_ref, cb_ref, dd_ref, wglu_ref, bglu_ref, wout_ref, fg_ref, aseg_ref,
                   tsub_ref, tcar_ref, tpos_ref, perm_ref, wwin_ref, lv_ref,
                   y_ref, shg_ref, sgla_ref, sre_ref, sim_ref,
                   p_ref, hb_ref, cma_ref, cmb_ref, work_ref, mixt_ref,
                   hg_s, gla_s, cre_s, cim_s, lbc_s):
    t_idx = pl.program_id(1)
    step = pl.program_id(0) * pl.num_programs(1) + t_idx
    cur = step % 2

    @pl.when(t_idx == 0)
    def _():
        hg_s[...] = jnp.zeros_like(hg_s)
        gla_s[...] = jnp.zeros_like(gla_s)
        cre_s[...] = jnp.zeros_like(cre_s)
        cim_s[...] = jnp.zeros_like(cim_s)
        _lower_bound_consts(lbl_ref, layer, lbc_s)

    @pl.when(step == 0)
    def _():
        _project(x_ref[...].reshape(STREAMS * LANES, D_MODEL), ng_ref, win_ref, p_ref.at[0])

    lane = lax.broadcasted_iota(jnp.int32, (1, LANES), 1)

    def chunk_steps(k):
        p_cur = p_ref.at[cur, k * LANES:(k + 1) * LANES]
        cma, cmb, mixt = (r.at[k] for r in (cma_ref, cmb_ref, mixt_ref))
        ea = work_ref.at[k]
        eb = work_ref.at[k, :B_QK]
        bu_re, bu_im = work_ref.at[k, :LANES], work_ref.at[k, LANES:2 * LANES]
        hg, gla, cre, cim = (r.at[k] for r in (hg_s, gla_s, cre_s, cim_s))
        cma[...] = p_cur[:, :A_CM].T
        cmb[...] = p_cur[:, OFF["qb"]:OFF["qb"] + B_CM].T
        yield

        log_f, k_a = _hgrn_gates(cma[A_WIDTH:2 * A_WIDTH, :], lbc_s)
        cma[A_WIDTH:2 * A_WIDTH, :] = k_a
        ea[...] = _window_sums(log_f, wwin_ref)
        yield

        def emit_a(hd, o):
            mixt[hd * A_DV:(hd + 1) * A_DV, :] = _head_rms(o)

        yield from _gated_chunk(lambda r: jax.nn.silu(cma[r, :]),
                                lambda r: cma[A_WIDTH + r.start:A_WIDTH + r.stop, :],
                                lambda r: cma[2 * A_WIDTH + r.start:2 * A_WIDTH + r.stop, :],
                                ea, hg, A_HEADS, A_DK, A_DV, lv_ref, lane, emit_a)

        eb[...] = _window_sums(_gla_log_gate_cm(p_cur[:, _cols("rb")], w2_ref, b2_ref), wwin_ref)
        yield

        def emit_b(hd, o):
            mixt[A_WIDTH + hd * B_DV:A_WIDTH + (hd + 1) * B_DV, :] = _head_rms(o)

        yield from _gated_chunk(lambda r: cmb[r, :] * (B_DK ** -0.5),
                                lambda r: cmb[B_QK + r.start:B_QK + r.stop, :],
                                lambda r: cmb[2 * B_QK + r.start:2 * B_QK + r.stop, :],
                                eb, gla, B_HEADS, B_DK, B_DV, lv_ref, lane, emit_b)

        u = p_cur[:, _cols("uc")]
        u_seg = _dot(perm_ref[0], u.astype(BF16)).astype(BF16)
        b = _dot(u_seg, bd_ref[...])
        bu_re[...] = b[:, :S5_DIM]
        bu_im[...] = b[:, S5_DIM:]
        yield
        ar, ai = aseg_ref[0], aseg_ref[1]
        xr = jnp.zeros((SUBLANES, S5_DIM), F32)
        xi = jnp.zeros((SUBLANES, S5_DIM), F32)
        for j in range(SEG):
            rows = slice(j * SUBLANES, (j + 1) * SUBLANES)
            xr, xi = _cmul_add(ar, ai, xr, xi, bu_re[rows, :], bu_im[rows, :])
            bu_re[rows, :] = xr
            bu_im[rows, :] = xi
            yield
        for i in range(3):
            xr, xi = _cmul_add(tsub_ref[i, 0], tsub_ref[i, 1],
                               pltpu.roll(xr, 1 << i, 0), pltpu.roll(xi, 1 << i, 0), xr, xi)
        cr, ci = cre[...], cim[...]
        xr, xi = _cmul_add(tcar_ref[0], tcar_ref[1], cr, ci, xr, xi)
        first = lax.broadcasted_iota(jnp.int32, (SUBLANES, S5_DIM), 0) == 0
        pr = jnp.where(first, cr, pltpu.roll(xr, 1, 0))
        pi = jnp.where(first, ci, pltpu.roll(xi, 1, 0))
        cre[...] = jnp.broadcast_to(xr[SUBLANES - 1:], (SUBLANES, S5_DIM))
        cim[...] = jnp.broadcast_to(xi[SUBLANES - 1:], (SUBLANES, S5_DIM))
        yield
        for j in range(SEG):
            rows = slice(j * SUBLANES, (j + 1) * SUBLANES)
            tr, ti = tpos_ref[0, rows, :], tpos_ref[1, rows, :]
            fr, fi = _cmul_add(tr, ti, pr, pi, bu_re[rows, :], bu_im[rows, :])
            bu_re[rows, :] = fr
            bu_im[rows, :] = fi
            yield
        y_seg = _s5_readout(bu_re[...], bu_im[...], cb_ref)
        y1 = y_seg.astype(BF16)
        r1 = y_seg - y1.astype(F32)
        y2 = r1.astype(BF16)
        y3 = (r1 - y2.astype(F32)).astype(BF16)
        yield
        yp = _dot(perm_ref[1], jnp.concatenate([y1, y2, y3], axis=1))
        y_tok = yp[:, :C_WIDTH] + yp[:, C_WIDTH:2 * C_WIDTH] + yp[:, 2 * C_WIDTH:]
        yield
        oc = _s5_gate_tokens(y_tok, u, p_cur[:, _cols("zc")], dd_ref, wglu_ref, bglu_ref)
        yield
        y_ref[k] = _mix_and_project(
            x_ref[k], mixt, p_cur[:, _cols("za")], p_cur[:, _cols("zb")], oc,
            hgain_ref, ggain_ref, wout_ref, fg_ref, final)

    project_next = _project_steps(xn_ref[...].reshape(STREAMS * LANES, D_MODEL), ng_ref, win_ref,
                                  hb_ref, p_ref.at[1 - cur])
    _interleave([chunk_steps(k) for k in range(STREAMS)] + [project_next])

    @pl.when(t_idx == pl.num_programs(1) - 1)
    def _():
        for k in range(STREAMS):
            for h in range(A_HEADS):
                shg_ref[k, h] = hg_s[k, h * A_DK:(h + 1) * A_DK, :]
            for h in range(B_HEADS):
                sgla_ref[k, h] = gla_s[k, h * B_DK:(h + 1) * B_DK, :]
            sre_ref[k] = cre_s[k, :1, :]
            sim_ref[k] = cim_s[k, :1, :]


def _sample_kernel(x_ref, ng_ref, win_ref, lbl_ref, hgain_ref, w2_ref, b2_ref, ggain_ref,
                   bd_ref, cb_ref, dd_ref, wglu_ref, bglu_ref, wout_ref, fg_ref, aseg_ref,
                   shg_in, sgla_in, sre_in, sim_in,
                   y_ref, shg_out, sgla_out, sre_out, sim_out,
                   xs_s, p_ref, cma_ref, cmb_ref, fa_s, gb_s, mixt_ref, sa_s, sb_s, lbc_s):
    l = pl.program_id(0)
    h = pl.program_id(1)

    @pl.when((l == 0) & (h == 0))
    def _():
        xs_s[...] = x_ref[...]

    @pl.when(h == 0)
    def _():
        _lower_bound_consts(lbl_ref, l, lbc_s)
        _project(xs_s[...], ng_ref, win_ref, p_ref)
        cma_ref[...] = p_ref[:, :A_CM].T
        cmb_ref[...] = p_ref[:, OFF["qb"]:OFF["qb"] + B_CM].T
        log_f, k_a = _hgrn_gates(cma_ref[A_WIDTH:2 * A_WIDTH, :], lbc_s)
        fa_s[...] = jnp.exp(log_f)
        cma_ref[A_WIDTH:2 * A_WIDTH, :] = k_a
        cma_ref[:A_WIDTH, :] = jax.nn.silu(cma_ref[:A_WIDTH, :])
        gb_s[...] = jnp.exp(_gla_log_gate_cm(p_ref[:, _cols("rb")], w2_ref, b2_ref))

    sa_s[...] = shg_in[...].T
    ra = pl.multiple_of(h * A_DK, A_DK)
    v_a = cma_ref[pl.ds(2 * A_WIDTH + ra, A_DV), :]

    def hg_body(k, o):
        r = ra + k
        rows = pl.ds(pl.multiple_of(k * A_DV, A_DV), A_DV)
        new = fa_s[pl.ds(r, 1), :] * sa_s[rows, :] + cma_ref[pl.ds(A_WIDTH + r, 1), :] * v_a
        sa_s[rows, :] = new
        return o + cma_ref[pl.ds(r, 1), :] * new

    o = lax.fori_loop(0, A_DK, hg_body, jnp.zeros((A_DV, LANES), F32))
    mixt_ref[pl.ds(ra, A_DV), :] = _head_rms(o)
    shg_out[...] = sa_s[...].T

    sb_s[...] = sgla_in[...].T
    rb = pl.multiple_of(h * B_DK, B_DK)
    vb0 = pl.multiple_of(h * B_DV, B_DV)
    v_b = cmb_ref[pl.ds(2 * B_QK + vb0, B_DV), :]

    def gla_body(k, o):
        r = rb + k
        rows = pl.ds(pl.multiple_of(k * B_DV, B_DV), B_DV)
        new = gb_s[pl.ds(r, 1), :] * sb_s[rows, :] + cmb_ref[pl.ds(B_QK + r, 1), :] * v_b
        sb_s[rows, :] = new
        return o + (cmb_ref[pl.ds(r, 1), :] * (B_DK ** -0.5)) * new

    o = lax.fori_loop(0, B_DK, gla_body, jnp.zeros((B_DV, LANES), F32))
    mixt_ref[pl.ds(A_WIDTH + vb0, B_DV), :] = _head_rms(o)
    sgla_out[...] = sb_s[...].T

    @pl.when(h == pl.num_programs(1) - 1)
    def _():
        u = p_ref[:, _cols("uc")]
        bu = _dot(u.astype(BF16), bd_ref[...])
        x_re, x_im = _cmul_add(aseg_ref[0, :1], aseg_ref[1, :1], sre_in[...], sim_in[...],
                               bu[:, :S5_DIM], bu[:, S5_DIM:])
        sre_out[...] = x_re
        sim_out[...] = x_im
        oc = _s5_gate_tokens(_s5_readout(x_re, x_im, cb_ref), u, p_ref[:, _cols("zc")],
                             dd_ref, wglu_ref, bglu_ref)
        out = _mix_and_project(
            xs_s[...], mixt_ref, p_ref[:, _cols("za")], p_ref[:, _cols("zb")], oc,
            hgain_ref, ggain_ref, wout_ref, fg_ref, False)
        xs_s[...] = out

        @pl.when(l == pl.num_programs(0) - 1)
        def _():
            y_ref[...] = _rmsnorm_rows(out, fg_ref[...])


def _lane_bcast(v):
    return jnp.broadcast_to(v.astype(F32)[..., None], v.shape + (LANES,))


def _row(v):
    return v.astype(F32).reshape(1, -1)


def _group_block_diag(m):
    g, r, c = m.shape
    eye = jnp.eye(g, dtype=m.dtype)
    return (m[:, :, None, :] * eye[:, None, :, None]).reshape(g * r, g * c)


def _s5_tables(A_re, A_im, B_re, B_im, C_re, C_im, log_dt):
    dt = jnp.exp(log_dt.astype(F32))[:, None]
    lr, li = A_re.astype(F32), A_im.astype(F32)
    mag = jnp.exp(lr * dt)
    ab_re = mag * jnp.cos(li * dt)
    ab_im = mag * jnp.sin(li * dt)
    den = lr * lr + li * li
    nr = ab_re - 1.0
    co_re = (nr * lr + ab_im * li) / den
    co_im = (ab_im * lr - nr * li) / den
    Br, Bi = B_re.astype(F32), B_im.astype(F32)
    bb_re = co_re[..., None] * Br - co_im[..., None] * Bi
    bb_im = co_re[..., None] * Bi + co_im[..., None] * Br
    bd = jnp.concatenate([_group_block_diag(bb_re.transpose(0, 2, 1)),
                          _group_block_diag(bb_im.transpose(0, 2, 1))], axis=1).astype(BF16)
    cb = jnp.concatenate([_group_block_diag(C_re.astype(F32).transpose(0, 2, 1)),
                          _group_block_diag(-C_im.astype(F32).transpose(0, 2, 1))], axis=0).astype(BF16)
    ns = np.concatenate([np.arange(1, SEG + 1), np.arange(2 * SEG, LANES + 1, SEG)]).astype(np.float32)
    mg = jnp.exp(ns[:, None] * (lr * dt).reshape(-1)[None, :])
    ang = ns[:, None] * (li * dt).reshape(-1)[None, :]
    pw = jnp.stack([mg * jnp.cos(ang), mg * jnp.sin(ang)])
    aseg = jnp.broadcast_to(pw[:, :1], (2, SUBLANES, S5_DIM))
    sub = jnp.arange(SUBLANES)[None, :, None]
    row_of = {int(n): r for r, n in enumerate(ns)}
    tsub = jnp.stack([jnp.where(sub >= (1 << i), pw[:, row_of[SEG << i]][:, None, :], 0.0) for i in range(3)])
    tcar = pw[:, SEG - 1:]
    tpos = jnp.repeat(pw[:, :SEG], SUBLANES, axis=1)
    return bd, cb, aseg, tsub, tcar, tpos


def _layer_spec(a, layer):
    nd = a.ndim - 1
    if layer is None:
        return pl.BlockSpec((None,) + a.shape[1:], lambda l, *_, nd=nd: (l,) + (0,) * nd)
    return pl.BlockSpec((None,) + a.shape[1:], lambda *_, nd=nd: (layer,) + (0,) * nd)


def _const_spec(a):
    nd = a.ndim
    return pl.BlockSpec(a.shape, lambda *_, nd=nd: (0,) * nd)


def _operands(norm_g, w_in, hg_lb_logits, hg_norm_g, gla_w2, gla_b2, gla_norm_g,
              s5_A_re, s5_A_im, s5_B_re, s5_B_im, s5_C_re, s5_C_im, s5_D, s5_log_dt,
              s5_w_glu, s5_b_glu, w_out, final_norm_g):
    bd, cb, aseg, tsub, tcar, tpos = jax.vmap(_s5_tables)(
        s5_A_re, s5_A_im, s5_B_re, s5_B_im, s5_C_re, s5_C_im, s5_log_dt)
    pad = 2 * LANES - B_QK
    rows = lambda v: v.astype(F32)[:, None, :]
    common = (
        (rows(norm_g), True),
        (w_in.astype(BF16), True),
        (_lane_bcast(hg_lb_logits), False),
        (rows(hg_norm_g), True),
        (jnp.pad(gla_w2, ((0, 0), (0, 0), (0, pad))).astype(BF16), True),
        (jnp.pad(rows(gla_b2), ((0, 0), (0, 0), (0, pad))), True),
        (rows(gla_norm_g), True),
        (bd, True), (cb, True),
        (rows(s5_D), True),
        (s5_w_glu.astype(BF16), True),
        (rows(s5_b_glu), True),
        (w_out.astype(BF16), True),
        (_row(final_norm_g), False),
        (aseg, True),
    )
    return common, ((tsub, True), (tcar, True), (tpos, True))


def _prompt_layer(layer, final, x, common, scan_tables, chunk_tables):
    bsz, t, _ = x.shape
    assert t % LANES == 0 and bsz % STREAMS == 0
    nb, nt = bsz // STREAMS, t // LANES
    consts = common + scan_tables + tuple((a, False) for a in chunk_tables)
    const_specs = [_layer_spec(a, layer) if stacked else _const_spec(a) for a, stacked in consts]
    out_shape = (
        jax.ShapeDtypeStruct((bsz, t, D_MODEL), F32),
        jax.ShapeDtypeStruct((bsz, A_HEADS, A_DK, A_DV), F32),
        jax.ShapeDtypeStruct((bsz, B_HEADS, B_DK, B_DV), F32),
        jax.ShapeDtypeStruct((bsz, 1, S5_DIM), F32),
        jax.ShapeDtypeStruct((bsz, 1, S5_DIM), F32),
    )
    x_spec = pl.BlockSpec((STREAMS, LANES, D_MODEL), lambda b, i: (b, i, 0))

    def next_block(b, i):
        nxt = jnp.minimum(b * nt + i + 1, nb * nt - 1)
        return (nxt // nt, nxt % nt, 0)

    out_specs = (
        x_spec,
        pl.BlockSpec((STREAMS, A_HEADS, A_DK, A_DV), lambda b, i: (b, 0, 0, 0)),
        pl.BlockSpec((STREAMS, B_HEADS, B_DK, B_DV), lambda b, i: (b, 0, 0, 0)),
        pl.BlockSpec((STREAMS, 1, S5_DIM), lambda b, i: (b, 0, 0)),
        pl.BlockSpec((STREAMS, 1, S5_DIM), lambda b, i: (b, 0, 0)),
    )
    scratch = [
        pltpu.VMEM((2, STREAMS * LANES, IN_TOTAL), F32),
        pltpu.VMEM((STREAMS * LANES, D_MODEL), BF16),
        pltpu.VMEM((STREAMS, A_CM, LANES), F32),
        pltpu.VMEM((STREAMS, B_CM, LANES), F32),
        pltpu.VMEM((STREAMS, A_WIDTH, N_WIN * LANES), F32),
        pltpu.VMEM((STREAMS, A_WIDTH + B_WIDTH, LANES), F32),
        pltpu.VMEM((STREAMS, A_WIDTH, A_DV), F32),
        pltpu.VMEM((STREAMS, B_QK, B_DV), F32),
        pltpu.VMEM((STREAMS, SUBLANES, S5_DIM), F32),
        pltpu.VMEM((STREAMS, SUBLANES, S5_DIM), F32),
        pltpu.VMEM((3, A_WIDTH, LANES), F32),
    ]
    return pl.pallas_call(
        functools.partial(_prompt_kernel, layer, final),
        grid=(nb, nt),
        in_specs=[x_spec, pl.BlockSpec((STREAMS, LANES, D_MODEL), next_block)] + const_specs,
        out_specs=out_specs,
        out_shape=out_shape,
        scratch_shapes=scratch,
        compiler_params=pltpu.CompilerParams(
            dimension_semantics=("arbitrary", "arbitrary"), vmem_limit_bytes=VMEM_LIMIT),
    )(x, x, *(a for a, _ in consts))


def _sample_layers(xs, common, shg, sgla, sre, sim):
    n = xs.shape[0]
    depth = shg.shape[0]
    assert n == LANES
    hg_cols, gla_cols = A_DK * A_DV, B_DK * B_DV
    state_shapes = [s.shape for s in (shg, sgla, sre, sim)]
    out_shape = [jax.ShapeDtypeStruct((n, D_MODEL), F32)] + [jax.ShapeDtypeStruct(s, F32) for s in state_shapes]
    full = pl.BlockSpec((n, D_MODEL), lambda l, h: (0, 0))
    state_specs = [
        pl.BlockSpec((None, n, hg_cols), lambda l, h: (l, 0, h)),
        pl.BlockSpec((None, n, gla_cols), lambda l, h: (l, 0, h)),
        pl.BlockSpec((None, n, S5_DIM), lambda l, h: (l, 0, 0)),
        pl.BlockSpec((None, n, S5_DIM), lambda l, h: (l, 0, 0)),
    ]
    scratch = [
        pltpu.VMEM((n, D_MODEL), F32),
        pltpu.VMEM((n, IN_TOTAL), F32),
        pltpu.VMEM((A_CM, LANES), F32),
        pltpu.VMEM((B_CM, LANES), F32),
        pltpu.VMEM((A_WIDTH, LANES), F32),
        pltpu.VMEM((B_QK, LANES), F32),
        pltpu.VMEM((A_WIDTH + B_WIDTH, LANES), F32),
        pltpu.VMEM((hg_cols, LANES), F32),
        pltpu.VMEM((gla_cols, LANES), F32),
        pltpu.VMEM((3, A_WIDTH, LANES), F32),
    ]
    return pl.pallas_call(
        _sample_kernel,
        grid=(depth, A_HEADS),
        in_specs=([full] + [_layer_spec(a, None) if stacked else _const_spec(a) for a, stacked in common]
                  + state_specs),
        out_specs=[full] + state_specs,
        out_shape=out_shape,
        scratch_shapes=scratch,
        compiler_params=pltpu.CompilerParams(
            dimension_semantics=("arbitrary", "arbitrary"), vmem_limit_bytes=VMEM_LIMIT),
    )(xs, *(a for a, _ in common), shg, sgla, sre, sim)


def kernel(x_prompt, x_sample, state_hgrn, state_gla, state_s5_re, state_s5_im, norm_g, w_in, hg_lb_logits, hg_norm_g, gla_w2, gla_b2, gla_norm_g, s5_A_re, s5_A_im, s5_B_re, s5_B_im, s5_C_re, s5_C_im, s5_D, s5_log_dt, s5_w_glu, s5_b_glu, w_out, final_norm_g):
    depth = w_in.shape[0]
    bsz = x_prompt.shape[0]
    n = x_sample.shape[0]
    assert x_sample.shape[1] == 1 and A_HEADS == B_HEADS
    win_np, lv_np, perm_np = _chunk_tables()
    chunk_tables = (jnp.asarray(perm_np, BF16), jnp.asarray(win_np, BF16), jnp.asarray(lv_np))
    common, scan_tables = _operands(
        norm_g, w_in, hg_lb_logits, hg_norm_g, gla_w2, gla_b2, gla_norm_g,
        s5_A_re, s5_A_im, s5_B_re, s5_B_im, s5_C_re, s5_C_im, s5_D, s5_log_dt,
        s5_w_glu, s5_b_glu, w_out, final_norm_g)

    yp = x_prompt.astype(F32)
    prompt_states = []
    for l in range(depth):
        yp, *states = _prompt_layer(l, l == depth - 1, yp, common, scan_tables, chunk_tables)
        prompt_states.append(states)
    hg_p, gla_p, re_p, im_p = (jnp.stack([s[i] for s in prompt_states], axis=0) for i in range(4))

    flat = lambda s: s.astype(F32).reshape(depth, n, -1)
    ys, hg_s, gla_s, re_s, im_s = _sample_layers(
        x_sample.astype(F32).reshape(n, D_MODEL), common,
        flat(state_hgrn), flat(state_gla), flat(state_s5_re), flat(state_s5_im))

    return (yp, ys.reshape(n, 1, D_MODEL),
            hg_p, gla_p, re_p.reshape(depth, bsz, C_GROUPS, C_STATE), im_p.reshape(depth, bsz, C_GROUPS, C_STATE),
            hg_s.reshape(state_hgrn.shape), gla_s.reshape(state_gla.shape),
            re_s.reshape(state_s5_re.shape), im_s.reshape(state_s5_im.shape))
```

```python
import functools

import numpy as np
import jax
import jax.numpy as jnp
from jax import lax
from jax.experimental import pallas as pl
from jax.experimental.pallas import tpu as pltpu

F32 = jnp.float32
BF16 = jnp.bfloat16

D_MODEL = 1024
A_WIDTH, A_DK, A_DV, A_HEADS = 384, 64, 64, 6
B_WIDTH, B_DV, B_HEADS, B_DK, B_QK = 384, 64, 6, 32, 192
GLA_RANK = 16
GLA_TAU = 16.0
C_WIDTH, C_GROUP, C_GROUPS, C_STATE = 256, 16, 16, 64
S5_DIM = C_GROUPS * C_STATE
NORM_EPS = 1e-6
BLOCKS = ("qa", "fa", "ia", "za", "qb", "kb", "vb", "zb", "rb", "uc", "zc")
BLOCK_SIZES = (A_WIDTH, A_WIDTH, A_WIDTH, A_WIDTH, B_QK, B_QK, B_WIDTH, B_WIDTH, GLA_RANK, C_WIDTH, C_WIDTH)
IN_TOTAL = sum(BLOCK_SIZES)
_SIZE = dict(zip(BLOCKS, BLOCK_SIZES))
OFF = dict(zip(BLOCKS, np.concatenate([[0], np.cumsum(BLOCK_SIZES)[:-1]]).tolist()))
A_CM = 3 * A_WIDTH
B_CM = 2 * B_QK + B_WIDTH

LANES = 128
SUBLANES = 8
N_LEVELS = 7
N_WIN = N_LEVELS + 1
SEG = LANES // SUBLANES
STREAMS = 4
PROJ_COLS = 256
PROJ_GAP = 7
VMEM_LIMIT = 56 * 1024 * 1024


def _chunk_tables():
    s = np.arange(LANES)[:, None]
    t = np.arange(LANES)[None, :]
    wins = []
    for l in range(N_LEVELS):
        h = (LANES // 2) >> l
        mid = (t // (2 * h)) * (2 * h) + h - 1
        role_q = (t & h) != 0
        wins.append(np.where(role_q, (s > mid) & (s <= t), (s > t) & (s <= mid)))
    wins.append(s <= t)
    win = np.concatenate(wins, axis=1).astype(np.float32)
    x = s ^ t
    lv = np.full((LANES, LANES), -1, np.int32)
    for l in range(N_LEVELS):
        h = (LANES // 2) >> l
        lv[(x >= h) & (x < 2 * h) & (t > s)] = l
    perm = np.zeros((LANES, LANES), np.float32)
    i, j = np.divmod(np.arange(LANES), SEG)
    perm[SUBLANES * j + i, np.arange(LANES)] = 1.0
    return np.concatenate([win, win], axis=0), np.tile(lv, (1, 2)), np.stack([perm, perm.T])


def _dot(a, b):
    return jnp.dot(a, b, preferred_element_type=F32)


def _dot_nt(a, b):
    return lax.dot_general(a, b, (((1,), (1,)), ((), ())), preferred_element_type=F32)


def _rmsnorm_rows(x, g):
    r = lax.rsqrt(jnp.mean(x * x, axis=-1, keepdims=True) + NORM_EPS)
    return x * r * g


def _lower_bound_consts(lbl_ref, layer, out_ref):
    rows = [lbl_ref[l] for l in range(lbl_ref.shape[0])]
    m = functools.reduce(jnp.maximum, rows)
    es = [jnp.exp(r - m) for r in rows]
    tot = functools.reduce(lambda a, b: a + b, es)
    sm = [e / tot for e in es]
    cum, lbs = sm[0], [sm[0] - sm[0]]
    for l in range(1, len(sm)):
        cum = cum + sm[l]
        lbs.append(cum - sm[0])
    if isinstance(layer, int):
        lb = lbs[layer]
    else:
        lb = lbs[0]
        for l in range(1, len(lbs)):
            lb = jnp.where(layer == l, lbs[l], lb)
    out_ref[0] = jnp.log(lb)
    out_ref[1] = jnp.log(1.0 - lb)
    out_ref[2] = 1.0 - lb


def _hgrn_gates(fa, lbc_ref):
    e = jnp.exp(-jnp.abs(fa))
    logsig = jnp.minimum(fa, 0.0) - jnp.log(1.0 + e)
    a = lbc_ref[0]
    c = lbc_ref[1] + logsig
    log_f = jnp.maximum(a, c) + jnp.log(1.0 + jnp.exp(-jnp.abs(a - c)))
    k_a = lbc_ref[2] * (jnp.where(fa >= 0.0, e, 1.0) / (1.0 + e))
    return log_f, k_a


def _gla_log_gate_cm(rb, w2_ref, b2_ref):
    pre = _dot(rb.astype(BF16), w2_ref[...]) + b2_ref[...]
    return (jax.nn.log_sigmoid(pre) / GLA_TAU).T[:B_QK]


def _window_sums(g, win_ref):
    hi = g.astype(BF16)
    lo = (g - hi.astype(F32)).astype(BF16)
    return _dot(jnp.concatenate([hi, lo], axis=1), win_ref[...])


def _gated_chunk(load_q, load_k, load_v, e_ref, s_ref, heads, dk, dv, lv_ref, lane, emit):
    for p in range(heads // 2):
        rows = slice(2 * p * dk, (2 * p + 2) * dk)
        q, k = load_q(rows), load_k(rows)
        att = [jnp.zeros((SUBLANES, 2 * LANES), F32)] * (LANES // SUBLANES)
        for l in range(N_LEVELS):
            h = (LANES // 2) >> l
            z = jnp.where((lane & h) != 0, q, k) * jnp.exp(e_ref[rows, l * LANES:(l + 1) * LANES])
            zb = z.astype(BF16)
            zero = jnp.zeros((dk, LANES), BF16)
            rhs = jnp.concatenate([jnp.concatenate([zb[:dk], zero], axis=1),
                                   jnp.concatenate([zero, zb[dk:]], axis=1)], axis=0)
            zt = z.T
            blocks = [b for b in range(LANES // SUBLANES) if h < SUBLANES or (b * SUBLANES) & h == 0]
            lhs = jnp.concatenate([zt[b * SUBLANES:(b + 1) * SUBLANES] for b in blocks], axis=0)
            gram = _dot(lhs.astype(BF16), rhs)
            for i, b in enumerate(blocks):
                sl = slice(b * SUBLANES, (b + 1) * SUBLANES)
                att[b] = jnp.where(lv_ref[sl, :] == l, gram[i * SUBLANES:(i + 1) * SUBLANES], att[b])
            yield
        attb = jnp.concatenate(att, axis=0).astype(BF16)
        bq = e_ref[rows, N_LEVELS * LANES:(N_LEVELS + 1) * LANES]
        b_last = bq[:, LANES - 1:LANES]
        qs = (q * jnp.exp(bq)).astype(BF16)
        kd = (k * jnp.exp(b_last - bq)).astype(BF16)
        dec = jnp.exp(b_last)
        qk = q * k
        for i in range(2):
            hd = 2 * p + i
            r = slice(i * dk, (i + 1) * dk)
            rs = slice(hd * dk, (hd + 1) * dk)
            v = load_v(slice(hd * dv, (hd + 1) * dv))
            vb = v.astype(BF16)
            s_old = s_ref[rs, :]
            o = _dot(s_old.T.astype(BF16), qs[r])
            o = o + _dot(vb, attb[:, i * LANES:(i + 1) * LANES])
            o = o + jnp.sum(qk[r], axis=0, keepdims=True) * v
            s_ref[rs, :] = dec[r] * s_old + _dot_nt(kd[r], vb)
            emit(hd, o)
            yield


def _head_rms(o):
    return o * lax.rsqrt(jnp.mean(o * o, axis=0, keepdims=True) + NORM_EPS)


def _cmul_add(ar, ai, xr, xi, br, bi):
    return br + ar * xr - ai * xi, bi + ar * xi + ai * xr


def _s5_readout(x_re, x_im, cb_ref):
    return _dot(jnp.concatenate([x_re, x_im], axis=1).astype(BF16), cb_ref[...])


def _s5_gate_tokens(y, u, zc, d_ref, wglu_ref, bglu_ref):
    y = y + d_ref[...] * u
    y = jax.nn.gelu(y)
    y = y * jax.nn.sigmoid(_dot(y.astype(BF16), wglu_ref[...]) + bglu_ref[...])
    return y * jax.nn.silu(zc)


def _project(x, ng_ref, win_ref, p_ref):
    h = _rmsnorm_rows(x, ng_ref[...])
    p_ref[...] = _dot(h.astype(BF16), win_ref[...])


def _project_steps(x, ng_ref, win_ref, hb_ref, p_ref):
    hb_ref[...] = _rmsnorm_rows(x, ng_ref[...]).astype(BF16)
    for c0 in range(0, IN_TOTAL, PROJ_COLS):
        for _ in range(PROJ_GAP + 1):
            yield
        cols = slice(c0, min(c0 + PROJ_COLS, IN_TOTAL))
        p_ref[:, cols] = _dot(hb_ref[...], win_ref[:, cols])


def _mix_and_project(x, mixt_ref, za, zb, oc, hgain_ref, ggain_ref, wout_ref, fg_ref, final):
    o_t = mixt_ref[...].T
    o_a = o_t[:, :A_WIDTH] * hgain_ref[...] * jax.nn.silu(za)
    o_b = o_t[:, A_WIDTH:] * ggain_ref[...] * jax.nn.silu(zb)
    mix = jnp.concatenate([o_a, o_b, oc], axis=1).astype(BF16)
    out = x + _dot(mix, wout_ref[...])
    if final:
        out = _rmsnorm_rows(out, fg_ref[...])
    return out


def _cols(name):
    return slice(OFF[name], OFF[name] + _SIZE[name])


def _interleave(gens):
    live = list(gens)
    while live:
        live = [g for g in live if next(g, StopIteration) is not StopIteration]


def _prompt_kernel(layer, final,
                   x_ref, xn_ref, ng_ref, win_ref, lbl_ref, hgain_ref, w2_ref, b2_ref, ggain_ref,
                   bd_ref, cb_ref, dd_ref, wglu_ref, bglu_ref, wout_ref, fg_ref, aseg_ref,
                   tsub_ref, tcar_ref, tpos_ref, perm_ref, wwin_ref, lv_ref,
                   y_ref, shg_ref, sgla_ref, sre_ref, sim_ref,
                   p_ref, hb_ref, cma_ref, cmb_ref, work_ref, mixt_ref,
                   hg_s, gla_s, cre_s, cim_s, lbc_s):
    t_idx = pl.program_id(1)
    step = pl.program_id(0) * pl.num_programs(1) + t_idx
    cur = step % 2

    @pl.when(t_idx == 0)
    def _():
        hg_s[...] = jnp.zeros_like(hg_s)
        gla_s[...] = jnp.zeros_like(gla_s)
        cre_s[...] = jnp.zeros_like(cre_s)
        cim_s[...] = jnp.zeros_like(cim_s)
        _lower_bound_consts(lbl_ref, layer, lbc_s)

    @pl.when(step == 0)
    def _():
        _project(x_ref[...].reshape(STREAMS * LANES, D_MODEL), ng_ref, win_ref, p_ref.at[0])

    lane = lax.broadcasted_iota(jnp.int32, (1, LANES), 1)

    def chunk_steps(k):
        p_cur = p_ref.at[cur, k * LANES:(k + 1) * LANES]
        cma, cmb, mixt = (r.at[k] for r in (cma_ref, cmb_ref, mixt_ref))
        ea = work_ref.at[k]
        eb = work_ref.at[k, :B_QK]
        bu_re, bu_im = work_ref.at[k, :LANES], work_ref.at[k, LANES:2 * LANES]
        hg, gla, cre, cim = (r.at[k] for r in (hg_s, gla_s, cre_s, cim_s))
        cma[...] = p_cur[:, :A_CM].T
        cmb[...] = p_cur[:, OFF["qb"]:OFF["qb"] + B_CM].T
        yield

        log_f, k_a = _hgrn_gates(cma[A_WIDTH:2 * A_WIDTH, :], lbc_s)
        cma[A_WIDTH:2 * A_WIDTH, :] = k_a
        ea[...] = _window_sums(log_f, wwin_ref)
        yield

        def emit_a(hd, o):
            mixt[hd * A_DV:(hd + 1) * A_DV, :] = _head_rms(o)

        yield from _gated_chunk(lambda r: jax.nn.silu(cma[r, :]),
                                lambda r: cma[A_WIDTH + r.start:A_WIDTH + r.stop, :],
                                lambda r: cma[2 * A_WIDTH + r.start:2 * A_WIDTH + r.stop, :],
                                ea, hg, A_HEADS, A_DK, A_DV, lv_ref, lane, emit_a)

        eb[...] = _window_sums(_gla_log_gate_cm(p_cur[:, _cols("rb")], w2_ref, b2_ref), wwin_ref)
        yield

        def emit_b(hd, o):
            mixt[A_WIDTH + hd * B_DV:A_WIDTH + (hd + 1) * B_DV, :] = _head_rms(o)

        yield from _gated_chunk(lambda r: cmb[r, :] * (B_DK ** -0.5),
                                lambda r: cmb[B_QK + r.start:B_QK + r.stop, :],
                                lambda r: cmb[2 * B_QK + r.start:2 * B_QK + r.stop, :],
                                eb, gla, B_HEADS, B_DK, B_DV, lv_ref, lane, emit_b)

        u = p_cur[:, _cols("uc")]
        u_seg = _dot(perm_ref[0], u.astype(BF16)).astype(BF16)
        b = _dot(u_seg, bd_ref[...])
        bu_re[...] = b[:, :S5_DIM]
        bu_im[...] = b[:, S5_DIM:]
        yield
        ar, ai = aseg_ref[0], aseg_ref[1]
        xr = jnp.zeros((SUBLANES, S5_DIM), F32)
        xi = jnp.zeros((SUBLANES, S5_DIM), F32)
        for j in range(SEG):
            rows = slice(j * SUBLANES, (j + 1) * SUBLANES)
            xr, xi = _cmul_add(ar, ai, xr, xi, bu_re[rows, :], bu_im[rows, :])
            bu_re[rows, :] = xr
            bu_im[rows, :] = xi
            yield
        for i in range(3):
            xr, xi = _cmul_add(tsub_ref[i, 0], tsub_ref[i, 1],
                               pltpu.roll(xr, 1 << i, 0), pltpu.roll(xi, 1 << i, 0), xr, xi)
        cr, ci = cre[...], cim[...]
        xr, xi = _cmul_add(tcar_ref[0], tcar_ref[1], cr, ci, xr, xi)
        first = lax.broadcasted_iota(jnp.int32, (SUBLANES, S5_DIM), 0) == 0
        pr = jnp.where(first, cr, pltpu.roll(xr, 1, 0))
        pi = jnp.where(first, ci, pltpu.roll(xi, 1, 0))
        cre[...] = jnp.broadcast_to(xr[SUBLANES - 1:], (SUBLANES, S5_DIM))
        cim[...] = jnp.broadcast_to(xi[SUBLANES - 1:], (SUBLANES, S5_DIM))
        yield
        for j in range(SEG):
            rows = slice(j * SUBLANES, (j + 1) * SUBLANES)
            tr, ti = tpos_ref[0, rows, :], tpos_ref[1, rows, :]
            fr, fi = _cmul_add(tr, ti, pr, pi, bu_re[rows, :], bu_im[rows, :])
            bu_re[rows, :] = fr
            bu_im[rows, :] = fi
            yield
        y_seg = _s5_readout(bu_re[...], bu_im[...], cb_ref)
        y1 = y_seg.astype(BF16)
        r1 = y_seg - y1.astype(F32)
        y2 = r1.astype(BF16)
        y3 = (r1 - y2.astype(F32)).astype(BF16)
        yield
        yp = _dot(perm_ref[1], jnp.concatenate([y1, y2, y3], axis=1))
        y_tok = yp[:, :C_WIDTH] + yp[:, C_WIDTH:2 * C_WIDTH] + yp[:, 2 * C_WIDTH:]
        yield
        oc = _s5_gate_tokens(y_tok, u, p_cur[:, _cols("zc")], dd_ref, wglu_ref, bglu_ref)
        yield
        y_ref[k] = _mix_and_project(
            x_ref[k], mixt, p_cur[:, _cols("za")], p_cur[:, _cols("zb")], oc,
            hgain_ref, ggain_ref, wout_ref, fg_ref, final)

    project_next = _project_steps(xn_ref[...].reshape(STREAMS * LANES, D_MODEL), ng_ref, win_ref,
                                  hb_ref, p_ref.at[1 - cur])
    _interleave([chunk_steps(k) for k in range(STREAMS)] + [project_next])

    @pl.when(t_idx == pl.num_programs(1) - 1)
    def _():
        for k in range(STREAMS):
            for h in range(A_HEADS):
                shg_ref[k, h] = hg_s[k, h * A_DK:(h + 1) * A_DK, :]
            for h in range(B_HEADS):
                sgla_ref[k, h] = gla_s[k, h * B_DK:(h + 1) * B_DK, :]
            sre_ref[k] = cre_s[k, :1, :]
            sim_ref[k] = cim_s[k, :1, :]


def _sample_kernel(x_ref, ng_ref, win_ref, lbl_ref, hgain_ref, w2_ref, b2_ref, ggain_ref,
                   bd_ref, cb_ref, dd_ref, wglu_ref, bglu_ref, wout_ref, fg_ref, aseg_ref,
                   shg_in, sgla_in, sre_in, sim_in,
                   y_ref, shg_out, sgla_out, sre_out, sim_out,
                   xs_s, p_ref, cma_ref, cmb_ref, fa_s, gb_s, mixt_ref, sa_s, sb_s, lbc_s):
    l = pl.program_id(0)
    h = pl.program_id(1)

    @pl.when((l == 0) & (h == 0))
    def _():
        xs_s[...] = x_ref[...]

    @pl.when(h == 0)
    def _():
        _lower_bound_consts(lbl_ref, l, lbc_s)
        _project(xs_s[...], ng_ref, win_ref, p_ref)
        cma_ref[...] = p_ref[:, :A_CM].T
        cmb_ref[...] = p_ref[:, OFF["qb"]:OFF["qb"] + B_CM].T
        log_f, k_a = _hgrn_gates(cma_ref[A_WIDTH:2 * A_WIDTH, :], lbc_s)
        fa_s[...] = jnp.exp(log_f)
        cma_ref[A_WIDTH:2 * A_WIDTH, :] = k_a
        cma_ref[:A_WIDTH, :] = jax.nn.silu(cma_ref[:A_WIDTH, :])
        gb_s[...] = jnp.exp(_gla_log_gate_cm(p_ref[:, _cols("rb")], w2_ref, b2_ref))

    sa_s[...] = shg_in[...].T
    ra = pl.multiple_of(h * A_DK, A_DK)
    v_a = cma_ref[pl.ds(2 * A_WIDTH + ra, A_DV), :]

    def hg_body(k, o):
        r = ra + k
        rows = pl.ds(pl.multiple_of(k * A_DV, A_DV), A_DV)
        new = fa_s[pl.ds(r, 1), :] * sa_s[rows, :] + cma_ref[pl.ds(A_WIDTH + r, 1), :] * v_a
        sa_s[rows, :] = new
        return o + cma_ref[pl.ds(r, 1), :] * new

    o = lax.fori_loop(0, A_DK, hg_body, jnp.zeros((A_DV, LANES), F32))
    mixt_ref[pl.ds(ra, A_DV), :] = _head_rms(o)
    shg_out[...] = sa_s[...].T

    sb_s[...] = sgla_in[...].T
    rb = pl.multiple_of(h * B_DK, B_DK)
    vb0 = pl.multiple_of(h * B_DV, B_DV)
    v_b = cmb_ref[pl.ds(2 * B_QK + vb0, B_DV), :]

    def gla_body(k, o):
        r = rb + k
        rows = pl.ds(pl.multiple_of(k * B_DV, B_DV), B_DV)
        new = gb_s[pl.ds(r, 1), :] * sb_s[rows, :] + cmb_ref[pl.ds(B_QK + r, 1), :] * v_b
        sb_s[rows, :] = new
        return o + (cmb_ref[pl.ds(r, 1), :] * (B_DK ** -0.5)) * new

    o = lax.fori_loop(0, B_DK, gla_body, jnp.zeros((B_DV, LANES), F32))
    mixt_ref[pl.ds(A_WIDTH + vb0, B_DV), :] = _head_rms(o)
    sgla_out[...] = sb_s[...].T

    @pl.when(h == pl.num_programs(1) - 1)
    def _():
        u = p_ref[:, _cols("uc")]
        bu = _dot(u.astype(BF16), bd_ref[...])
        x_re, x_im = _cmul_add(aseg_ref[0, :1], aseg_ref[1, :1], sre_in[...], sim_in[...],
                               bu[:, :S5_DIM], bu[:, S5_DIM:])
        sre_out[...] = x_re
        sim_out[...] = x_im
        oc = _s5_gate_tokens(_s5_readout(x_re, x_im, cb_ref), u, p_ref[:, _cols("zc")],
                             dd_ref, wglu_ref, bglu_ref)
        out = _mix_and_project(
            xs_s[...], mixt_ref, p_ref[:, _cols("za")], p_ref[:, _cols("zb")], oc,
            hgain_ref, ggain_ref, wout_ref, fg_ref, False)
        xs_s[...] = out

        @pl.when(l == pl.num_programs(0) - 1)
        def _():
            y_ref[...] = _rmsnorm_rows(out, fg_ref[...])


def _lane_bcast(v):
    return jnp.broadcast_to(v.astype(F32)[..., None], v.shape + (LANES,))


def _row(v):
    return v.astype(F32).reshape(1, -1)


def _group_block_diag(m):
    g, r, c = m.shape
    eye = jnp.eye(g, dtype=m.dtype)
    return (m[:, :, None, :] * eye[:, None, :, None]).reshape(g * r, g * c)


def _s5_tables(A_re, A_im, B_re, B_im, C_re, C_im, log_dt):
    dt = jnp.exp(log_dt.astype(F32))[:, None]
    lr, li = A_re.astype(F32), A_im.astype(F32)
    mag = jnp.exp(lr * dt)
    ab_re = mag * jnp.cos(li * dt)
    ab_im = mag * jnp.sin(li * dt)
    den = lr * lr + li * li
    nr = ab_re - 1.0
    co_re = (nr * lr + ab_im * li) / den
    co_im = (ab_im * lr - nr * li) / den
    Br, Bi = B_re.astype(F32), B_im.astype(F32)
    bb_re = co_re[..., None] * Br - co_im[..., None] * Bi
    bb_im = co_re[..., None] * Bi + co_im[..., None] * Br
    bd = jnp.concatenate([_group_block_diag(bb_re.transpose(0, 2, 1)),
                          _group_block_diag(bb_im.transpose(0, 2, 1))], axis=1).astype(BF16)
    cb = jnp.concatenate([_group_block_diag(C_re.astype(F32).transpose(0, 2, 1)),
                          _group_block_diag(-C_im.astype(F32).transpose(0, 2, 1))], axis=0).astype(BF16)
    ns = np.concatenate([np.arange(1, SEG + 1), np.arange(2 * SEG, LANES + 1, SEG)]).astype(np.float32)
    mg = jnp.exp(ns[:, None] * (lr * dt).reshape(-1)[None, :])
    ang = ns[:, None] * (li * dt).reshape(-1)[None, :]
    pw = jnp.stack([mg * jnp.cos(ang), mg * jnp.sin(ang)])
    aseg = jnp.broadcast_to(pw[:, :1], (2, SUBLANES, S5_DIM))
    sub = jnp.arange(SUBLANES)[None, :, None]
    row_of = {int(n): r for r, n in enumerate(ns)}
    tsub = jnp.stack([jnp.where(sub >= (1 << i), pw[:, row_of[SEG << i]][:, None, :], 0.0) for i in range(3)])
    tcar = pw[:, SEG - 1:]
    tpos = jnp.repeat(pw[:, :SEG], SUBLANES, axis=1)
    return bd, cb, aseg, tsub, tcar, tpos


def _layer_spec(a, layer):
    nd = a.ndim - 1
    if layer is None:
        return pl.BlockSpec((None,) + a.shape[1:], lambda l, *_, nd=nd: (l,) + (0,) * nd)
    return pl.BlockSpec((None,) + a.shape[1:], lambda *_, nd=nd: (layer,) + (0,) * nd)


def _const_spec(a):
    nd = a.ndim
    return pl.BlockSpec(a.shape, lambda *_, nd=nd: (0,) * nd)


def _operands(norm_g, w_in, hg_lb_logits, hg_norm_g, gla_w2, gla_b2, gla_norm_g,
              s5_A_re, s5_A_im, s5_B_re, s5_B_im, s5_C_re, s5_C_im, s5_D, s5_log_dt,
              s5_w_glu, s5_b_glu, w_out, final_norm_g):
    bd, cb, aseg, tsub, tcar, tpos = jax.vmap(_s5_tables)(
        s5_A_re, s5_A_im, s5_B_re, s5_B_im, s5_C_re, s5_C_im, s5_log_dt)
    pad = 2 * LANES - B_QK
    rows = lambda v: v.astype(F32)[:, None, :]
    common = (
        (rows(norm_g), True),
        (w_in.astype(BF16), True),
        (_lane_bcast(hg_lb_logits), False),
        (rows(hg_norm_g), True),
        (jnp.pad(gla_w2, ((0, 0), (0, 0), (0, pad))).astype(BF16), True),
        (jnp.pad(rows(gla_b2), ((0, 0), (0, 0), (0, pad))), True),
        (rows(gla_norm_g), True),
        (bd, True), (cb, True),
        (rows(s5_D), True),
        (s5_w_glu.astype(BF16), True),
        (rows(s5_b_glu), True),
        (w_out.astype(BF16), True),
        (_row(final_norm_g), False),
        (aseg, True),
    )
    return common, ((tsub, True), (tcar, True), (tpos, True))


def _prompt_layer(layer, final, x, common, scan_tables, chunk_tables):
    bsz, t, _ = x.shape
    assert t % LANES == 0 and bsz % STREAMS == 0
    nb, nt = bsz // STREAMS, t // LANES
    consts = common + scan_tables + tuple((a, False) for a in chunk_tables)
    const_specs = [_layer_spec(a, layer) if stacked else _const_spec(a) for a, stacked in consts]
    out_shape = (
        jax.ShapeDtypeStruct((bsz, t, D_MODEL), F32),
        jax.ShapeDtypeStruct((bsz, A_HEADS, A_DK, A_DV), F32),
        jax.ShapeDtypeStruct((bsz, B_HEADS, B_DK, B_DV), F32),
        jax.ShapeDtypeStruct((bsz, 1, S5_DIM), F32),
        jax.ShapeDtypeStruct((bsz, 1, S5_DIM), F32),
    )
    x_spec = pl.BlockSpec((STREAMS, LANES, D_MODEL), lambda b, i: (b, i, 0))

    def next_block(b, i):
        nxt = jnp.minimum(b * nt + i + 1, nb * nt - 1)
        return (nxt // nt, nxt % nt, 0)

    out_specs = (
        x_spec,
        pl.BlockSpec((STREAMS, A_HEADS, A_DK, A_DV), lambda b, i: (b, 0, 0, 0)),
        pl.BlockSpec((STREAMS, B_HEADS, B_DK, B_DV), lambda b, i: (b, 0, 0, 0)),
        pl.BlockSpec((STREAMS, 1, S5_DIM), lambda b, i: (b, 0, 0)),
        pl.BlockSpec((STREAMS, 1, S5_DIM), lambda b, i: (b, 0, 0)),
    )
    scratch = [
        pltpu.VMEM((2, STREAMS * LANES, IN_TOTAL), F32),
        pltpu.VMEM((STREAMS * LANES, D_MODEL), BF16),
        pltpu.VMEM((STREAMS, A_CM, LANES), F32),
        pltpu.VMEM((STREAMS, B_CM, LANES), F32),
        pltpu.VMEM((STREAMS, A_WIDTH, N_WIN * LANES), F32),
        pltpu.VMEM((STREAMS, A_WIDTH + B_WIDTH, LANES), F32),
        pltpu.VMEM((STREAMS, A_WIDTH, A_DV), F32),
        pltpu.VMEM((STREAMS, B_QK, B_DV), F32),
        pltpu.VMEM((STREAMS, SUBLANES, S5_DIM), F32),
        pltpu.VMEM((STREAMS, SUBLANES, S5_DIM), F32),
        pltpu.VMEM((3, A_WIDTH, LANES), F32),
    ]
    return pl.pallas_call(
        functools.partial(_prompt_kernel, layer, final),
        grid=(nb, nt),
        in_specs=[x_spec, pl.BlockSpec((STREAMS, LANES, D_MODEL), next_block)] + const_specs,
        out_specs=out_specs,
        out_shape=out_shape,
        scratch_shapes=scratch,
        compiler_params=pltpu.CompilerParams(
            dimension_semantics=("arbitrary", "arbitrary"), vmem_limit_bytes=VMEM_LIMIT),
    )(x, x, *(a for a, _ in consts))


def _sample_layers(xs, common, shg, sgla, sre, sim):
    n = xs.shape[0]
    depth = shg.shape[0]
    assert n == LANES
    hg_cols, gla_cols = A_DK * A_DV, B_DK * B_DV
    state_shapes = [s.shape for s in (shg, sgla, sre, sim)]
    out_shape = [jax.ShapeDtypeStruct((n, D_MODEL), F32)] + [jax.ShapeDtypeStruct(s, F32) for s in state_shapes]
    full = pl.BlockSpec((n, D_MODEL), lambda l, h: (0, 0))
    state_specs = [
        pl.BlockSpec((None, n, hg_cols), lambda l, h: (l, 0, h)),
        pl.BlockSpec((None, n, gla_cols), lambda l, h: (l, 0, h)),
        pl.BlockSpec((None, n, S5_DIM), lambda l, h: (l, 0, 0)),
        pl.BlockSpec((None, n, S5_DIM), lambda l, h: (l, 0, 0)),
    ]
    scratch = [
        pltpu.VMEM((n, D_MODEL), F32),
        pltpu.VMEM((n, IN_TOTAL), F32),
        pltpu.VMEM((A_CM, LANES), F32),
        pltpu.VMEM((B_CM, LANES), F32),
        pltpu.VMEM((A_WIDTH, LANES), F32),
        pltpu.VMEM((B_QK, LANES), F32),
        pltpu.VMEM((A_WIDTH + B_WIDTH, LANES), F32),
        pltpu.VMEM((hg_cols, LANES), F32),
        pltpu.VMEM((gla_cols, LANES), F32),
        pltpu.VMEM((3, A_WIDTH, LANES), F32),
    ]
    return pl.pallas_call(
        _sample_kernel,
        grid=(depth, A_HEADS),
        in_specs=([full] + [_layer_spec(a, None) if stacked else _const_spec(a) for a, stacked in common]
                  + state_specs),
        out_specs=[full] + state_specs,
        out_shape=out_shape,
        scratch_shapes=scratch,
        compiler_params=pltpu.CompilerParams(
            dimension_semantics=("arbitrary", "arbitrary"), vmem_limit_bytes=VMEM_LIMIT),
    )(xs, *(a for a, _ in common), shg, sgla, sre, sim)


def kernel(x_prompt, x_sample, state_hgrn, state_gla, state_s5_re, state_s5_im, norm_g, w_in, hg_lb_logits, hg_norm_g, gla_w2, gla_b2, gla_norm_g, s5_A_re, s5_A_im, s5_B_re, s5_B_im, s5_C_re, s5_C_im, s5_D, s5_log_dt, s5_w_glu, s5_b_glu, w_out, final_norm_g):
    depth = w_in.shape[0]
    bsz = x_prompt.shape[0]
    n = x_sample.shape[0]
    assert x_sample.shape[1] == 1 and A_HEADS == B_HEADS
    win_np, lv_np, perm_np = _chunk_tables()
    chunk_tables = (jnp.asarray(perm_np, BF16), jnp.asarray(win_np, BF16), jnp.asarray(lv_np))
    common, scan_tables = _operands(
        norm_g, w_in, hg_lb_logits, hg_norm_g, gla_w2, gla_b2, gla_norm_g,
        s5_A_re, s5_A_im, s5_B_re, s5_B_im, s5_C_re, s5_C_im, s5_D, s5_log_dt,
        s5_w_glu, s5_b_glu, w_out, final_norm_g)

    yp = x_prompt.astype(F32)
    prompt_states = []
    for l in range(depth):
        yp, *states = _prompt_layer(l, l == depth - 1, yp, common, scan_tables, chunk_tables)
        prompt_states.append(states)
    hg_p, gla_p, re_p, im_p = (jnp.stack([s[i] for s in prompt_states], axis=0) for i in range(4))

    flat = lambda s: s.astype(F32).reshape(depth, n, -1)
    ys, hg_s, gla_s, re_s, im_s = _sample_layers(
        x_sample.astype(F32).reshape(n, D_MODEL), common,
        flat(state_hgrn), flat(state_gla), flat(state_s5_re), flat(state_s5_im))

    return (yp, ys.reshape(n, 1, D_MODEL),
            hg_p, gla_p, re_p.reshape(depth, bsz, C_GROUPS, C_STATE), im_p.reshape(depth, bsz, C_GROUPS, C_STATE),
            hg_s.reshape(state_hgrn.shape), gla_s.reshape(state_gla.shape),
            re_s.reshape(state_s5_re.shape), im_s.reshape(state_s5_im.shape))
```
